```python
import jax, jax.numpy as jnp
from jax import lax
import numpy as np

D_MODEL = 2048
BATCH = 2
SEQ = 4096
DEPTH = 4
DEC_BATCH = 8
DEC_SEQ = 4
PAST_LEN = 16384
PAGE_SIZE = 128

N_MIXERS = 2
N_HEADS = 16
HEAD_DIM = D_MODEL // N_HEADS
N_KV_HEADS = 8
KV_GROUP = N_HEADS // N_KV_HEADS
ATTN_W = N_HEADS * HEAD_DIM
KV_W = N_KV_HEADS * HEAD_DIM
IDX_HEADS = 16
IDX_DIM = 128
IDX_Q_W = IDX_HEADS * IDX_DIM
TOPK_MAX = 256
D_FF = -(-8 * D_MODEL // (3 * 256)) * 256
ROPE_THETA = 10000.0
EPS = 1e-6
Q_BLOCK = 128
ATTN_SCALE = HEAD_DIM ** -0.5
IDX_SCALE = (IDX_HEADS ** -0.5) * (IDX_DIM ** -0.5)
N_DSA = (DEPTH + 1) // 2
N_SB = DEPTH // 2
DSA_WIDTHS = (ATTN_W, KV_W, KV_W, IDX_Q_W, IDX_DIM, IDX_HEADS)
SB_WIDTHS = (ATTN_W, KV_W, KV_W)
DSA_IN = sum(DSA_WIDTHS)
SB_IN = sum(SB_WIDTHS)

kernel_name = 'hybrid_dsa_stickbreaking_adaln_decode_step'


def _rmsnorm(x, g):
    x32 = x.astype(jnp.float32)
    y = x32 * lax.rsqrt(jnp.mean(x32 * x32, axis=-1, keepdims=True) + EPS) * g.astype(jnp.float32)
    return y.astype(x.dtype)


def _adaln(c, w, b):
    m = jax.nn.silu(c) @ w + b
    return [t[:, None, :] for t in jnp.split(m, 6, axis=-1)]


def _rope(x, pos):
    half = x.shape[-1] // 2
    inv = ROPE_THETA ** (-jnp.arange(half, dtype=jnp.float32) / half)
    ang = pos.astype(jnp.float32)[:, None] * inv[None, :]
    cos = jnp.cos(ang)[:, None, :]
    sin = jnp.sin(ang)[:, None, :]
    x32 = x.astype(jnp.float32)
    x1, x2 = x32[..., :half], x32[..., half:]
    return jnp.concatenate([x1 * cos - x2 * sin, x2 * cos + x1 * sin], axis=-1).astype(x.dtype)


def _split_cols(p, widths):
    idx = [int(i) for i in np.cumsum(widths)[:-1]]
    return jnp.split(p, idx, axis=-1)


def _blocked(fn, *arrs):
    T = arrs[0].shape[1]
    qb = Q_BLOCK if T % Q_BLOCK == 0 else T
    nb = T // qb
    def split(a):
        return jnp.swapaxes(a.reshape(a.shape[0], nb, qb, *a.shape[2:]), 0, 1)
    out = lax.map(lambda xs: fn(*xs), tuple(split(a) for a in arrs))
    out = jnp.swapaxes(out, 0, 1)
    return out.reshape(out.shape[0], T, *out.shape[3:])


def _gather_rows(rows, sel):
    return jax.vmap(lambda r, s: r[s])(rows, sel)


def _gather_paged(pool, layer, new_rows, page_table, sel):
    past_len = page_table.shape[1] * PAGE_SIZE
    sp = jnp.minimum(sel, past_len - 1)
    phys = jax.vmap(lambda pt, s: pt[s // PAGE_SIZE])(page_table, sp)
    rows_past = pool[layer, phys, sp % PAGE_SIZE]
    rows_new = _gather_rows(new_rows, jnp.clip(sel - past_len, 0, new_rows.shape[1] - 1))
    return jnp.where((sel < past_len)[..., None, None], rows_past, rows_new)


def _dsa_project(h, w_in, pos):
    B, T, _ = h.shape
    q, k, v, iq, ik, iw = _split_cols(h @ w_in, DSA_WIDTHS)
    q = _rope(q.reshape(B, T, N_HEADS, HEAD_DIM), pos)
    k = _rope(k.reshape(B, T, N_KV_HEADS, HEAD_DIM), pos)
    v = v.reshape(B, T, N_KV_HEADS, HEAD_DIM)
    iq = _rope(iq.reshape(B, T, IDX_HEADS, IDX_DIM), pos)
    ik = _rope(ik.reshape(B, T, 1, IDX_DIM), pos)[:, :, 0]
    return q, k, v, iq, ik, iw


def _sb_project(h, w_in):
    B, T, _ = h.shape
    q, k, v = _split_cols(h @ w_in, SB_WIDTHS)
    return (q.reshape(B, T, N_HEADS, HEAD_DIM), k.reshape(B, T, N_KV_HEADS, HEAD_DIM),
            v.reshape(B, T, N_KV_HEADS, HEAD_DIM))


def _sparse_softmax_attend(q, kg, vg, valid):
    B, T = q.shape[:2]
    qg = q.reshape(B, T, N_KV_HEADS, KV_GROUP, HEAD_DIM)
    s = jnp.einsum('btkgd,btnkd->btkgn', qg, kg, preferred_element_type=jnp.float32) * ATTN_SCALE
    s = jnp.where(valid[:, :, None, None, :], s, -jnp.inf)
    p = jax.nn.softmax(s, axis=-1)
    o = jnp.einsum('btkgn,btnkd->btkgd', p.astype(vg.dtype), vg)
    return o.reshape(B, T, ATTN_W)


def _dsa_attend(q, iq, iw, qpos, ik_all, gather_kv):
    L = ik_all.shape[1]
    topk = min(TOPK_MAX, L // 4)
    kpos = jnp.arange(L)
    def block(qb, iqb, iwb, pb):
        logits = jnp.einsum('bthd,bsd->bths', iqb, ik_all, preferred_element_type=jnp.float32)
        score = jnp.einsum('bths,bth->bts', jax.nn.relu(logits), iwb.astype(jnp.float32)) * IDX_SCALE
        score = jnp.where(kpos[None, None, :] <= pb[:, :, None], score, -jnp.inf)
        _, sel = lax.top_k(score, topk)
        valid = sel <= pb[:, :, None]
        kg, vg = gather_kv(sel)
        return _sparse_softmax_attend(qb, kg, vg, valid)
    return _blocked(block, q, iq, iw, qpos)


def _stick_breaking_attend(q, qpos, k_all, v_all):
    L = k_all.shape[1]
    kpos = jnp.arange(L)
    def block(qb, pb):
        B, T = qb.shape[:2]
        qg = qb.reshape(B, T, N_KV_HEADS, KV_GROUP, HEAD_DIM)
        z = jnp.einsum('btkgd,bskd->bkgts', qg, k_all, preferred_element_type=jnp.float32) * ATTN_SCALE
        mask = (kpos[None, None, :] < pb[:, :, None])[:, None, None]
        log_keep = jnp.where(mask, jax.nn.log_sigmoid(-z), 0.0)
        log_after = lax.cumsum(log_keep, axis=4, reverse=True) - log_keep
        a = jnp.where(mask, jnp.exp(jax.nn.log_sigmoid(z) + log_after), 0.0)
        o = jnp.einsum('bkgts,bskd->btkgd', a.astype(v_all.dtype), v_all)
        return o.reshape(B, T, ATTN_W)
    return _blocked(block, q, qpos)


def _swiglu(h, wg, wu, wd):
    return (jax.nn.silu(h @ wg) * (h @ wu)) @ wd


def setup_inputs(seed: int = 0) -> dict:
    key = jax.random.key(seed)
    ks = jax.random.split(key, 20)
    f32 = jnp.float32
    n_pages = PAST_LEN // PAGE_SIZE
    n_used = DEC_BATCH * n_pages
    n_phys = n_used + (n_used + 3) // 4
    def nrm(k, shape, s):
        return jax.random.normal(k, shape, f32) * s
    x_prompt = nrm(ks[0], (BATCH, SEQ, D_MODEL), 1.0)
    x_sample = nrm(ks[1], (DEC_BATCH, DEC_SEQ, D_MODEL), 1.0)
    c_prompt = nrm(ks[2], (BATCH, D_MODEL), 1.0)
    c_sample = nrm(ks[3], (DEC_BATCH, D_MODEL), 1.0)
    cache_k = nrm(ks[4], (DEPTH, n_phys, PAGE_SIZE, N_KV_HEADS, HEAD_DIM), 1.0)
    cache_v = nrm(ks[5], (DEPTH, n_phys, PAGE_SIZE, N_KV_HEADS, HEAD_DIM), 1.0)
    cache_idx_k = nrm(ks[6], (N_DSA, n_phys, PAGE_SIZE, IDX_DIM), 1.0)
    page_table = jax.random.permutation(ks[7], n_phys)[:n_used].reshape(DEC_BATCH, n_pages).astype(jnp.int32)
    norm_mix_g = 1.0 + nrm(ks[8], (DEPTH, D_MODEL), 0.02)
    norm_ffn_g = 1.0 + nrm(ks[9], (DEPTH, D_MODEL), 0.02)
    w_ada = nrm(ks[10], (DEPTH, D_MODEL, 6 * D_MODEL), D_MODEL ** -0.5)
    b_ada = nrm(ks[11], (DEPTH, 6 * D_MODEL), 0.01)
    w_in_dsa = nrm(ks[12], (N_DSA, D_MODEL, DSA_IN), D_MODEL ** -0.5)
    w_in_sb = nrm(ks[13], (N_SB, D_MODEL, SB_IN), D_MODEL ** -0.5)
    w_out = nrm(ks[14], (DEPTH, ATTN_W, D_MODEL), ATTN_W ** -0.5)
    w_gate = nrm(ks[15], (DEPTH, D_MODEL, D_FF), D_MODEL ** -0.5)
    w_up = nrm(ks[16], (DEPTH, D_MODEL, D_FF), D_MODEL ** -0.5)
    w_down = nrm(ks[17], (DEPTH, D_FF, D_MODEL), D_FF ** -0.5)
    norm_final_g = 1.0 + nrm(ks[18], (D_MODEL,), 0.02)
    return {'x_prompt': x_prompt, 'x_sample': x_sample, 'c_prompt': c_prompt, 'c_sample': c_sample,
            'cache_k': cache_k, 'cache_v': cache_v, 'cache_idx_k': cache_idx_k, 'page_table': page_table,
            'norm_mix_g': norm_mix_g, 'norm_ffn_g': norm_ffn_g, 'w_ada': w_ada, 'b_ada': b_ada,
            'w_in_dsa': w_in_dsa, 'w_in_sb': w_in_sb, 'w_out': w_out, 'w_gate': w_gate, 'w_up': w_up,
            'w_down': w_down, 'norm_final_g': norm_final_g}


def reference(x_prompt, x_sample, c_prompt, c_sample, cache_k, cache_v, cache_idx_k, page_table,
              norm_mix_g, norm_ffn_g, w_ada, b_ada, w_in_dsa, w_in_sb, w_out, w_gate, w_up, w_down,
              norm_final_g):
    B, S, _ = x_prompt.shape
    DB, T, _ = x_sample.shape
    past_len = page_table.shape[1] * PAGE_SIZE
    pos_p = jnp.arange(S)
    pos_s = past_len + jnp.arange(T)
    qpos_p = jnp.broadcast_to(pos_p, (B, S))
    qpos_s = jnp.broadcast_to(pos_s, (DB, T))
    xp, xs = x_prompt, x_sample
    kp_l, vp_l, ikp_l, ks_l, vs_l, iks_l = [], [], [], [], [], []
    for l in range(DEPTH):
        sh1p, sc1p, g1p, sh2p, sc2p, g2p = _adaln(c_prompt, w_ada[l], b_ada[l])
        sh1s, sc1s, g1s, sh2s, sc2s, g2s = _adaln(c_sample, w_ada[l], b_ada[l])
        hp = _rmsnorm(xp, norm_mix_g[l]) * (1.0 + sc1p) + sh1p
        hs = _rmsnorm(xs, norm_mix_g[l]) * (1.0 + sc1s) + sh1s
        i = l // N_MIXERS
        if l % N_MIXERS == 0:
            qp, kp, vp, iqp, ikp, iwp = _dsa_project(hp, w_in_dsa[i], pos_p)
            mix_p = _dsa_attend(qp, iqp, iwp, qpos_p, ikp,
                                lambda sel: (_gather_rows(kp, sel), _gather_rows(vp, sel)))
            qs, ks, vs, iqs, iks, iws = _dsa_project(hs, w_in_dsa[i], pos_s)
            ik_all = jnp.concatenate(
                [cache_idx_k[i, page_table].reshape(DB, past_len, IDX_DIM), iks], axis=1)
            mix_s = _dsa_attend(qs, iqs, iws, qpos_s, ik_all,
                                lambda sel: (_gather_paged(cache_k, l, ks, page_table, sel),
                                             _gather_paged(cache_v, l, vs, page_table, sel)))
            ikp_l.append(ikp)
            iks_l.append(iks)
        else:
            qp, kp, vp = _sb_project(hp, w_in_sb[i])
            mix_p = _stick_breaking_attend(qp, qpos_p, kp, vp)
            qs, ks, vs = _sb_project(hs, w_in_sb[i])
            k_all = jnp.concatenate(
                [cache_k[l, page_table].reshape(DB, past_len, N_KV_HEADS, HEAD_DIM), ks], axis=1)
            v_all = jnp.concatenate(
                [cache_v[l, page_table].reshape(DB, past_len, N_KV_HEADS, HEAD_DIM), vs], axis=1)
            mix_s = _stick_breaking_attend(qs, qpos_s, k_all, v_all)
        kp_l.append(kp)
        vp_l.append(vp)
        ks_l.append(ks)
        vs_l.append(vs)
        xp = xp + g1p * (mix_p @ w_out[l])
        xs = xs + g1s * (mix_s @ w_out[l])
        hp = _rmsnorm(xp, norm_ffn_g[l]) * (1.0 + sc2p) + sh2p
        hs = _rmsnorm(xs, norm_ffn_g[l]) * (1.0 + sc2s) + sh2s
        xp = xp + g2p * _swiglu(hp, w_gate[l], w_up[l], w_down[l])
        xs = xs + g2s * _swiglu(hs, w_gate[l], w_up[l], w_down[l])
    y_prompt = _rmsnorm(xp, norm_final_g)
    y_sample = _rmsnorm(xs, norm_final_g)
    new_k_prompt = jnp.stack(kp_l)
    new_v_prompt = jnp.stack(vp_l)
    new_idx_k_prompt = jnp.stack(ikp_l)
    new_k_sample = jnp.stack(ks_l)
    new_v_sample = jnp.stack(vs_l)
    new_idx_k_sample = jnp.stack(iks_l)
    return (y_prompt, y_sample, new_k_prompt, new_v_prompt, new_idx_k_prompt, new_k_sample, new_v_sample, new_idx_k_sample)
```

```python
import functools

import numpy as np
import jax
import jax.numpy as jnp
from jax import lax
from jax.experimental import pallas as pl
from jax.experimental.pallas import tpu as pltpu

F32 = jnp.float32
BF16 = jnp.bfloat16
I32 = jnp.int32

LANES = 128
HEAD_DIM = 128
KV_GROUP = 2
IDX_HEADS = 16
IDX_DIM = 128
TOPK_MAX = 256
ROPE_THETA = 10000.0
EPS = 1e-6
NEG = -1e30
INT_MIN = -2 ** 31
VMEM_LIMIT = 56 * 1024 * 1024

_NT = (((1,), (1,)), ((), ()))


def _cparams(sem):
    return pltpu.CompilerParams(dimension_semantics=sem, vmem_limit_bytes=VMEM_LIMIT)


def _split_bf16(x):
    hi = x.astype(BF16)
    lo = (x - hi.astype(F32)).astype(BF16)
    return hi, lo


def _sortable(x):
    bits = lax.bitcast_convert_type(x, I32)
    return bits ^ ((bits >> 31) & 0x7FFFFFFF)


def _adaln_kernel(c_ref, w_ref, b_ref, o_ref):
    c = c_ref[...]
    s = (c * jax.nn.sigmoid(c)).astype(BF16)
    o_ref[...] = jnp.dot(s, w_ref[...].astype(BF16), preferred_element_type=F32) + b_ref[...]


def _adaln(c_pad, w_ada, b_ada):
    depth, d, n = w_ada.shape
    r = c_pad.shape[0]
    tn = 1024
    return pl.pallas_call(
        _adaln_kernel,
        grid=(depth, n // tn),
        in_specs=[pl.BlockSpec((r, d), lambda l, j: (0, 0)),
                  pl.BlockSpec((None, d, tn), lambda l, j: (l, 0, j)),
                  pl.BlockSpec((None, 1, tn), lambda l, j: (l, 0, j))],
        out_specs=pl.BlockSpec((None, r, tn), lambda l, j: (l, 0, j)),
        out_shape=jax.ShapeDtypeStruct((depth, r, n), F32),
        compiler_params=_cparams(("arbitrary", "arbitrary")),
        name="adaln",
    )(c_pad, w_ada, b_ada.reshape(depth, 1, n))


def _norm_kernel(x_ref, g_ref, sc_ref, sh_ref, o_ref):
    x = x_ref[...]
    y = x * lax.rsqrt(jnp.mean(x * x, axis=-1, keepdims=True) + EPS) * g_ref[...]
    o_ref[...] = (y * (1.0 + sc_ref[...]) + sh_ref[...]).astype(o_ref.dtype)


def _final_norm_kernel(x_ref, g_ref, o_ref):
    x = x_ref[...]
    o_ref[...] = x * lax.rsqrt(jnp.mean(x * x, axis=-1, keepdims=True) + EPS) * g_ref[...]


def _mod_spec(mod, tm, width, rows_per_batch, col_of):
    if mod.ndim == 3:
        return pl.BlockSpec((None, 1, width), lambda *ij: ((ij[0] if col_of is None else ij[1]) * tm // rows_per_batch, 0,
                                                           0 if col_of is None else ij[0]))
    return pl.BlockSpec((tm, width), lambda *ij: ((ij[0] if col_of is None else ij[1]), 0 if col_of is None else ij[0]))


def _norm(x, g, sc, sh, tm, rows_per_batch):
    m, d = x.shape
    return pl.pallas_call(
        _norm_kernel,
        grid=(m // tm,),
        in_specs=[pl.BlockSpec((tm, d), lambda i: (i, 0)),
                  pl.BlockSpec((1, d), lambda i: (0, 0)),
                  _mod_spec(sc, tm, d, rows_per_batch, None),
                  _mod_spec(sh, tm, d, rows_per_batch, None)],
        out_specs=pl.BlockSpec((tm, d), lambda i: (i, 0)),
        out_shape=jax.ShapeDtypeStruct((m, d), BF16),
        compiler_params=_cparams(("arbitrary",)),
        name="norm_mod",
    )(x, g.reshape(1, d), sc, sh)


def _final_norm(x, g, tm):
    m, d = x.shape
    return pl.pallas_call(
        _final_norm_kernel,
        grid=(m // tm,),
        in_specs=[pl.BlockSpec((tm, d), lambda i: (i, 0)), pl.BlockSpec((1, d), lambda i: (0, 0))],
        out_specs=pl.BlockSpec((tm, d), lambda i: (i, 0)),
        out_shape=jax.ShapeDtypeStruct((m, d), F32),
        compiler_params=_cparams(("arbitrary",)),
        name="final_norm",
    )(x, g.reshape(1, d))


def _proj_kernel(h_ref, w_ref, cos_ref, sin_ref, *rest, rope, n_out):
    outs, wb_ref = rest[:n_out], rest[n_out]

    @pl.when(pl.program_id(1) == 0)
    def _():
        wb_ref[...] = w_ref[...].astype(BF16)

    acc = jnp.dot(h_ref[...], wb_ref[...], preferred_element_type=F32)
    tn = acc.shape[1]
    if rope:
        cos = cos_ref[...]
        sin = sin_ref[...]
        for c in range(tn // HEAD_DIM):
            y = acc[:, c * HEAD_DIM:(c + 1) * HEAD_DIM]
            if c < rope:
                y = y * cos + pltpu.roll(y, HEAD_DIM // 2, 1) * sin
            for o in outs:
                o[:, c * HEAD_DIM:(c + 1) * HEAD_DIM] = y.astype(o.dtype)
    else:
        for o in outs:
            o[...] = acc.astype(o.dtype)


def _proj(h, w, layer, col0, ncols, cos, sin, rope, out_dtypes, tm, tn):
    m, k = h.shape
    npb = cos.shape[0] // tm
    j0 = col0 // tn
    n_out = len(out_dtypes)
    return pl.pallas_call(
        functools.partial(_proj_kernel, rope=rope, n_out=n_out),
        grid=(ncols // tn, m // tm),
        in_specs=[pl.BlockSpec((tm, k), lambda j, i: (i, 0)),
                  pl.BlockSpec((None, k, tn), lambda j, i: (layer, 0, j + j0)),
                  pl.BlockSpec((tm, HEAD_DIM), lambda j, i: (i % npb, 0)),
                  pl.BlockSpec((tm, HEAD_DIM), lambda j, i: (i % npb, 0))],
        out_specs=[pl.BlockSpec((tm, tn), lambda j, i: (i, j)) for _ in out_dtypes],
        out_shape=[jax.ShapeDtypeStruct((m, ncols), dt) for dt in out_dtypes],
        scratch_shapes=[pltpu.VMEM((k, tn), BF16)],
        compiler_params=_cparams(("arbitrary", "arbitrary")),
        name="proj_rope" if rope else "proj",
    )(h, w, cos, sin)


def _gres_kernel(h_ref, w_ref, x_ref, g_ref, o_ref, wb_ref):
    @pl.when(pl.program_id(1) == 0)
    def _():
        wb_ref[...] = w_ref[...].astype(BF16)

    acc = jnp.dot(h_ref[...], wb_ref[...], preferred_element_type=F32)
    o_ref[...] = x_ref[...] + g_ref[...] * acc


def _gres(h, w, layer, x, gate, tm, tn, rows_per_batch):
    m, k = h.shape
    n = w.shape[2]
    return pl.pallas_call(
        _gres_kernel,
        grid=(n // tn, m // tm),
        in_specs=[pl.BlockSpec((tm, k), lambda j, i: (i, 0)),
                  pl.BlockSpec((None, k, tn), lambda j, i: (layer, 0, j)),
                  pl.BlockSpec((tm, tn), lambda j, i: (i, j)),
                  _mod_spec(gate, tm, tn, rows_per_batch, True)],
        out_specs=pl.BlockSpec((tm, tn), lambda j, i: (i, j)),
        out_shape=jax.ShapeDtypeStruct((m, n), F32),
        scratch_shapes=[pltpu.VMEM((k, tn), BF16)],
        compiler_params=_cparams(("arbitrary", "arbitrary")),
        name="gated_residual",
    )(h, w, x, gate)


def _swiglu_kernel(h_ref, wg_ref, wu_ref, o_ref, wgb_ref, wub_ref):
    @pl.when(pl.program_id(1) == 0)
    def _():
        wgb_ref[...] = wg_ref[...].astype(BF16)
        wub_ref[...] = wu_ref[...].astype(BF16)

    h = h_ref[...]
    a = jnp.dot(h, wgb_ref[...], preferred_element_type=F32)
    b = jnp.dot(h, wub_ref[...], preferred_element_type=F32)
    o_ref[...] = (a * jax.nn.sigmoid(a) * b).astype(o_ref.dtype)


def _swiglu(h, wg, wu, layer, tm, tn):
    m, k = h.shape
    n = wg.shape[2]
    return pl.pallas_call(
        _swiglu_kernel,
        grid=(n // tn, m // tm),
        in_specs=[pl.BlockSpec((tm, k), lambda j, i: (i, 0)),
                  pl.BlockSpec((None, k, tn), lambda j, i: (layer, 0, j)),
                  pl.BlockSpec((None, k, tn), lambda j, i: (layer, 0, j))],
        out_specs=pl.BlockSpec((tm, tn), lambda j, i: (i, j)),
        out_shape=jax.ShapeDtypeStruct((m, n), BF16),
        scratch_shapes=[pltpu.VMEM((k, tn), BF16), pltpu.VMEM((k, tn), BF16)],
        compiler_params=_cparams(("arbitrary", "arbitrary")),
        name="swiglu",
    )(h, wg, wu)


def _kth_threshold(count_ge, topk, rows):
    t0 = jnp.where(count_ge(jnp.zeros((rows, 1), I32)) >= topk, 0, INT_MIN).astype(I32)

    def body(it, t):
        cand = t | jnp.left_shift(jnp.int32(1), 30 - it)
        return jnp.where(count_ge(cand) >= topk, cand, t)

    return lax.fori_loop(0, 31, body, t0)


def _tie_cutoff(count_tie_below, need, nbits, rows):
    def body(it, a):
        cand = a | jnp.left_shift(jnp.int32(1), nbits - 1 - it)
        return jnp.where(count_tie_below(cand) < need, cand, a)

    return lax.fori_loop(0, nbits, body, jnp.zeros((rows, 1), I32))


def _dsa_prompt_kernel(q_ref, k_ref, v_ref, iq_ref, ik_ref, iw_ref, o_ref,
                       iqcat_ref, keys_ref, bias_ref, *, tq, ck, topk, nbits, idx_scale, attn_scale):
    i = pl.program_id(1)
    g = pl.program_id(2)
    nkc = ((i + 1) * tq + ck - 1) // ck
    qpos = i * tq + lax.broadcasted_iota(I32, (tq, 1), 0)

    @pl.when(g == 0)
    def _index():
        for h in range(IDX_HEADS):
            hi, lo = _split_bf16(iq_ref[:, h * IDX_DIM:(h + 1) * IDX_DIM])
            iqcat_ref[:, h * 3 * IDX_DIM:h * 3 * IDX_DIM + IDX_DIM] = hi
            iqcat_ref[:, h * 3 * IDX_DIM + IDX_DIM:h * 3 * IDX_DIM + 2 * IDX_DIM] = hi
            iqcat_ref[:, h * 3 * IDX_DIM + 2 * IDX_DIM:(h + 1) * 3 * IDX_DIM] = lo
        w = iw_ref[:, :IDX_HEADS]

        def score_chunk(c, carry):
            ikh, ikl = _split_bf16(ik_ref[pl.ds(pl.multiple_of(c * ck, ck), ck), :])
            ikcat = jnp.concatenate([ikh, ikl, ikh], axis=1)
            acc = jnp.zeros((tq, ck), F32)
            for h in range(IDX_HEADS):
                lg = lax.dot_general(iqcat_ref[:, h * 3 * IDX_DIM:(h + 1) * 3 * IDX_DIM], ikcat, _NT,
                                     preferred_element_type=F32)
                acc = acc + w[:, h:h + 1] * jnp.maximum(lg, 0.0)
            kpos = c * ck + lax.broadcasted_iota(I32, (tq, ck), 1)
            sc = jnp.where(kpos <= qpos, acc * idx_scale, -jnp.inf)
            keys_ref[c] = _sortable(sc)
            return carry

        lax.fori_loop(0, nkc, score_chunk, 0)

        def count(pred):
            def body(c, part):
                m = jnp.where(pred(keys_ref[c], c), 1.0, 0.0)
                for j in range(ck // LANES):
                    part = part + m[:, j * LANES:(j + 1) * LANES]
                return part
            part = lax.fori_loop(0, nkc, body, jnp.zeros((tq, LANES), F32))
            return jnp.sum(part, axis=1, keepdims=True)

        def kpos_of(c):
            return c * ck + lax.broadcasted_iota(I32, (tq, ck), 1)

        thr = _kth_threshold(lambda cand: count(lambda kc, c: kc >= cand), float(topk), tq)
        need = float(topk) - count(lambda kc, c: kc > thr)
        cut = _tie_cutoff(lambda cand: count(lambda kc, c: (kc == thr) & (kpos_of(c) < cand)), need, nbits, tq)

        def bias_chunk(c, carry):
            kc = keys_ref[c]
            kpos = kpos_of(c)
            sel = ((kc > thr) | ((kc == thr) & (kpos <= cut))) & (kpos <= qpos)
            bias_ref[c] = jnp.where(sel, 0.0, NEG)
            return carry

        lax.fori_loop(0, nkc, bias_chunk, 0)

    q2 = jnp.concatenate([q_ref[:, :HEAD_DIM], q_ref[:, HEAD_DIM:]], axis=0)

    def attend(c, carry):
        m, l, acc = carry
        off = pl.multiple_of(c * ck, ck)
        s = lax.dot_general(q2, k_ref[pl.ds(off, ck), :], _NT, preferred_element_type=F32) * attn_scale
        b = bias_ref[c]
        s = s + jnp.concatenate([b, b], axis=0)
        m_new = jnp.maximum(m, jnp.max(s, axis=1, keepdims=True))
        alpha = jnp.exp(m - m_new)
        p = jnp.exp(s - m_new)
        l = alpha * l + jnp.sum(p, axis=1, keepdims=True)
        acc = alpha * acc + jnp.dot(p.astype(BF16), v_ref[pl.ds(off, ck), :], preferred_element_type=F32)
        return m_new, l, acc

    m, l, acc = lax.fori_loop(0, nkc, attend, (jnp.full((2 * tq, 1), NEG, F32), jnp.zeros((2 * tq, 1), F32),
                                               jnp.zeros((2 * tq, HEAD_DIM), F32)))
    out = acc / l
    o_ref[:, :HEAD_DIM] = out[:tq].astype(o_ref.dtype)
    o_ref[:, HEAD_DIM:] = out[tq:].astype(o_ref.dtype)


def _dsa_prompt(q, k, v, iq, ikiw, nb, s_len, tq, ck):
    n_kv = k.shape[1] // HEAD_DIM
    nq = s_len // tq
    topk = min(TOPK_MAX, s_len // 4)
    gw = KV_GROUP * HEAD_DIM
    kern = functools.partial(_dsa_prompt_kernel, tq=tq, ck=ck, topk=topk, nbits=max(1, (s_len - 1).bit_length()),
                             idx_scale=(IDX_HEADS ** -0.5) * (IDX_DIM ** -0.5), attn_scale=HEAD_DIM ** -0.5)
    return pl.pallas_call(
        kern,
        grid=(nb, nq, n_kv),
        in_specs=[pl.BlockSpec((tq, gw), lambda b, i, g: (b * nq + i, g)),
                  pl.BlockSpec((s_len, HEAD_DIM), lambda b, i, g: (b, g)),
                  pl.BlockSpec((s_len, HEAD_DIM), lambda b, i, g: (b, g)),
                  pl.BlockSpec((tq, IDX_HEADS * IDX_DIM), lambda b, i, g: (b * nq + i, 0)),
                  pl.BlockSpec((s_len, IDX_DIM), lambda b, i, g: (b, 0)),
                  pl.BlockSpec((tq, LANES), lambda b, i, g: (b * nq + i, 1))],
        out_specs=pl.BlockSpec((tq, gw), lambda b, i, g: (b * nq + i, g)),
        out_shape=jax.ShapeDtypeStruct(q.shape, BF16),
        scratch_shapes=[pltpu.VMEM((tq, IDX_HEADS * 3 * IDX_DIM), BF16),
                        pltpu.VMEM((s_len // ck, tq, ck), I32),
                        pltpu.VMEM((s_len // ck, tq, ck), F32)],
        compiler_params=_cparams(("arbitrary", "arbitrary", "arbitrary")),
        name="dsa_prompt",
    )(q, k, v, iq, ikiw, ikiw)


def _softplus(z):
    return jnp.maximum(z, 0.0) + jnp.log(1.0 + jnp.exp(-jnp.abs(z)))


def _sb_prompt_kernel(q_ref, k_ref, v_ref, u_ref, o_ref, *, tq, ck, attn_scale):
    i = pl.program_id(1)
    nkc = ((i + 1) * tq + ck - 1) // ck
    row = lax.broadcasted_iota(I32, (2 * tq, 1), 0)
    qpos = i * tq + jnp.where(row >= tq, row - tq, row)
    q2 = jnp.concatenate([q_ref[:, :HEAD_DIM], q_ref[:, HEAD_DIM:]], axis=0)
    u = u_ref[...]

    def body(r, carry):
        tail, acc = carry
        c = nkc - 1 - r
        off = pl.multiple_of(c * ck, ck)
        z = lax.dot_general(q2, k_ref[pl.ds(off, ck), :], _NT, preferred_element_type=F32) * attn_scale
        mask = (c * ck + lax.broadcasted_iota(I32, (2 * tq, ck), 1)) < qpos
        sp = _softplus(z)
        lk = jnp.where(mask, -sp, 0.0)
        hi, lo = _split_bf16(lk)
        la = jnp.dot(hi, u, preferred_element_type=F32) + jnp.dot(lo, u, preferred_element_type=F32) + tail
        a = jnp.where(mask, jnp.exp(z - sp + la), 0.0)
        acc = acc + jnp.dot(a.astype(BF16), v_ref[pl.ds(off, ck), :], preferred_element_type=F32)
        return tail + jnp.sum(lk, axis=1, keepdims=True), acc

    _, acc = lax.fori_loop(0, nkc, body, (jnp.zeros((2 * tq, 1), F32), jnp.zeros((2 * tq, HEAD_DIM), F32)))
    o_ref[:, :HEAD_DIM] = acc[:tq].astype(o_ref.dtype)
    o_ref[:, HEAD_DIM:] = acc[tq:].astype(o_ref.dtype)


def _suffix_matrix(n):
    return jnp.asarray(np.tril(np.ones((n, n), np.float32), -1), dtype=BF16)


def _sb_prompt(q, k, v, nb, s_len, tq, ck):
    n_kv = k.shape[1] // HEAD_DIM
    nq = s_len // tq
    gw = KV_GROUP * HEAD_DIM
    return pl.pallas_call(
        functools.partial(_sb_prompt_kernel, tq=tq, ck=ck, attn_scale=HEAD_DIM ** -0.5),
        grid=(nb, nq, n_kv),
        in_specs=[pl.BlockSpec((tq, gw), lambda b, i, g: (b * nq + i, g)),
                  pl.BlockSpec((s_len, HEAD_DIM), lambda b, i, g: (b, g)),
                  pl.BlockSpec((s_len, HEAD_DIM), lambda b, i, g: (b, g)),
                  pl.BlockSpec((ck, ck), lambda b, i, g: (0, 0))],
        out_specs=pl.BlockSpec((tq, gw), lambda b, i, g: (b * nq + i, g)),
        out_shape=jax.ShapeDtypeStruct(q.shape, BF16),
        compiler_params=_cparams(("arbitrary", "arbitrary", "arbitrary")),
        name="sb_prompt",
    )(q, k, v, _suffix_matrix(ck))


def _page_consts(page, n_kv, rows):
    lane = np.arange(page * n_kv)
    expand = (lane[None, :] // n_kv == np.arange(page)[:, None]).astype(np.float32)
    valid = (lane[None, :] % n_kv == (np.arange(rows)[:, None] // (rows // n_kv))).astype(np.float32)
    return jnp.asarray(expand, BF16), jnp.asarray(expand.T.copy(), BF16), jnp.asarray(valid, F32)


def _idx_sample_kernel(pt_ref, iq_ref, w_ref, ikn_ref, *rest, pp, idx_scale):
    pages, (past_ref, new_ref) = rest[:pp], rest[pp:]
    hi, lo = _split_bf16(iq_ref[...])
    iqcat = jnp.concatenate([hi, hi, lo], axis=1)
    w = w_ref[...]
    rows = iqcat.shape[0]

    def score(ik):
        ikh, ikl = _split_bf16(ik)
        lg = lax.dot_general(iqcat, jnp.concatenate([ikh, ikl, ikh], axis=1), _NT, preferred_element_type=F32)
        x = jnp.maximum(lg, 0.0) * w
        return jnp.sum(x.reshape(IDX_HEADS, rows // IDX_HEADS, x.shape[1]), axis=0) * idx_scale

    for j in range(pp):
        past_ref[:, j * LANES:(j + 1) * LANES] = score(pages[j][...])

    @pl.when(pl.program_id(1) == 0)
    def _():
        new_ref[...] = score(ikn_ref[...])


def _idx_sample(cache_idx_k, layer, page_table, iq_rows, w_rows, ik_new_pad, pp):
    db, n_pages = page_table.shape
    page = cache_idx_k.shape[2]
    rows = iq_rows.shape[1]
    r8 = rows // IDX_HEADS
    page_specs = [pl.BlockSpec((None, None, page, IDX_DIM),
                               functools.partial(lambda b, p, pt, j: (layer, pt[b, p * pp + j], 0, 0), j=j))
                  for j in range(pp)]
    return pl.pallas_call(
        functools.partial(_idx_sample_kernel, pp=pp, idx_scale=(IDX_HEADS ** -0.5) * (IDX_DIM ** -0.5)),
        grid_spec=pltpu.PrefetchScalarGridSpec(
            num_scalar_prefetch=1,
            grid=(db, n_pages // pp),
            in_specs=[pl.BlockSpec((None, rows, IDX_DIM), lambda b, p, pt: (b, 0, 0)),
                      pl.BlockSpec((None, rows, 1), lambda b, p, pt: (b, 0, 0)),
                      pl.BlockSpec((None, page, IDX_DIM), lambda b, p, pt: (b, 0, 0))] + page_specs,
            out_specs=[pl.BlockSpec((None, r8, pp * page), lambda b, p, pt: (b, 0, p)),
                       pl.BlockSpec((None, r8, page), lambda b, p, pt: (b, 0, 0))]),
        out_shape=[jax.ShapeDtypeStruct((db, r8, n_pages * page), F32),
                   jax.ShapeDtypeStruct((db, r8, page), F32)],
        compiler_params=_cparams(("arbitrary", "arbitrary")),
        name="idx_sample",
    )(page_table, iq_rows, w_rows, ik_new_pad, *([cache_idx_k] * pp))


def _select_sample_kernel(s_ref, o_ref, *, topk, nbits, past_len):
    sc = s_ref[...]
    rows, length = sc.shape
    qpos = past_len + lax.broadcasted_iota(I32, (rows, 1), 0) // KV_GROUP
    kpos = lax.broadcasted_iota(I32, (rows, length), 1)
    causal = kpos <= qpos
    key = _sortable(jnp.where(causal, sc, -jnp.inf))

    def count(pred):
        return jnp.sum(jnp.where(pred, 1.0, 0.0), axis=1, keepdims=True)

    thr = _kth_threshold(lambda cand: count(key >= cand), float(topk), rows)
    need = float(topk) - count(key > thr)
    cut = _tie_cutoff(lambda cand: count((key == thr) & (kpos < cand)), need, nbits, rows)
    sel = ((key > thr) | ((key == thr) & (kpos <= cut))) & causal
    o_ref[...] = jnp.where(sel, 1.0, 0.0)


def _select_sample(scores, past_len, n_new):
    db, rows, length = scores.shape
    topk = min(TOPK_MAX, (past_len + n_new) // 4)
    return pl.pallas_call(
        functools.partial(_select_sample_kernel, topk=topk, nbits=max(1, (length - 1).bit_length()), past_len=past_len),
        grid=(db,),
        in_specs=[pl.BlockSpec((None, rows, length), lambda b: (b, 0, 0))],
        out_specs=pl.BlockSpec((None, rows, length), lambda b: (b, 0, 0)),
        out_shape=jax.ShapeDtypeStruct(scores.shape, F32),
        compiler_params=_cparams(("arbitrary",)),
        name="select_sample",
    )(scores)


def _flat_bf16(ref):
    x = ref[...]
    return x.reshape(x.shape[0] * x.shape[1], x.shape[2]).astype(BF16)


def _dsa_sample_kernel(pt_ref, q_ref, mask_ref, e_ref, valid_ref, kn_ref, vn_ref, *rest,
                       pp, n_steps, attn_scale):
    kpages, vpages = rest[:pp], rest[pp:2 * pp]
    o_ref, m_ref, l_ref, acc_ref = rest[2 * pp:]
    p = pl.program_id(1)
    q = q_ref[...]
    rows = q.shape[0]
    reps = rows // mask_ref.shape[0]
    valid = valid_ref[...]

    @pl.when(p == 0)
    def _():
        m_ref[...] = jnp.full(m_ref.shape, NEG, F32)
        l_ref[...] = jnp.zeros(l_ref.shape, F32)
        acc_ref[...] = jnp.zeros(acc_ref.shape, F32)

    def page(kref, vref, mask8):
        s = lax.dot_general(q, _flat_bf16(kref), _NT, preferred_element_type=F32) * attn_scale
        mrows = jnp.concatenate([mask8] * reps, axis=0).astype(BF16)
        keep = jnp.dot(mrows, e_ref[...], preferred_element_type=F32) * valid > 0.5
        s = jnp.where(keep, s, NEG)
        m = m_ref[...]
        m_new = jnp.maximum(m, jnp.max(s, axis=1, keepdims=True))
        alpha = jnp.exp(m - m_new)
        pr = jnp.exp(s - m_new)
        l_ref[...] = alpha * l_ref[...] + jnp.sum(pr, axis=1, keepdims=True)
        acc_ref[...] = alpha * acc_ref[...] + jnp.dot(pr.astype(BF16), _flat_bf16(vref), preferred_element_type=F32)
        m_ref[...] = m_new

    @pl.when(p < n_steps - 1)
    def _():
        for j in range(pp):
            page(kpages[j], vpages[j], mask_ref[:, j * LANES:(j + 1) * LANES])

    @pl.when(p == n_steps - 1)
    def _():
        page(kn_ref, vn_ref, mask_ref[:, :LANES])
        o_ref[...] = (acc_ref[...] / l_ref[...]).astype(o_ref.dtype)


def _dsa_sample(cache_k, cache_v, layer, page_table, q_rows, mask, k_new_pad, v_new_pad, pp):
    db, n_pages = page_table.shape
    page, n_kv = cache_k.shape[2], cache_k.shape[3]
    rows = q_rows.shape[1]
    n_steps = n_pages // pp + 1
    expand, _, valid = _page_consts(page, n_kv, rows)

    def cache_spec(j):
        return pl.BlockSpec((None, None, page, n_kv, HEAD_DIM),
                            lambda b, p, pt: (layer, pt[b, jnp.minimum(p, n_steps - 2) * pp + j], 0, 0, 0))

    new_spec = pl.BlockSpec((None, page, n_kv, HEAD_DIM), lambda b, p, pt: (b, 0, 0, 0))
    mask_spec = pl.BlockSpec((None, mask.shape[1], pp * page),
                             lambda b, p, pt: (b, 0, jnp.where(p == n_steps - 1, n_pages // pp, p)))
    return pl.pallas_call(
        functools.partial(_dsa_sample_kernel, pp=pp, n_steps=n_steps, attn_scale=HEAD_DIM ** -0.5),
        grid_spec=pltpu.PrefetchScalarGridSpec(
            num_scalar_prefetch=1,
            grid=(db, n_steps),
            in_specs=[pl.BlockSpec((None, rows, HEAD_DIM), lambda b, p, pt: (b, 0, 0)),
                      mask_spec,
                      pl.BlockSpec(expand.shape, lambda b, p, pt: (0, 0)),
                      pl.BlockSpec(valid.shape, lambda b, p, pt: (0, 0)),
                      new_spec, new_spec] + [cache_spec(j) for j in range(pp)] * 2,
            out_specs=pl.BlockSpec((None, rows, HEAD_DIM), lambda b, p, pt: (b, 0, 0)),
            scratch_shapes=[pltpu.VMEM((rows, 1), F32), pltpu.VMEM((rows, 1), F32), pltpu.VMEM((rows, HEAD_DIM), F32)]),
        out_shape=jax.ShapeDtypeStruct((db, rows, HEAD_DIM), BF16),
        compiler_params=_cparams(("arbitrary", "arbitrary")),
        name="dsa_sample",
    )(page_table, q_rows, mask, expand, valid, k_new_pad, v_new_pad, *([cache_k] * pp), *([cache_v] * pp))


def _sb_sample_kernel(pt_ref, q_ref, e_ref, c_ref, valid_ref, u_ref, kn_ref, vn_ref, *rest,
                      pp, n_pages, past_len, attn_scale):
    kpages, vpages = rest[:pp], rest[pp:2 * pp]
    o_ref, tail_ref, acc_ref = rest[2 * pp:]
    p = pl.program_id(1)
    q = q_ref[...]
    rows = q.shape[0]
    n_kv = valid_ref.shape[1] // e_ref.shape[0]
    valid = valid_ref[...]
    page_len = e_ref.shape[0]
    qpos = past_len + (lax.broadcasted_iota(I32, (rows, 1), 0) % (rows // n_kv)) // KV_GROUP

    @pl.when(p == 0)
    def _():
        tail_ref[...] = jnp.zeros(tail_ref.shape, F32)
        acc_ref[...] = jnp.zeros(acc_ref.shape, F32)

    def page(kref, vref, start):
        zf = lax.dot_general(q, _flat_bf16(kref), _NT, preferred_element_type=F32) * (attn_scale * valid)
        hi, lo = _split_bf16(zf)
        zz = jnp.dot(jnp.concatenate([hi, lo], axis=0), c_ref[...], preferred_element_type=F32)
        z = zz[:rows] + zz[rows:]
        mask = (start + lax.broadcasted_iota(I32, (rows, page_len), 1)) < qpos
        sp = _softplus(z)
        lk = jnp.where(mask, -sp, 0.0)
        hi, lo = _split_bf16(lk)
        ll = jnp.dot(jnp.concatenate([hi, lo], axis=0), u_ref[...], preferred_element_type=F32)
        la = ll[:rows] + ll[rows:] + tail_ref[...]
        a = jnp.where(mask, jnp.exp(z - sp + la), 0.0).astype(BF16)
        ae = (jnp.dot(a, e_ref[...], preferred_element_type=F32) * valid).astype(BF16)
        acc_ref[...] += jnp.dot(ae, _flat_bf16(vref), preferred_element_type=F32)
        tail_ref[...] += jnp.sum(lk, axis=1, keepdims=True)

    @pl.when(p == 0)
    def _():
        page(kn_ref, vn_ref, past_len)

    @pl.when(p > 0)
    def _():
        for j in range(pp):
            page(kpages[j], vpages[j], (n_pages - 1 - ((p - 1) * pp + j)) * page_len)

    @pl.when(p == pl.num_programs(1) - 1)
    def _():
        o_ref[...] = acc_ref[...].astype(o_ref.dtype)


def _sb_sample(cache_k, cache_v, layer, page_table, q_rows, k_new_pad, v_new_pad, pp):
    db, n_pages = page_table.shape
    page, n_kv = cache_k.shape[2], cache_k.shape[3]
    rows = q_rows.shape[1]
    n_steps = n_pages // pp + 1
    expand, compact, valid = _page_consts(page, n_kv, rows)

    def cache_spec(j):
        return pl.BlockSpec((None, None, page, n_kv, HEAD_DIM),
                            lambda b, p, pt: (layer, pt[b, n_pages - 1 - (jnp.maximum(p - 1, 0) * pp + j)], 0, 0, 0))

    new_spec = pl.BlockSpec((None, page, n_kv, HEAD_DIM), lambda b, p, pt: (b, 0, 0, 0))
    const = lambda a: pl.BlockSpec(a.shape, lambda b, p, pt: (0, 0))
    u = _suffix_matrix(page)
    return pl.pallas_call(
        functools.partial(_sb_sample_kernel, pp=pp, n_pages=n_pages, past_len=n_pages * page,
                          attn_scale=HEAD_DIM ** -0.5),
        grid_spec=pltpu.PrefetchScalarGridSpec(
            num_scalar_prefetch=1,
            grid=(db, n_steps),
            in_specs=[pl.BlockSpec((None, rows, HEAD_DIM), lambda b, p, pt: (b, 0, 0)),
                      const(expand), const(compact), const(valid), const(u),
                      new_spec, new_spec] + [cache_spec(j) for j in range(pp)] * 2,
            out_specs=pl.BlockSpec((None, rows, HEAD_DIM), lambda b, p, pt: (b, 0, 0)),
            scratch_shapes=[pltpu.VMEM((rows, 1), F32), pltpu.VMEM((rows, HEAD_DIM), F32)]),
        out_shape=jax.ShapeDtypeStruct((db, rows, HEAD_DIM), BF16),
        compiler_params=_cparams(("arbitrary", "arbitrary")),
        name="sb_sample",
    )(page_table, q_rows, expand, compact, valid, u, k_new_pad, v_new_pad, *([cache_k] * pp), *([cache_v] * pp))


def _rope_tables(pos):
    half = HEAD_DIM // 2
    inv = ROPE_THETA ** (-jnp.arange(half, dtype=F32) / half)
    ang = pos.astype(F32)[:, None] * inv[None, :]
    cos, sin = jnp.cos(ang), jnp.sin(ang)
    return jnp.concatenate([cos, cos], axis=1), jnp.concatenate([-sin, sin], axis=1)


def _pick_tile(n, prefs):
    for t in prefs:
        if n % t == 0:
            return t
    return n


def _rows_to_heads(x, db, t, n_kv):
    return x.reshape(db, t, n_kv, KV_GROUP, HEAD_DIM).transpose(0, 2, 1, 3, 4).reshape(db, n_kv * t * KV_GROUP, HEAD_DIM)


def _heads_to_rows(x, db, t, n_kv):
    return x.reshape(db, n_kv, t, KV_GROUP, HEAD_DIM).transpose(0, 2, 1, 3, 4).reshape(db * t, n_kv * KV_GROUP * HEAD_DIM)


def _pad_page(x, page):
    return jnp.pad(x, [(0, 0), (0, page - x.shape[1])] + [(0, 0)] * (x.ndim - 2))


def kernel(x_prompt, x_sample, c_prompt, c_sample, cache_k, cache_v, cache_idx_k, page_table, norm_mix_g, norm_ffn_g,
           w_ada, b_ada, w_in_dsa, w_in_sb, w_out, w_gate, w_up, w_down, norm_final_g):
    nb, s_len, d = x_prompt.shape
    db, t_new, _ = x_sample.shape
    depth = w_ada.shape[0]
    page, n_kv = cache_k.shape[2], cache_k.shape[3]
    n_pages = page_table.shape[1]
    past_len = n_pages * page
    attn_w = w_out.shape[1]
    kv_w = n_kv * HEAD_DIM
    idx_w = IDX_HEADS * IDX_DIM
    mp, ms = nb * s_len, db * t_new

    tm = _pick_tile(s_len, (1024, 512, 256, 128))
    tq = 128
    ck = 256
    pp = _pick_tile(n_pages, (4, 2, 1))

    n_c = nb + db
    c_all = jnp.pad(jnp.concatenate([c_prompt, c_sample], axis=0), ((0, (-n_c) % 16), (0, 0)))
    mod = _adaln(c_all, w_ada, b_ada).reshape(depth, c_all.shape[0], 6, d)

    cos_p, sin_p = _rope_tables(jnp.arange(s_len))
    cos_s, sin_s = _rope_tables(jnp.tile(past_len + jnp.arange(t_new), db))

    xp = x_prompt.reshape(mp, d)
    xs = x_sample.reshape(ms, d)
    outs = {n: [] for n in ("kp", "vp", "ikp", "ks", "vs", "iks")}

    for l in range(depth):
        mp_l = [mod[l, :nb, j].reshape(nb, 1, d) for j in range(6)]
        ms_l = [jnp.repeat(mod[l, nb:n_c, j], t_new, axis=0) for j in range(6)]
        hp = _norm(xp, norm_mix_g[l], mp_l[1], mp_l[0], tm, s_len)
        hs = _norm(xs, norm_mix_g[l], ms_l[1], ms_l[0], ms, 1)
        i = l // 2
        if l % 2 == 0:
            w = w_in_dsa
            w_tail = jnp.pad(w[i, :, attn_w + 2 * kv_w + idx_w:], ((0, 0), (0, 2 * LANES - IDX_DIM - IDX_HEADS)))[None]

            def project(h, cos, sin, tm_):
                tn = 512
                (q,) = _proj(h, w, i, 0, attn_w, cos, sin, tn // HEAD_DIM, (BF16,), tm_, tn)
                k32, k16 = _proj(h, w, i, attn_w, kv_w, cos, sin, tn // HEAD_DIM, (F32, BF16), tm_, tn)
                v32, v16 = _proj(h, w, i, attn_w + kv_w, kv_w, cos, sin, 0, (F32, BF16), tm_, tn)
                (iq,) = _proj(h, w, i, attn_w + 2 * kv_w, idx_w, cos, sin, tn // HEAD_DIM, (F32,), tm_, tn)
                (ikiw,) = _proj(h, w_tail, 0, 0, 2 * LANES, cos, sin, 1, (F32,), tm_, 2 * LANES)
                return q, k32, k16, v32, v16, iq, ikiw

            q, k32, k16, v32, v16, iq, ikiw = project(hp, cos_p, sin_p, tm)
            mix_p = _dsa_prompt(q, k16, v16, iq, ikiw, nb, s_len, tq, ck)
            outs["ikp"].append(ikiw[:, :IDX_DIM].reshape(nb, s_len, IDX_DIM))

            qs, ks32, _, vs32, _, iqs, ikiw_s = project(hs, cos_s, sin_s, ms)
            iks = ikiw_s[:, :IDX_DIM].reshape(db, t_new, IDX_DIM)
            iws = ikiw_s[:, IDX_DIM:IDX_DIM + IDX_HEADS].reshape(db, t_new, IDX_HEADS)
            iq_rows = jnp.broadcast_to(iqs.reshape(db, t_new, IDX_HEADS, 1, IDX_DIM).transpose(0, 2, 1, 3, 4),
                                       (db, IDX_HEADS, t_new, KV_GROUP, IDX_DIM)).reshape(db, -1, IDX_DIM)
            w_rows = jnp.broadcast_to(iws.transpose(0, 2, 1)[..., None], (db, IDX_HEADS, t_new, KV_GROUP)).reshape(db, -1, 1)
            sc_past, sc_new = _idx_sample(cache_idx_k, i, page_table, iq_rows, w_rows, _pad_page(iks, page), pp)
            scores = jnp.concatenate([sc_past, sc_new, jnp.zeros((db, sc_new.shape[1], (pp - 1) * page), F32)], axis=2)
            mask = _select_sample(scores, past_len, t_new)
            ks4 = ks32.reshape(db, t_new, n_kv, HEAD_DIM)
            vs4 = vs32.reshape(db, t_new, n_kv, HEAD_DIM)
            mix_s = _dsa_sample(cache_k, cache_v, l, page_table, _rows_to_heads(qs, db, t_new, n_kv), mask,
                                _pad_page(ks4, page), _pad_page(vs4, page), pp)
            mix_s = _heads_to_rows(mix_s, db, t_new, n_kv)
            outs["iks"].append(iks)
        else:
            w = w_in_sb

            def project(h, cos, sin, tm_):
                tn = 512
                (q,) = _proj(h, w, i, 0, attn_w, cos, sin, 0, (BF16,), tm_, tn)
                k32, k16 = _proj(h, w, i, attn_w, kv_w, cos, sin, 0, (F32, BF16), tm_, tn)
                v32, v16 = _proj(h, w, i, attn_w + kv_w, kv_w, cos, sin, 0, (F32, BF16), tm_, tn)
                return q, k32, k16, v32, v16

            q, k32, k16, v32, v16 = project(hp, cos_p, sin_p, tm)
            mix_p = _sb_prompt(q, k16, v16, nb, s_len, tq, ck)
            qs, ks32, _, vs32, _ = project(hs, cos_s, sin_s, ms)
            ks4 = ks32.reshape(db, t_new, n_kv, HEAD_DIM)
            vs4 = vs32.reshape(db, t_new, n_kv, HEAD_DIM)
            mix_s = _sb_sample(cache_k, cache_v, l, page_table, _rows_to_heads(qs, db, t_new, n_kv),
                               _pad_page(ks4, page), _pad_page(vs4, page), pp)
            mix_s = _heads_to_rows(mix_s, db, t_new, n_kv)
        outs["kp"].append(k32.reshape(nb, s_len, n_kv, HEAD_DIM))
        outs["vp"].append(v32.reshape(nb, s_len, n_kv, HEAD_DIM))
        outs["ks"].append(ks4)
        outs["vs"].append(vs4)

        xp = _gres(mix_p, w_out, l, xp, mp_l[2], tm, 512, s_len)
        xs = _gres(mix_s, w_out, l, xs, ms_l[2], ms, 512, 1)
        hp = _norm(xp, norm_ffn_g[l], mp_l[4], mp_l[3], tm, s_len)
        hs = _norm(xs, norm_ffn_g[l], ms_l[4], ms_l[3], ms, 1)
        tn_ff = _pick_tile(w_gate.shape[2], (512, 256, 128))
        xp = _gres(_swiglu(hp, w_gate, w_up, l, tm, tn_ff), w_down, l, xp, mp_l[5], tm, 256, s_len)
        xs = _gres(_swiglu(hs, w_gate, w_up, l, ms, tn_ff), w_down, l, xs, ms_l[5], ms, 256, 1)

    y_prompt = _final_norm(xp, norm_final_g, tm).reshape(nb, s_len, d)
    y_sample = _final_norm(xs, norm_final_g, ms).reshape(db, t_new, d)
    return (y_prompt, y_sample, jnp.stack(outs["kp"]), jnp.stack(outs["vp"]), jnp.stack(outs["ikp"]),
            jnp.stack(outs["ks"]), jnp.stack(outs["vs"]), jnp.stack(outs["iks"]))
```

```python
import functools

import numpy as np
import jax
import jax.numpy as jnp
from jax import lax
from jax.experimental import pallas as pl
from jax.experimental.pallas import tpu as pltpu

F32 = jnp.float32
BF16 = jnp.bfloat16
I32 = jnp.int32

LANES = 128
HEAD_DIM = 128
KV_GROUP = 2
IDX_HEADS = 16
IDX_DIM = 128
TOPK_MAX = 256
ROPE_THETA = 10000.0
EPS = 1e-6
NEG = -1e30
INT_MIN = -2 ** 31
KEY_NEG_INF = -2139095041
LOG2E = 1.4426950408889634
VMEM_LIMIT = 56 * 1024 * 1024

_NT = (((1,), (1,)), ((), ()))


def _cparams(sem):
    return pltpu.CompilerParams(dimension_semantics=sem, vmem_limit_bytes=VMEM_LIMIT)


def _split_bf16(x):
    hi = x.astype(BF16)
    lo = (x - hi.astype(F32)).astype(BF16)
    return hi, lo


def _sortable(x):
    bits = lax.bitcast_convert_type(x, I32)
    return bits ^ ((bits >> 31) & 0x7FFFFFFF)


def _adaln_kernel(c_ref, w_ref, b_ref, o_ref):
    c = c_ref[...]
    s = (c * jax.nn.sigmoid(c)).astype(BF16)
    o_ref[...] = jnp.dot(s, w_ref[...].astype(BF16), preferred_element_type=F32) + b_ref[...]


def _adaln(c_pad, w_ada, b_ada):
    depth, d, n = w_ada.shape
    r = c_pad.shape[0]
    tn = 1024
    return pl.pallas_call(
        _adaln_kernel,
        grid=(depth, n // tn),
        in_specs=[pl.BlockSpec((r, d), lambda l, j: (0, 0)),
                  pl.BlockSpec((None, d, tn), lambda l, j: (l, 0, j)),
                  pl.BlockSpec((None, 1, tn), lambda l, j: (l, 0, j))],
        out_specs=pl.BlockSpec((None, r, tn), lambda l, j: (l, 0, j)),
        out_shape=jax.ShapeDtypeStruct((depth, r, n), F32),
        compiler_params=_cparams(("arbitrary", "arbitrary")),
        name="adaln",
    )(c_pad, w_ada, b_ada.reshape(depth, 1, n))


def _norm_kernel(x_ref, g_ref, sc_ref, sh_ref, o_ref):
    x = x_ref[...]
    y = x * lax.rsqrt(jnp.mean(x * x, axis=-1, keepdims=True) + EPS) * g_ref[...]
    o_ref[...] = (y * (1.0 + sc_ref[...]) + sh_ref[...]).astype(o_ref.dtype)


def _final_norm_kernel(x_ref, g_ref, o_ref):
    x = x_ref[...]
    o_ref[...] = x * lax.rsqrt(jnp.mean(x * x, axis=-1, keepdims=True) + EPS) * g_ref[...]


def _mod_spec(mod, tm, width, rows_per_batch, col_of):
    if mod.ndim == 3:
        return pl.BlockSpec((None, 1, width), lambda *ij: ((ij[0] if col_of is None else ij[1]) * tm // rows_per_batch, 0,
                                                           0 if col_of is None else ij[0]))
    return pl.BlockSpec((tm, width), lambda *ij: ((ij[0] if col_of is None else ij[1]), 0 if col_of is None else ij[0]))


def _norm(x, g, sc, sh, tm, rows_per_batch):
    m, d = x.shape
    return pl.pallas_call(
        _norm_kernel,
        grid=(m // tm,),
        in_specs=[pl.BlockSpec((tm, d), lambda i: (i, 0)),
                  pl.BlockSpec((1, d), lambda i: (0, 0)),
                  _mod_spec(sc, tm, d, rows_per_batch, None),
                  _mod_spec(sh, tm, d, rows_per_batch, None)],
        out_specs=pl.BlockSpec((tm, d), lambda i: (i, 0)),
        out_shape=jax.ShapeDtypeStruct((m, d), BF16),
        compiler_params=_cparams(("arbitrary",)),
        name="norm_mod",
    )(x, g.reshape(1, d), sc, sh)


def _final_norm(x, g, tm):
    m, d = x.shape
    return pl.pallas_call(
        _final_norm_kernel,
        grid=(m // tm,),
        in_specs=[pl.BlockSpec((tm, d), lambda i: (i, 0)), pl.BlockSpec((1, d), lambda i: (0, 0))],
        out_specs=pl.BlockSpec((tm, d), lambda i: (i, 0)),
        out_shape=jax.ShapeDtypeStruct((m, d), F32),
        compiler_params=_cparams(("arbitrary",)),
        name="final_norm",
    )(x, g.reshape(1, d))


def _proj_kernel(h_ref, w_ref, cos_ref, sin_ref, *rest, rope, n_out, scale):
    outs, wb_ref = rest[:n_out], rest[n_out]

    @pl.when(pl.program_id(1) == 0)
    def _():
        wb_ref[...] = w_ref[...].astype(BF16)

    acc = jnp.dot(h_ref[...], wb_ref[...], preferred_element_type=F32)
    if scale != 1.0:
        acc = acc * scale
    tn = acc.shape[1]
    if rope:
        cos = cos_ref[...]
        sin = sin_ref[...]
        for c in range(tn // HEAD_DIM):
            y = acc[:, c * HEAD_DIM:(c + 1) * HEAD_DIM]
            if c < rope:
                y = y * cos + pltpu.roll(y, HEAD_DIM // 2, 1) * sin
            for o in outs:
                o[:, c * HEAD_DIM:(c + 1) * HEAD_DIM] = y.astype(o.dtype)
    else:
        for o in outs:
            o[...] = acc.astype(o.dtype)


def _proj(h, w, layer, col0, ncols, cos, sin, rope, out_dtypes, tm, tn, scale=1.0):
    m, k = h.shape
    npb = cos.shape[0] // tm
    j0 = col0 // tn
    n_out = len(out_dtypes)
    return pl.pallas_call(
        functools.partial(_proj_kernel, rope=rope, n_out=n_out, scale=scale),
        grid=(ncols // tn, m // tm),
        in_specs=[pl.BlockSpec((tm, k), lambda j, i: (i, 0)),
                  pl.BlockSpec((None, k, tn), lambda j, i: (layer, 0, j + j0)),
                  pl.BlockSpec((tm, HEAD_DIM), lambda j, i: (i % npb, 0)),
                  pl.BlockSpec((tm, HEAD_DIM), lambda j, i: (i % npb, 0))],
        out_specs=[pl.BlockSpec((tm, tn), lambda j, i: (i, j)) for _ in out_dtypes],
        out_shape=[jax.ShapeDtypeStruct((m, ncols), dt) for dt in out_dtypes],
        scratch_shapes=[pltpu.VMEM((k, tn), BF16)],
        compiler_params=_cparams(("arbitrary", "arbitrary")),
        name="proj_rope" if rope else "proj",
    )(h, w, cos, sin)


def _gres_kernel(h_ref, w_ref, x_ref, g_ref, o_ref, wb_ref):
    @pl.when(pl.program_id(1) == 0)
    def _():
        wb_ref[...] = w_ref[...].astype(BF16)

    acc = jnp.dot(h_ref[...], wb_ref[...], preferred_element_type=F32)
    o_ref[...] = x_ref[...] + g_ref[...] * acc


def _gres(h, w, layer, x, gate, tm, tn, rows_per_batch):
    m, k = h.shape
    n = w.shape[2]
    return pl.pallas_call(
        _gres_kernel,
        grid=(n // tn, m // tm),
        in_specs=[pl.BlockSpec((tm, k), lambda j, i: (i, 0)),
                  pl.BlockSpec((None, k, tn), lambda j, i: (layer, 0, j)),
                  pl.BlockSpec((tm, tn), lambda j, i: (i, j)),
                  _mod_spec(gate, tm, tn, rows_per_batch, True)],
        out_specs=pl.BlockSpec((tm, tn), lambda j, i: (i, j)),
        out_shape=jax.ShapeDtypeStruct((m, n), F32),
        scratch_shapes=[pltpu.VMEM((k, tn), BF16)],
        compiler_params=_cparams(("arbitrary", "arbitrary")),
        name="gated_residual",
    )(h, w, x, gate)


def _swiglu_kernel(h_ref, wg_ref, wu_ref, o_ref, wgb_ref, wub_ref):
    @pl.when(pl.program_id(1) == 0)
    def _():
        wgb_ref[...] = wg_ref[...].astype(BF16)
        wub_ref[...] = wu_ref[...].astype(BF16)

    h = h_ref[...]
    a = jnp.dot(h, wgb_ref[...], preferred_element_type=F32)
    b = jnp.dot(h, wub_ref[...], preferred_element_type=F32)
    o_ref[...] = (a * jax.nn.sigmoid(a) * b).astype(o_ref.dtype)


def _swiglu(h, wg, wu, layer, tm, tn):
    m, k = h.shape
    n = wg.shape[2]
    return pl.pallas_call(
        _swiglu_kernel,
        grid=(n // tn, m // tm),
        in_specs=[pl.BlockSpec((tm, k), lambda j, i: (i, 0)),
                  pl.BlockSpec((None, k, tn), lambda j, i: (layer, 0, j)),
                  pl.BlockSpec((None, k, tn), lambda j, i: (layer, 0, j))],
        out_specs=pl.BlockSpec((tm, tn), lambda j, i: (i, j)),
        out_shape=jax.ShapeDtypeStruct((m, n), BF16),
        scratch_shapes=[pltpu.VMEM((k, tn), BF16), pltpu.VMEM((k, tn), BF16)],
        compiler_params=_cparams(("arbitrary", "arbitrary")),
        name="swiglu",
    )(h, wg, wu)


def _kth_threshold(count_ge, topk, shape):
    t0 = jnp.where(count_ge(jnp.zeros(shape, I32)) >= topk, 0, INT_MIN).astype(I32)

    def body(it, t):
        cand = t | jnp.left_shift(jnp.int32(1), 30 - it)
        return jnp.where(count_ge(cand) >= topk, cand, t)

    return lax.fori_loop(0, 31, body, t0)


def _tie_cutoff(count_tie_below, need, nbits, shape):
    def body(it, a):
        cand = a | jnp.left_shift(jnp.int32(1), nbits - 1 - it)
        return jnp.where(count_tie_below(cand) < need, cand, a)

    return lax.fori_loop(0, nbits, body, jnp.zeros(shape, I32))


def _dsa_prompt_kernel(q_ref, k_ref, vt_ref, iq_ref, ik_ref, iwt_ref, o_ref,
                       iqcat_ref, keys_ref, bias_ref, cut_ref, m_ref, l_ref, acc_ref,
                       *, tq, ck, ng, topk, nbits, idx_scale):
    i = pl.program_id(1)
    nkc = ((i + 1) * tq + ck - 1) // ck
    qpos = i * tq + lax.broadcasted_iota(I32, (1, tq), 1)

    def kpos_of(c):
        return c * ck + lax.broadcasted_iota(I32, (ck, tq), 0)

    @pl.when(pl.program_id(2) == 0)
    def _index():
        for h in range(IDX_HEADS):
            hi, lo = _split_bf16(iq_ref[:, h * IDX_DIM:(h + 1) * IDX_DIM])
            iqcat_ref[:, h * 3 * IDX_DIM:h * 3 * IDX_DIM + IDX_DIM] = hi
            iqcat_ref[:, h * 3 * IDX_DIM + IDX_DIM:h * 3 * IDX_DIM + 2 * IDX_DIM] = hi
            iqcat_ref[:, h * 3 * IDX_DIM + 2 * IDX_DIM:(h + 1) * 3 * IDX_DIM] = lo

        def score_chunk(c, carry):
            ikh, ikl = _split_bf16(ik_ref[pl.ds(pl.multiple_of(c * ck, ck), ck), :])
            ikcat = jnp.concatenate([ikh, ikl, ikh], axis=1)
            acc = jnp.zeros((ck, tq), F32)
            for h in range(IDX_HEADS):
                lg = lax.dot_general(ikcat, iqcat_ref[:, h * 3 * IDX_DIM:(h + 1) * 3 * IDX_DIM], _NT,
                                     preferred_element_type=F32)
                acc = acc + iwt_ref[h:h + 1, :] * jnp.maximum(lg, 0.0)
            sc = jnp.where(kpos_of(c) <= qpos, acc * idx_scale, -jnp.inf)
            keys_ref[c] = _sortable(sc)
            return carry

        lax.fori_loop(0, nkc, score_chunk, 0)

        def count(pred):
            def body(c, part):
                m = jnp.where(pred(keys_ref[c], c), 1.0, 0.0)
                return part + jnp.sum(m.reshape(ck // 8, 8, tq), axis=0)
            part = lax.fori_loop(0, nkc, body, jnp.zeros((8, tq), F32))
            return jnp.sum(part, axis=0, keepdims=True)

        thr = _kth_threshold(lambda cand: count(lambda kc, c: kc >= cand), float(topk), (1, tq))
        n_gt = count(lambda kc, c: kc > thr)
        n_ge = count(lambda kc, c: kc >= thr)
        need = float(topk) - n_gt
        crowded = (n_ge > float(topk)) & (thr > KEY_NEG_INF)
        cut_ref[...] = jnp.full((1, tq), 2 ** 30, I32)

        @pl.when(jnp.max(jnp.where(crowded, 1.0, 0.0)) > 0.5)
        def _():
            cut_ref[...] = _tie_cutoff(lambda cand: count(lambda kc, c: (kc == thr) & (kpos_of(c) < cand)),
                                       need, nbits, (1, tq))

        cut = cut_ref[...]

        def bias_chunk(c, carry):
            kc = keys_ref[c]
            kpos = kpos_of(c)
            sel = ((kc > thr) | ((kc == thr) & (kpos <= cut))) & (kpos <= qpos)
            bias_ref[c] = jnp.where(sel, 0.0, NEG)
            return carry

        lax.fori_loop(0, nkc, bias_chunk, 0)

    m_ref[...] = jnp.full(m_ref.shape, NEG, F32)
    l_ref[...] = jnp.zeros(l_ref.shape, F32)
    acc_ref[...] = jnp.zeros(acc_ref.shape, F32)

    nh = ng * KV_GROUP

    def attend(c, carry):
        off = pl.multiple_of(c * ck, ck)
        b = bias_ref[c]

        def scores(h):
            g = h // KV_GROUP
            return lax.dot_general(k_ref[pl.ds(off, ck), g * HEAD_DIM:(g + 1) * HEAD_DIM],
                                   q_ref[:, h * HEAD_DIM:(h + 1) * HEAD_DIM], _NT,
                                   preferred_element_type=F32) + b

        def softmax(h, s):
            m = m_ref[h]
            m_new = jnp.maximum(m, jnp.max(s, axis=0, keepdims=True))
            alpha = jnp.exp2(m - m_new)
            p = jnp.exp2(s - m_new)
            l_ref[h] = alpha * l_ref[h] + jnp.sum(p, axis=0, keepdims=True)
            m_ref[h] = m_new
            return p.astype(BF16), alpha

        def weigh(h, p, alpha):
            g = h // KV_GROUP
            acc_ref[h] = alpha * acc_ref[h] + jnp.dot(vt_ref[c, g * HEAD_DIM:(g + 1) * HEAD_DIM, :], p,
                                                      preferred_element_type=F32)

        ahead = 4
        s = {h: scores(h) for h in range(min(ahead, nh))}
        for h in range(nh):
            p, alpha = softmax(h, s.pop(h))
            if h + ahead < nh:
                s[h + ahead] = scores(h + ahead)
            weigh(h, p, alpha)
        return carry

    lax.fori_loop(0, nkc, attend, 0)
    for h in range(nh):
        out = acc_ref[h] * (1.0 / l_ref[h])
        o_ref[:, h * HEAD_DIM:(h + 1) * HEAD_DIM] = out.T.astype(o_ref.dtype)


def _dsa_prompt(q, k, vt, iq, ikiw, iwt, nb, s_len, tq, ck, ng):
    n_kv = k.shape[1] // HEAD_DIM
    nq = s_len // tq
    topk = min(TOPK_MAX, s_len // 4)
    nh = ng * KV_GROUP
    kern = functools.partial(_dsa_prompt_kernel, tq=tq, ck=ck, ng=ng, topk=topk, nbits=max(1, (s_len - 1).bit_length()),
                             idx_scale=(IDX_HEADS ** -0.5) * (IDX_DIM ** -0.5))
    return pl.pallas_call(
        kern,
        grid=(nb, nq, n_kv // ng),
        in_specs=[pl.BlockSpec((tq, nh * HEAD_DIM), lambda b, i, g: (b * nq + i, g)),
                  pl.BlockSpec((s_len, ng * HEAD_DIM), lambda b, i, g: (b, g)),
                  pl.BlockSpec((None, s_len // ck, ng * HEAD_DIM, ck), lambda b, i, g: (b, 0, g, 0)),
                  pl.BlockSpec((tq, IDX_HEADS * IDX_DIM), lambda b, i, g: (b * nq + i, 0)),
                  pl.BlockSpec((s_len, IDX_DIM), lambda b, i, g: (b, 0)),
                  pl.BlockSpec((IDX_HEADS, tq), lambda b, i, g: (0, b * nq + i))],
        out_specs=pl.BlockSpec((tq, nh * HEAD_DIM), lambda b, i, g: (b * nq + i, g)),
        out_shape=jax.ShapeDtypeStruct(q.shape, BF16),
        scratch_shapes=[pltpu.VMEM((tq, IDX_HEADS * 3 * IDX_DIM), BF16),
                        pltpu.VMEM((s_len // ck, ck, tq), I32),
                        pltpu.VMEM((s_len // ck, ck, tq), F32),
                        pltpu.VMEM((1, tq), I32),
                        pltpu.VMEM((nh, 1, tq), F32),
                        pltpu.VMEM((nh, 1, tq), F32),
                        pltpu.VMEM((nh, HEAD_DIM, tq), F32)],
        compiler_params=_cparams(("arbitrary", "arbitrary", "arbitrary")),
        name="dsa_prompt",
    )(q, k, vt, iq, ikiw, iwt)


def _softplus2(z):
    return jnp.maximum(z, 0.0) + jnp.log2(1.0 + jnp.exp2(-jnp.abs(z)))


def _sb_prompt_kernel(q_ref, k_ref, vt_ref, ln_ref, o_ref, tail_ref, acc_ref, *, tq, ck, ng):
    i = pl.program_id(1)
    nkc = ((i + 1) * tq + ck - 1) // ck
    nh = ng * KV_GROUP
    qpos = i * tq + lax.broadcasted_iota(I32, (1, tq), 1)
    tail_ref[...] = jnp.zeros(tail_ref.shape, F32)
    acc_ref[...] = jnp.zeros(acc_ref.shape, F32)

    def chunk(c, masked):
        off = pl.multiple_of(c * ck, ck)
        mask = (c * ck + lax.broadcasted_iota(I32, (ck, tq), 0)) < qpos if masked else None

        def logits(h):
            g = h // KV_GROUP
            return lax.dot_general(k_ref[pl.ds(off, ck), g * HEAD_DIM:(g + 1) * HEAD_DIM],
                                   q_ref[:, h * HEAD_DIM:(h + 1) * HEAD_DIM], _NT, preferred_element_type=F32)

        def keep(h, z):
            sp = jnp.where(z > 64.0, z, jnp.log2(1.0 + jnp.exp2(z)))
            spm = jnp.where(mask, sp, 0.0) if masked else sp
            la = jnp.dot(ln_ref[...], spm.astype(BF16), preferred_element_type=F32)
            tail = tail_ref[h]
            tail_ref[h] = tail + la[ck:ck + 1]
            return z - sp + tail, la

        def weigh(h, base, la):
            g = h // KV_GROUP
            a = jnp.exp2(base + la[:ck])
            if masked:
                a = jnp.where(mask, a, 0.0)
            acc_ref[h] += jnp.dot(vt_ref[c, g * HEAD_DIM:(g + 1) * HEAD_DIM, :], a.astype(BF16),
                                  preferred_element_type=F32)

        a1, a2 = 2, 4
        z = {h: logits(h) for h in range(min(a2, nh))}
        kept = {h: keep(h, z.pop(h)) for h in range(min(a1, nh))}
        for h in range(nh):
            if h + a1 < nh:
                kept[h + a1] = keep(h + a1, z.pop(h + a1))
            if h + a2 < nh:
                z[h + a2] = logits(h + a2)
            weigh(h, *kept.pop(h))

    chunk(nkc - 1, True)

    def body(r, carry):
        chunk(nkc - 1 - r, False)
        return carry

    lax.fori_loop(1, nkc, body, 0)
    for h in range(nh):
        o_ref[:, h * HEAD_DIM:(h + 1) * HEAD_DIM] = acc_ref[h].T.astype(o_ref.dtype)


def _suffix_matrix(n):
    return jnp.asarray(np.tril(np.ones((n, n), np.float32), -1), dtype=BF16)


def _sb_prompt(q, k, vt, nb, s_len, tq, ck, ng):
    assert ck % tq == 0
    n_kv = k.shape[1] // HEAD_DIM
    nq = s_len // tq
    gw = KV_GROUP * HEAD_DIM
    ln = jnp.concatenate([-_suffix_matrix(ck).T, -jnp.ones((8, ck), BF16)], axis=0)
    return pl.pallas_call(
        functools.partial(_sb_prompt_kernel, tq=tq, ck=ck, ng=ng),
        grid=(nb, nq, n_kv // ng),
        in_specs=[pl.BlockSpec((tq, ng * gw), lambda b, i, g: (b * nq + i, g)),
                  pl.BlockSpec((s_len, ng * HEAD_DIM), lambda b, i, g: (b, g)),
                  pl.BlockSpec((None, s_len // ck, ng * HEAD_DIM, ck), lambda b, i, g: (b, 0, g, 0)),
                  pl.BlockSpec((ck + 8, ck), lambda b, i, g: (0, 0))],
        out_specs=pl.BlockSpec((tq, ng * gw), lambda b, i, g: (b * nq + i, g)),
        out_shape=jax.ShapeDtypeStruct(q.shape, BF16),
        scratch_shapes=[pltpu.VMEM((ng * KV_GROUP, 1, tq), F32),
                        pltpu.VMEM((ng * KV_GROUP, HEAD_DIM, tq), F32)],
        compiler_params=_cparams(("arbitrary", "arbitrary", "arbitrary")),
        name="sb_prompt",
    )(q, k, vt, ln)


def _page_consts(page, n_kv, rows):
    lane = np.arange(page * n_kv)
    expand = (lane[None, :] // n_kv == np.arange(page)[:, None]).astype(np.float32)
    valid = (lane[None, :] % n_kv == (np.arange(rows)[:, None] // (rows // n_kv))).astype(np.float32)
    return jnp.asarray(expand, BF16), jnp.asarray(expand.T.copy(), BF16), jnp.asarray(valid, F32)


def _idx_sample_kernel(pt_ref, iq_ref, w_ref, ikn_ref, *rest, pp, idx_scale):
    pages, (past_ref, new_ref) = rest[:pp], rest[pp:]
    hi, lo = _split_bf16(iq_ref[...])
    iqcat = jnp.concatenate([hi, hi, lo], axis=1)
    w = w_ref[...]
    rows = iqcat.shape[0]

    def logits(ik):
        ikh, ikl = _split_bf16(ik)
        return lax.dot_general(iqcat, jnp.concatenate([ikh, ikl, ikh], axis=1), _NT, preferred_element_type=F32)

    def score(lg):
        x = jnp.maximum(lg, 0.0) * w
        return jnp.sum(x.reshape(IDX_HEADS, rows // IDX_HEADS, x.shape[1]), axis=0) * idx_scale

    lgs = [logits(pages[j][...]) for j in range(pp)]
    for j in range(pp):
        past_ref[:, j * LANES:(j + 1) * LANES] = score(lgs[j])

    @pl.when(pl.program_id(1) == 0)
    def _():
        new_ref[...] = score(logits(ikn_ref[...]))


def _idx_sample(cache_idx_k, layer, page_table, iq_rows, w_rows, ik_new_pad, pp):
    db, n_pages = page_table.shape
    page = cache_idx_k.shape[2]
    rows = iq_rows.shape[1]
    r8 = rows // IDX_HEADS
    page_specs = [pl.BlockSpec((None, None, page, IDX_DIM),
                               functools.partial(lambda b, p, pt, j: (layer, pt[b, p * pp + j], 0, 0), j=j))
                  for j in range(pp)]
    return pl.pallas_call(
        functools.partial(_idx_sample_kernel, pp=pp, idx_scale=(IDX_HEADS ** -0.5) * (IDX_DIM ** -0.5)),
        grid_spec=pltpu.PrefetchScalarGridSpec(
            num_scalar_prefetch=1,
            grid=(db, n_pages // pp),
            in_specs=[pl.BlockSpec((None, rows, IDX_DIM), lambda b, p, pt: (b, 0, 0)),
                      pl.BlockSpec((None, rows, 1), lambda b, p, pt: (b, 0, 0)),
                      pl.BlockSpec((None, page, IDX_DIM), lambda b, p, pt: (b, 0, 0))] + page_specs,
            out_specs=[pl.BlockSpec((None, r8, pp * page), lambda b, p, pt: (b, 0, p)),
                       pl.BlockSpec((None, r8, page), lambda b, p, pt: (b, 0, 0))]),
        out_shape=[jax.ShapeDtypeStruct((db, r8, n_pages * page), F32),
                   jax.ShapeDtypeStruct((db, r8, page), F32)],
        compiler_params=_cparams(("arbitrary", "arbitrary")),
        name="idx_sample",
    )(page_table, iq_rows, w_rows, ik_new_pad, *([cache_idx_k] * pp))


def _select_sample_kernel(s_ref, o_ref, *, topk, nbits, past_len):
    sc = s_ref[...]
    rows, length = sc.shape
    qpos = past_len + lax.broadcasted_iota(I32, (rows, 1), 0) // KV_GROUP
    kpos = lax.broadcasted_iota(I32, (rows, length), 1)
    causal = kpos <= qpos
    key = _sortable(jnp.where(causal, sc, -jnp.inf))

    def count(pred):
        return jnp.sum(jnp.where(pred, 1.0, 0.0), axis=1, keepdims=True)

    thr = _kth_threshold(lambda cand: count(key >= cand), float(topk), (rows, 1))
    need = float(topk) - count(key > thr)
    cut = _tie_cutoff(lambda cand: count((key == thr) & (kpos < cand)), need, nbits, (rows, 1))
    sel = ((key > thr) | ((key == thr) & (kpos <= cut))) & causal
    o_ref[...] = jnp.where(sel, 1.0, 0.0)


def _select_sample(scores, past_len, n_new):
    db, rows, length = scores.shape
    topk = min(TOPK_MAX, (past_len + n_new) // 4)
    return pl.pallas_call(
        functools.partial(_select_sample_kernel, topk=topk, nbits=max(1, (length - 1).bit_length()), past_len=past_len),
        grid=(db,),
        in_specs=[pl.BlockSpec((None, rows, length), lambda b: (b, 0, 0))],
        out_specs=pl.BlockSpec((None, rows, length), lambda b: (b, 0, 0)),
        out_shape=jax.ShapeDtypeStruct(scores.shape, F32),
        compiler_params=_cparams(("arbitrary",)),
        name="select_sample",
    )(scores)


def _flat_bf16(ref):
    x = ref[...]
    return x.reshape(x.shape[0] * x.shape[1], x.shape[2]).astype(BF16)


def _dsa_sample_kernel(pt_ref, q_ref, mask_ref, e_ref, valid_ref, kn_ref, vn_ref, *rest,
                       pp, n_steps):
    kpages, vpages = rest[:pp], rest[pp:2 * pp]
    o_ref, m_ref, l_ref, acc_ref = rest[2 * pp:]
    p = pl.program_id(1)
    q = q_ref[...]
    rows = q.shape[0]
    reps = rows // mask_ref.shape[0]
    valid = valid_ref[...]

    @pl.when(p == 0)
    def _():
        m_ref[...] = jnp.full(m_ref.shape, NEG, F32)
        l_ref[...] = jnp.zeros(l_ref.shape, F32)
        acc_ref[...] = jnp.zeros(acc_ref.shape, F32)

    def pages(krefs, vrefs):
        ss = []
        for j, kref in enumerate(krefs):
            s = lax.dot_general(q, _flat_bf16(kref), _NT, preferred_element_type=F32)
            mrows = jnp.concatenate([mask_ref[:, j * LANES:(j + 1) * LANES]] * reps, axis=0).astype(BF16)
            keep = jnp.dot(mrows, e_ref[...], preferred_element_type=F32) * valid > 0.5
            ss.append(jnp.where(keep, s, NEG))
        m = m_ref[...]
        m_new = m
        for s in ss:
            m_new = jnp.maximum(m_new, jnp.max(s, axis=1, keepdims=True))
        alpha = jnp.exp2(m - m_new)
        l = alpha * l_ref[...]
        acc = alpha * acc_ref[...]
        for s, vref in zip(ss, vrefs):
            pr = jnp.exp2(s - m_new)
            l = l + jnp.sum(pr, axis=1, keepdims=True)
            acc = acc + jnp.dot(pr.astype(BF16), _flat_bf16(vref), preferred_element_type=F32)
        l_ref[...] = l
        acc_ref[...] = acc
        m_ref[...] = m_new

    @pl.when(p < n_steps - 1)
    def _():
        pages(kpages, vpages)

    @pl.when(p == n_steps - 1)
    def _():
        pages([kn_ref], [vn_ref])
        o_ref[...] = (acc_ref[...] / l_ref[...]).astype(o_ref.dtype)


def _dsa_sample(cache_k, cache_v, layer, page_table, q_rows, mask, k_new_pad, v_new_pad, pp):
    db, n_pages = page_table.shape
    page, n_kv = cache_k.shape[2], cache_k.shape[3]
    rows = q_rows.shape[1]
    n_steps = n_pages // pp + 1
    expand, _, valid = _page_consts(page, n_kv, rows)

    def cache_spec(j):
        return pl.BlockSpec((None, None, page, n_kv, HEAD_DIM),
                            lambda b, p, pt: (layer, pt[b, jnp.minimum(p, n_steps - 2) * pp + j], 0, 0, 0))

    new_spec = pl.BlockSpec((None, page, n_kv, HEAD_DIM), lambda b, p, pt: (b, 0, 0, 0))
    mask_spec = pl.BlockSpec((None, mask.shape[1], pp * page),
                             lambda b, p, pt: (b, 0, jnp.where(p == n_steps - 1, n_pages // pp, p)))
    return pl.pallas_call(
        functools.partial(_dsa_sample_kernel, pp=pp, n_steps=n_steps),
        grid_spec=pltpu.PrefetchScalarGridSpec(
            num_scalar_prefetch=1,
            grid=(db, n_steps),
            in_specs=[pl.BlockSpec((None, rows, HEAD_DIM), lambda b, p, pt: (b, 0, 0)),
                      mask_spec,
                      pl.BlockSpec(expand.shape, lambda b, p, pt: (0, 0)),
                      pl.BlockSpec(valid.shape, lambda b, p, pt: (0, 0)),
                      new_spec, new_spec] + [cache_spec(j) for j in range(pp)] * 2,
            out_specs=pl.BlockSpec((None, rows, HEAD_DIM), lambda b, p, pt: (b, 0, 0)),
            scratch_shapes=[pltpu.VMEM((rows, 1), F32), pltpu.VMEM((rows, 1), F32), pltpu.VMEM((rows, HEAD_DIM), F32)]),
        out_shape=jax.ShapeDtypeStruct((db, rows, HEAD_DIM), BF16),
        compiler_params=_cparams(("arbitrary", "arbitrary")),
        name="dsa_sample",
    )(page_table, q_rows, mask, expand, valid, k_new_pad, v_new_pad, *([cache_k] * pp), *([cache_v] * pp))


def _sb_sample_kernel(pt_ref, q_ref, e_ref, c_ref, valid_ref, u_ref, kn_ref, vn_ref, *rest,
                      pp, n_pages, past_len):
    kpages, vpages = rest[:pp], rest[pp:2 * pp]
    o_ref, tail_ref, acc_ref = rest[2 * pp:]
    p = pl.program_id(1)
    q = q_ref[...]
    rows = q.shape[0]
    n_kv = valid_ref.shape[1] // e_ref.shape[0]
    valid = valid_ref[...]
    page_len = e_ref.shape[0]
    qpos = past_len + (lax.broadcasted_iota(I32, (rows, 1), 0) % (rows // n_kv)) // KV_GROUP

    @pl.when(p == 0)
    def _():
        tail_ref[...] = jnp.zeros(tail_ref.shape, F32)
        acc_ref[...] = jnp.zeros(acc_ref.shape, F32)

    def pages(krefs, vrefs, starts):
        zfs = [lax.dot_general(q, _flat_bf16(kref), _NT, preferred_element_type=F32) * valid for kref in krefs]
        zs = []
        for zf in zfs:
            hi, lo = _split_bf16(zf)
            zz = jnp.dot(jnp.concatenate([hi, lo], axis=0), c_ref[...], preferred_element_type=F32)
            zs.append(zz[:rows] + zz[rows:])
        parts = []
        tail = tail_ref[...]
        for z, start in zip(zs, starts):
            mask = (start + lax.broadcasted_iota(I32, (rows, page_len), 1)) < qpos
            sp = _softplus2(z)
            lk = jnp.where(mask, -sp, 0.0)
            hi, lo = _split_bf16(lk)
            ll = jnp.dot(jnp.concatenate([hi, lo], axis=0), u_ref[...], preferred_element_type=F32)
            parts.append((mask, z - sp + tail, ll))
            tail = tail + jnp.sum(lk, axis=1, keepdims=True)
        tail_ref[...] = tail
        acc = acc_ref[...]
        for (mask, base, ll), vref in zip(parts, vrefs):
            a = jnp.where(mask, jnp.exp2(base + ll[:rows] + ll[rows:]), 0.0).astype(BF16)
            ae = (jnp.dot(a, e_ref[...], preferred_element_type=F32) * valid).astype(BF16)
            acc = acc + jnp.dot(ae, _flat_bf16(vref), preferred_element_type=F32)
        acc_ref[...] = acc

    @pl.when(p == 0)
    def _():
        pages([kn_ref], [vn_ref], [past_len])

    @pl.when(p > 0)
    def _():
        pages(kpages, vpages, [(n_pages - 1 - ((p - 1) * pp + j)) * page_len for j in range(pp)])

    @pl.when(p == pl.num_programs(1) - 1)
    def _():
        o_ref[...] = acc_ref[...].astype(o_ref.dtype)


def _sb_sample(cache_k, cache_v, layer, page_table, q_rows, k_new_pad, v_new_pad, pp):
    db, n_pages = page_table.shape
    page, n_kv = cache_k.shape[2], cache_k.shape[3]
    rows = q_rows.shape[1]
    n_steps = n_pages // pp + 1
    expand, compact, valid = _page_consts(page, n_kv, rows)

    def cache_spec(j):
        return pl.BlockSpec((None, None, page, n_kv, HEAD_DIM),
                            lambda b, p, pt: (layer, pt[b, n_pages - 1 - (jnp.maximum(p - 1, 0) * pp + j)], 0, 0, 0))

    new_spec = pl.BlockSpec((None, page, n_kv, HEAD_DIM), lambda b, p, pt: (b, 0, 0, 0))
    const = lambda a: pl.BlockSpec(a.shape, lambda b, p, pt: (0, 0))
    u = _suffix_matrix(page)
    return pl.pallas_call(
        functools.partial(_sb_sample_kernel, pp=pp, n_pages=n_pages, past_len=n_pages * page),
        grid_spec=pltpu.PrefetchScalarGridSpec(
            num_scalar_prefetch=1,
            grid=(db, n_steps),
            in_specs=[pl.BlockSpec((None, rows, HEAD_DIM), lambda b, p, pt: (b, 0, 0)),
                      const(expand), const(compact), const(valid), const(u),
                      new_spec, new_spec] + [cache_spec(j) for j in range(pp)] * 2,
            out_specs=pl.BlockSpec((None, rows, HEAD_DIM), lambda b, p, pt: (b, 0, 0)),
            scratch_shapes=[pltpu.VMEM((rows, 1), F32), pltpu.VMEM((rows, HEAD_DIM), F32)]),
        out_shape=jax.ShapeDtypeStruct((db, rows, HEAD_DIM), BF16),
        compiler_params=_cparams(("arbitrary", "arbitrary")),
        name="sb_sample",
    )(page_table, q_rows, expand, compact, valid, u, k_new_pad, v_new_pad, *([cache_k] * pp), *([cache_v] * pp))


def _rope_tables(pos):
    half = HEAD_DIM // 2
    inv = ROPE_THETA ** (-jnp.arange(half, dtype=F32) / half)
    ang = pos.astype(F32)[:, None] * inv[None, :]
    cos, sin = jnp.cos(ang), jnp.sin(ang)
    return jnp.concatenate([cos, cos], axis=1), jnp.concatenate([-sin, sin], axis=1)


def _pick_tile(n, prefs):
    for t in prefs:
        if n % t == 0:
            return t
    return n


def _rows_to_heads(x, db, t, n_kv):
    return x.reshape(db, t, n_kv, KV_GROUP, HEAD_DIM).transpose(0, 2, 1, 3, 4).reshape(db, n_kv * t * KV_GROUP, HEAD_DIM)


def _heads_to_rows(x, db, t, n_kv):
    return x.reshape(db, n_kv, t, KV_GROUP, HEAD_DIM).transpose(0, 2, 1, 3, 4).reshape(db * t, n_kv * KV_GROUP * HEAD_DIM)


def _chunk_t(v, nb, s_len, ck):
    return v.reshape(nb, s_len // ck, ck, v.shape[1]).transpose(0, 1, 3, 2)


def _pad_page(x, page):
    return jnp.pad(x, [(0, 0), (0, page - x.shape[1])] + [(0, 0)] * (x.ndim - 2))


def kernel(x_prompt, x_sample, c_prompt, c_sample, cache_k, cache_v, cache_idx_k, page_table, norm_mix_g, norm_ffn_g,
           w_ada, b_ada, w_in_dsa, w_in_sb, w_out, w_gate, w_up, w_down, norm_final_g):
    nb, s_len, d = x_prompt.shape
    db, t_new, _ = x_sample.shape
    depth = w_ada.shape[0]
    page, n_kv = cache_k.shape[2], cache_k.shape[3]
    n_pages = page_table.shape[1]
    past_len = n_pages * page
    attn_w = w_out.shape[1]
    kv_w = n_kv * HEAD_DIM
    idx_w = IDX_HEADS * IDX_DIM
    mp, ms = nb * s_len, db * t_new

    tm = _pick_tile(s_len, (1024, 512, 256, 128))
    tq_dsa, tq_sb, ck = 256, 256, 256
    ng = _pick_tile(n_kv, (4, 2, 1))
    pp = _pick_tile(n_pages, (4, 2, 1))
    attn_scale = HEAD_DIM ** -0.5 * LOG2E

    n_c = nb + db
    c_all = jnp.pad(jnp.concatenate([c_prompt, c_sample], axis=0), ((0, (-n_c) % 16), (0, 0)))
    mod = _adaln(c_all, w_ada, b_ada).reshape(depth, c_all.shape[0], 6, d)

    cos_p, sin_p = _rope_tables(jnp.arange(s_len))
    cos_s, sin_s = _rope_tables(jnp.tile(past_len + jnp.arange(t_new), db))

    xp = x_prompt.reshape(mp, d)
    xs = x_sample.reshape(ms, d)
    outs = {n: [] for n in ("kp", "vp", "ikp", "ks", "vs", "iks")}

    for l in range(depth):
        mp_l = [mod[l, :nb, j].reshape(nb, 1, d) for j in range(6)]
        ms_l = [jnp.repeat(mod[l, nb:n_c, j], t_new, axis=0) for j in range(6)]
        hp = _norm(xp, norm_mix_g[l], mp_l[1], mp_l[0], tm, s_len)
        hs = _norm(xs, norm_mix_g[l], ms_l[1], ms_l[0], ms, 1)
        i = l // 2
        if l % 2 == 0:
            w = w_in_dsa
            w_tail = jnp.pad(w[i, :, attn_w + 2 * kv_w + idx_w:], ((0, 0), (0, 2 * LANES - IDX_DIM - IDX_HEADS)))[None]

            def project(h, cos, sin, tm_):
                tn = 512
                (q,) = _proj(h, w, i, 0, attn_w, cos, sin, tn // HEAD_DIM, (BF16,), tm_, tn, attn_scale)
                k32, k16 = _proj(h, w, i, attn_w, kv_w, cos, sin, tn // HEAD_DIM, (F32, BF16), tm_, tn)
                v32, v16 = _proj(h, w, i, attn_w + kv_w, kv_w, cos, sin, 0, (F32, BF16), tm_, tn)
                (iq,) = _proj(h, w, i, attn_w + 2 * kv_w, idx_w, cos, sin, tn // HEAD_DIM, (F32,), tm_, tn)
                (ikiw,) = _proj(h, w_tail, 0, 0, 2 * LANES, cos, sin, 1, (F32,), tm_, 2 * LANES)
                return q, k32, k16, v32, v16, iq, ikiw

            q, k32, k16, v32, v16, iq, ikiw = project(hp, cos_p, sin_p, tm)
            mix_p = _dsa_prompt(q, k16, _chunk_t(v16, nb, s_len, ck), iq, ikiw, ikiw[:, IDX_DIM:IDX_DIM + IDX_HEADS].T,
                                nb, s_len, tq_dsa, ck, ng)
            outs["ikp"].append(ikiw[:, :IDX_DIM].reshape(nb, s_len, IDX_DIM))

            qs, ks32, _, vs32, _, iqs, ikiw_s = project(hs, cos_s, sin_s, ms)
            iks = ikiw_s[:, :IDX_DIM].reshape(db, t_new, IDX_DIM)
            iws = ikiw_s[:, IDX_DIM:IDX_DIM + IDX_HEADS].reshape(db, t_new, IDX_HEADS)
            iq_rows = jnp.broadcast_to(iqs.reshape(db, t_new, IDX_HEADS, 1, IDX_DIM).transpose(0, 2, 1, 3, 4),
                                       (db, IDX_HEADS, t_new, KV_GROUP, IDX_DIM)).reshape(db, -1, IDX_DIM)
            w_rows = jnp.broadcast_to(iws.transpose(0, 2, 1)[..., None], (db, IDX_HEADS, t_new, KV_GROUP)).reshape(db, -1, 1)
            sc_past, sc_new = _idx_sample(cache_idx_k, i, page_table, iq_rows, w_rows, _pad_page(iks, page),
                                          _pick_tile(n_pages, (16, 8, 4, 2, 1)))
            scores = jnp.concatenate([sc_past, sc_new, jnp.zeros((db, sc_new.shape[1], (pp - 1) * page), F32)], axis=2)
            mask = _select_sample(scores, past_len, t_new)
            ks4 = ks32.reshape(db, t_new, n_kv, HEAD_DIM)
            vs4 = vs32.reshape(db, t_new, n_kv, HEAD_DIM)
            mix_s = _dsa_sample(cache_k, cache_v, l, page_table, _rows_to_heads(qs, db, t_new, n_kv), mask,
                                _pad_page(ks4, page), _pad_page(vs4, page), pp)
            mix_s = _heads_to_rows(mix_s, db, t_new, n_kv)
            outs["iks"].append(iks)
        else:
            w = w_in_sb

            def project(h, cos, sin, tm_):
                tn = 512
                (q,) = _proj(h, w, i, 0, attn_w, cos, sin, 0, (BF16,), tm_, tn, attn_scale)
                k32, k16 = _proj(h, w, i, attn_w, kv_w, cos, sin, 0, (F32, BF16), tm_, tn)
                v32, v16 = _proj(h, w, i, attn_w + kv_w, kv_w, cos, sin, 0, (F32, BF16), tm_, tn)
                return q, k32, k16, v32, v16

            q, k32, k16, v32, v16 = project(hp, cos_p, sin_p, tm)
            mix_p = _sb_prompt(q, k16, _chunk_t(v16, nb, s_len, ck), nb, s_len, tq_sb, ck, ng)
            qs, ks32, _, vs32, _ = project(hs, cos_s, sin_s, ms)
            ks4 = ks32.reshape(db, t_new, n_kv, HEAD_DIM)
            vs4 = vs32.reshape(db, t_new, n_kv, HEAD_DIM)
            mix_s = _sb_sample(cache_k, cache_v, l, page_table, _rows_to_heads(qs, db, t_new, n_kv),
                               _pad_page(ks4, page), _pad_page(vs4, page), pp)
            mix_s = _heads_to_rows(mix_s, db, t_new, n_kv)
        outs["kp"].append(k32.reshape(nb, s_len, n_kv, HEAD_DIM))
        outs["vp"].append(v32.reshape(nb, s_len, n_kv, HEAD_DIM))
        outs["ks"].append(ks4)
        outs["vs"].append(vs4)

        xp = _gres(mix_p, w_out, l, xp, mp_l[2], tm, 512, s_len)
        xs = _gres(mix_s, w_out, l, xs, ms_l[2], ms, 512, 1)
        hp = _norm(xp, norm_ffn_g[l], mp_l[4], mp_l[3], tm, s_len)
        hs = _norm(xs, norm_ffn_g[l], ms_l[4], ms_l[3], ms, 1)
        tn_ff = _pick_tile(w_gate.shape[2], (512, 256, 128))
        xp = _gres(_swiglu(hp, w_gate, w_up, l, tm, tn_ff), w_down, l, xp, mp_l[5], tm, 256, s_len)
        xs = _gres(_swiglu(hs, w_gate, w_up, l, ms, tn_ff), w_down, l, xs, ms_l[5], ms, 256, 1)

    y_prompt = _final_norm(xp, norm_final_g, tm).reshape(nb, s_len, d)
    y_sample = _final_norm(xs, norm_final_g, ms).reshape(db, t_new, d)
    return (y_prompt, y_sample, jnp.stack(outs["kp"]), jnp.stack(outs["vp"]), jnp.stack(outs["ikp"]),
            jnp.stack(outs["ks"]), jnp.stack(outs["vs"]), jnp.stack(outs["iks"]))
```

```python
import functools

import numpy as np
import jax
import jax.numpy as jnp
from jax import lax
from jax.experimental import pallas as pl
from jax.experimental.pallas import tpu as pltpu

F32 = jnp.float32
BF16 = jnp.bfloat16
I32 = jnp.int32

LANES = 128
HEAD_DIM = 128
KV_GROUP = 2
IDX_HEADS = 16
IDX_DIM = 128
TOPK_MAX = 256
ROPE_THETA = 10000.0
EPS = 1e-6
NEG = -1e30
INT_MIN = -2 ** 31
KEY_NEG_INF = -2139095041
LOG2E = 1.4426950408889634
VMEM_LIMIT = 56 * 1024 * 1024

_NT = (((1,), (1,)), ((), ()))


def _cparams(sem):
    return pltpu.CompilerParams(dimension_semantics=sem, vmem_limit_bytes=VMEM_LIMIT)


def _split_bf16(x):
    hi = x.astype(BF16)
    lo = (x - hi.astype(F32)).astype(BF16)
    return hi, lo


def _sortable(x):
    bits = lax.bitcast_convert_type(x, I32)
    return bits ^ ((bits >> 31) & 0x7FFFFFFF)


def _adaln_kernel(c_ref, w_ref, b_ref, o_ref):
    c = c_ref[...]
    s = (c * jax.nn.sigmoid(c)).astype(BF16)
    o_ref[...] = jnp.dot(s, w_ref[...].astype(BF16), preferred_element_type=F32) + b_ref[...]


def _adaln(c_pad, w_ada, b_ada):
    depth, d, n = w_ada.shape
    r = c_pad.shape[0]
    tn = 1024
    return pl.pallas_call(
        _adaln_kernel,
        grid=(depth, n // tn),
        in_specs=[pl.BlockSpec((r, d), lambda l, j: (0, 0)),
                  pl.BlockSpec((None, d, tn), lambda l, j: (l, 0, j)),
                  pl.BlockSpec((None, 1, tn), lambda l, j: (l, 0, j))],
        out_specs=pl.BlockSpec((None, r, tn), lambda l, j: (l, 0, j)),
        out_shape=jax.ShapeDtypeStruct((depth, r, n), F32),
        compiler_params=_cparams(("arbitrary", "arbitrary")),
        name="adaln",
    )(c_pad, w_ada, b_ada.reshape(depth, 1, n))


def _norm_kernel(x_ref, g_ref, sc_ref, sh_ref, o_ref):
    x = x_ref[...]
    y = x * lax.rsqrt(jnp.mean(x * x, axis=-1, keepdims=True) + EPS) * g_ref[...]
    o_ref[...] = (y * (1.0 + sc_ref[...]) + sh_ref[...]).astype(o_ref.dtype)


def _final_norm_kernel(x_ref, g_ref, o_ref):
    x = x_ref[...]
    o_ref[...] = x * lax.rsqrt(jnp.mean(x * x, axis=-1, keepdims=True) + EPS) * g_ref[...]


def _mod_spec(mod, tm, width, rows_per_batch, col_of):
    if mod.ndim == 3:
        return pl.BlockSpec((None, 1, width), lambda *ij: ((ij[0] if col_of is None else ij[1]) * tm // rows_per_batch, 0,
                                                           0 if col_of is None else ij[0]))
    return pl.BlockSpec((tm, width), lambda *ij: ((ij[0] if col_of is None else ij[1]), 0 if col_of is None else ij[0]))


def _norm(x, g, sc, sh, tm, rows_per_batch):
    m, d = x.shape
    return pl.pallas_call(
        _norm_kernel,
        grid=(m // tm,),
        in_specs=[pl.BlockSpec((tm, d), lambda i: (i, 0)),
                  pl.BlockSpec((1, d), lambda i: (0, 0)),
                  _mod_spec(sc, tm, d, rows_per_batch, None),
                  _mod_spec(sh, tm, d, rows_per_batch, None)],
        out_specs=pl.BlockSpec((tm, d), lambda i: (i, 0)),
        out_shape=jax.ShapeDtypeStruct((m, d), BF16),
        compiler_params=_cparams(("arbitrary",)),
        name="norm_mod",
    )(x, g.reshape(1, d), sc, sh)


def _final_norm(x, g, tm):
    m, d = x.shape
    return pl.pallas_call(
        _final_norm_kernel,
        grid=(m // tm,),
        in_specs=[pl.BlockSpec((tm, d), lambda i: (i, 0)), pl.BlockSpec((1, d), lambda i: (0, 0))],
        out_specs=pl.BlockSpec((tm, d), lambda i: (i, 0)),
        out_shape=jax.ShapeDtypeStruct((m, d), F32),
        compiler_params=_cparams(("arbitrary",)),
        name="final_norm",
    )(x, g.reshape(1, d))


def _proj_kernel(h_ref, w_ref, cos_ref, sin_ref, *rest, rope, n_out, scale):
    outs, wb_ref = rest[:n_out], rest[n_out]

    @pl.when(pl.program_id(1) == 0)
    def _():
        wb_ref[...] = w_ref[...].astype(BF16)

    acc = jnp.dot(h_ref[...], wb_ref[...], preferred_element_type=F32)
    if scale != 1.0:
        acc = acc * scale
    tn = acc.shape[1]
    if rope:
        cos = cos_ref[...]
        sin = sin_ref[...]
        for c in range(tn // HEAD_DIM):
            y = acc[:, c * HEAD_DIM:(c + 1) * HEAD_DIM]
            if c < rope:
                y = y * cos + pltpu.roll(y, HEAD_DIM // 2, 1) * sin
            for o in outs:
                o[:, c * HEAD_DIM:(c + 1) * HEAD_DIM] = y.astype(o.dtype)
    else:
        for o in outs:
            o[...] = acc.astype(o.dtype)


def _proj(h, w, layer, col0, ncols, cos, sin, rope, out_dtypes, tm, tn, scale=1.0):
    m, k = h.shape
    npb = cos.shape[0] // tm
    j0 = col0 // tn
    n_out = len(out_dtypes)
    return pl.pallas_call(
        functools.partial(_proj_kernel, rope=rope, n_out=n_out, scale=scale),
        grid=(ncols // tn, m // tm),
        in_specs=[pl.BlockSpec((tm, k), lambda j, i: (i, 0)),
                  pl.BlockSpec((None, k, tn), lambda j, i: (layer, 0, j + j0)),
                  pl.BlockSpec((tm, HEAD_DIM), lambda j, i: (i % npb, 0)),
                  pl.BlockSpec((tm, HEAD_DIM), lambda j, i: (i % npb, 0))],
        out_specs=[pl.BlockSpec((tm, tn), lambda j, i: (i, j)) for _ in out_dtypes],
        out_shape=[jax.ShapeDtypeStruct((m, ncols), dt) for dt in out_dtypes],
        scratch_shapes=[pltpu.VMEM((k, tn), BF16)],
        compiler_params=_cparams(("arbitrary", "arbitrary")),
        name="proj_rope" if rope else "proj",
    )(h, w, cos, sin)


def _gres_kernel(h_ref, w_ref, x_ref, g_ref, o_ref, wb_ref):
    @pl.when(pl.program_id(1) == 0)
    def _():
        wb_ref[...] = w_ref[...].astype(BF16)

    acc = jnp.dot(h_ref[...], wb_ref[...], preferred_element_type=F32)
    o_ref[...] = x_ref[...] + g_ref[...] * acc


def _gres(h, w, layer, x, gate, tm, tn, rows_per_batch):
    m, k = h.shape
    n = w.shape[2]
    return pl.pallas_call(
        _gres_kernel,
        grid=(n // tn, m // tm),
        in_specs=[pl.BlockSpec((tm, k), lambda j, i: (i, 0)),
                  pl.BlockSpec((None, k, tn), lambda j, i: (layer, 0, j)),
                  pl.BlockSpec((tm, tn), lambda j, i: (i, j)),
                  _mod_spec(gate, tm, tn, rows_per_batch, True)],
        out_specs=pl.BlockSpec((tm, tn), lambda j, i: (i, j)),
        out_shape=jax.ShapeDtypeStruct((m, n), F32),
        scratch_shapes=[pltpu.VMEM((k, tn), BF16)],
        compiler_params=_cparams(("arbitrary", "arbitrary")),
        name="gated_residual",
    )(h, w, x, gate)


def _swiglu_kernel(h_ref, wg_ref, wu_ref, o_ref, wgb_ref, wub_ref):
    @pl.when(pl.program_id(1) == 0)
    def _():
        wgb_ref[...] = wg_ref[...].astype(BF16)
        wub_ref[...] = wu_ref[...].astype(BF16)

    h = h_ref[...]
    a = jnp.dot(h, wgb_ref[...], preferred_element_type=F32)
    b = jnp.dot(h, wub_ref[...], preferred_element_type=F32)
    o_ref[...] = (a * jax.nn.sigmoid(a) * b).astype(o_ref.dtype)


def _swiglu(h, wg, wu, layer, tm, tn):
    m, k = h.shape
    n = wg.shape[2]
    return pl.pallas_call(
        _swiglu_kernel,
        grid=(n // tn, m // tm),
        in_specs=[pl.BlockSpec((tm, k), lambda j, i: (i, 0)),
                  pl.BlockSpec((None, k, tn), lambda j, i: (layer, 0, j)),
                  pl.BlockSpec((None, k, tn), lambda j, i: (layer, 0, j))],
        out_specs=pl.BlockSpec((tm, tn), lambda j, i: (i, j)),
        out_shape=jax.ShapeDtypeStruct((m, n), BF16),
        scratch_shapes=[pltpu.VMEM((k, tn), BF16), pltpu.VMEM((k, tn), BF16)],
        compiler_params=_cparams(("arbitrary", "arbitrary")),
        name="swiglu",
    )(h, wg, wu)


def _kth_threshold(count_ge, topk, shape):
    t0 = jnp.where(count_ge(jnp.zeros(shape, I32)) >= topk, 0, INT_MIN).astype(I32)

    def body(it, t):
        cand = t | jnp.left_shift(jnp.int32(1), 30 - it)
        return jnp.where(count_ge(cand) >= topk, cand, t)

    return lax.fori_loop(0, 31, body, t0)


def _tie_cutoff(count_tie_below, need, nbits, shape):
    def body(it, a):
        cand = a | jnp.left_shift(jnp.int32(1), nbits - 1 - it)
        return jnp.where(count_tie_below(cand) < need, cand, a)

    return lax.fori_loop(0, nbits, body, jnp.zeros(shape, I32))


def _dsa_prompt_kernel(q_ref, k_ref, vt_ref, iq_ref, ik_ref, iwt_ref, o_ref,
                       iqcat_ref, keys_ref, bias_ref, cut_ref, m_ref, l_ref, acc_ref,
                       *, tq, ck, ng, topk, nbits, idx_scale):
    i = pl.program_id(1)
    nkc = ((i + 1) * tq + ck - 1) // ck
    qpos = i * tq + lax.broadcasted_iota(I32, (1, tq), 1)

    def kpos_of(c):
        return c * ck + lax.broadcasted_iota(I32, (ck, tq), 0)

    @pl.when(pl.program_id(2) == 0)
    def _index():
        for h in range(IDX_HEADS):
            hi, lo = _split_bf16(iq_ref[:, h * IDX_DIM:(h + 1) * IDX_DIM])
            iqcat_ref[:, h * 3 * IDX_DIM:h * 3 * IDX_DIM + IDX_DIM] = hi
            iqcat_ref[:, h * 3 * IDX_DIM + IDX_DIM:h * 3 * IDX_DIM + 2 * IDX_DIM] = hi
            iqcat_ref[:, h * 3 * IDX_DIM + 2 * IDX_DIM:(h + 1) * 3 * IDX_DIM] = lo

        def score_chunk(c, carry):
            ikh, ikl = _split_bf16(ik_ref[pl.ds(pl.multiple_of(c * ck, ck), ck), :])
            ikcat = jnp.concatenate([ikh, ikl, ikh], axis=1)
            acc = jnp.zeros((ck, tq), F32)
            for h in range(IDX_HEADS):
                lg = lax.dot_general(ikcat, iqcat_ref[:, h * 3 * IDX_DIM:(h + 1) * 3 * IDX_DIM], _NT,
                                     preferred_element_type=F32)
                acc = acc + iwt_ref[h:h + 1, :] * jnp.maximum(lg, 0.0)
            sc = jnp.where(kpos_of(c) <= qpos, acc * idx_scale, -jnp.inf)
            keys_ref[c] = _sortable(sc)
            return carry

        lax.fori_loop(0, nkc, score_chunk, 0)

        def count(pred):
            def body(c, part):
                m = jnp.where(pred(keys_ref[c], c), 1.0, 0.0)
                return part + jnp.sum(m.reshape(ck // 32, 32, tq), axis=0)
            part = lax.fori_loop(0, nkc, body, jnp.zeros((32, tq), F32))
            return jnp.sum(part, axis=0, keepdims=True)

        thr = _kth_threshold(lambda cand: count(lambda kc, c: kc >= cand), float(topk), (1, tq))
        n_gt = count(lambda kc, c: kc > thr)
        n_ge = count(lambda kc, c: kc >= thr)
        need = float(topk) - n_gt
        crowded = (n_ge > float(topk)) & (thr > KEY_NEG_INF)
        cut_ref[...] = jnp.full((1, tq), 2 ** 30, I32)

        @pl.when(jnp.max(jnp.where(crowded, 1.0, 0.0)) > 0.5)
        def _():
            cut_ref[...] = _tie_cutoff(lambda cand: count(lambda kc, c: (kc == thr) & (kpos_of(c) < cand)),
                                       need, nbits, (1, tq))

        cut = cut_ref[...]

        def bias_chunk(c, carry):
            kc = keys_ref[c]
            kpos = kpos_of(c)
            sel = ((kc > thr) | ((kc == thr) & (kpos <= cut))) & (kpos <= qpos)
            bias_ref[c] = jnp.where(sel, 0.0, NEG)
            return carry

        lax.fori_loop(0, nkc, bias_chunk, 0)

    m_ref[...] = jnp.full(m_ref.shape, NEG, F32)
    l_ref[...] = jnp.zeros(l_ref.shape, F32)
    acc_ref[...] = jnp.zeros(acc_ref.shape, F32)

    nh = ng * KV_GROUP

    def attend(c, carry):
        off = pl.multiple_of(c * ck, ck)
        b = bias_ref[c]

        def scores(h):
            g = h // KV_GROUP
            return lax.dot_general(k_ref[pl.ds(off, ck), g * HEAD_DIM:(g + 1) * HEAD_DIM],
                                   q_ref[:, h * HEAD_DIM:(h + 1) * HEAD_DIM], _NT,
                                   preferred_element_type=F32) + b

        def softmax(h, s):
            m = m_ref[h]
            m_new = jnp.maximum(m, jnp.max(s, axis=0, keepdims=True))
            alpha = jnp.exp2(m - m_new)
            p = jnp.exp2(s - m_new)
            l_ref[h] = alpha * l_ref[h] + jnp.sum(p, axis=0, keepdims=True)
            m_ref[h] = m_new
            return p.astype(BF16), alpha

        def weigh(h, p, alpha):
            g = h // KV_GROUP
            acc_ref[h] = alpha * acc_ref[h] + jnp.dot(vt_ref[c, g * HEAD_DIM:(g + 1) * HEAD_DIM, :], p,
                                                      preferred_element_type=F32)

        ahead = 4
        s = {h: scores(h) for h in range(min(ahead, nh))}
        for h in range(nh):
            p, alpha = softmax(h, s.pop(h))
            if h + ahead < nh:
                s[h + ahead] = scores(h + ahead)
            weigh(h, p, alpha)
        return carry

    lax.fori_loop(0, nkc, attend, 0)
    for h in range(nh):
        out = acc_ref[h] * (1.0 / l_ref[h])
        o_ref[:, h * HEAD_DIM:(h + 1) * HEAD_DIM] = out.T.astype(o_ref.dtype)


def _dsa_prompt(q, k, vt, iq, ikiw, iwt, nb, s_len, tq, ck, ng):
    n_kv = k.shape[1] // HEAD_DIM
    nq = s_len // tq
    topk = min(TOPK_MAX, s_len // 4)
    nh = ng * KV_GROUP
    kern = functools.partial(_dsa_prompt_kernel, tq=tq, ck=ck, ng=ng, topk=topk, nbits=max(1, (s_len - 1).bit_length()),
                             idx_scale=(IDX_HEADS ** -0.5) * (IDX_DIM ** -0.5))
    return pl.pallas_call(
        kern,
        grid=(nb, nq, n_kv // ng),
        in_specs=[pl.BlockSpec((tq, nh * HEAD_DIM), lambda b, i, g: (b * nq + i, g)),
                  pl.BlockSpec((s_len, ng * HEAD_DIM), lambda b, i, g: (b, g)),
                  pl.BlockSpec((None, s_len // ck, ng * HEAD_DIM, ck), lambda b, i, g: (b, 0, g, 0)),
                  pl.BlockSpec((tq, IDX_HEADS * IDX_DIM), lambda b, i, g: (b * nq + i, 0)),
                  pl.BlockSpec((s_len, IDX_DIM), lambda b, i, g: (b, 0)),
                  pl.BlockSpec((IDX_HEADS, tq), lambda b, i, g: (0, b * nq + i))],
        out_specs=pl.BlockSpec((tq, nh * HEAD_DIM), lambda b, i, g: (b * nq + i, g)),
        out_shape=jax.ShapeDtypeStruct(q.shape, BF16),
        scratch_shapes=[pltpu.VMEM((tq, IDX_HEADS * 3 * IDX_DIM), BF16),
                        pltpu.VMEM((s_len // ck, ck, tq), I32),
                        pltpu.VMEM((s_len // ck, ck, tq), F32),
                        pltpu.VMEM((1, tq), I32),
                        pltpu.VMEM((nh, 1, tq), F32),
                        pltpu.VMEM((nh, 1, tq), F32),
                        pltpu.VMEM((nh, HEAD_DIM, tq), F32)],
        compiler_params=_cparams(("arbitrary", "arbitrary", "arbitrary")),
        name="dsa_prompt",
    )(q, k, vt, iq, ikiw, iwt)


def _softplus2(z):
    return jnp.maximum(z, 0.0) + jnp.log2(1.0 + jnp.exp2(-jnp.abs(z)))


def _sb_prompt_kernel(q_ref, k_ref, vt_ref, ln_ref, o_ref, tail_ref, acc_ref, *, tq, ck, ng):
    i = pl.program_id(1)
    nkc = ((i + 1) * tq + ck - 1) // ck
    nh = ng * KV_GROUP
    qpos = i * tq + lax.broadcasted_iota(I32, (1, tq), 1)
    tail_ref[...] = jnp.zeros(tail_ref.shape, F32)
    acc_ref[...] = jnp.zeros(acc_ref.shape, F32)

    def chunk(c, masked):
        off = pl.multiple_of(c * ck, ck)
        mask = (c * ck + lax.broadcasted_iota(I32, (ck, tq), 0)) < qpos if masked else None

        def logits(h):
            g = h // KV_GROUP
            return lax.dot_general(k_ref[pl.ds(off, ck), g * HEAD_DIM:(g + 1) * HEAD_DIM],
                                   q_ref[:, h * HEAD_DIM:(h + 1) * HEAD_DIM], _NT, preferred_element_type=F32)

        def keep(h, z):
            sp = jnp.where(z > 64.0, z, jnp.log2(1.0 + jnp.exp2(z)))
            spm = jnp.where(mask, sp, 0.0) if masked else sp
            la = jnp.dot(ln_ref[...], spm.astype(BF16), preferred_element_type=F32)
            tail = tail_ref[h]
            tail_ref[h] = tail + la[ck:ck + 1]
            return z - sp + tail, la

        def weigh(h, base, la):
            g = h // KV_GROUP
            a = jnp.exp2(base + la[:ck])
            if masked:
                a = jnp.where(mask, a, 0.0)
            acc_ref[h] += jnp.dot(vt_ref[c, g * HEAD_DIM:(g + 1) * HEAD_DIM, :], a.astype(BF16),
                                  preferred_element_type=F32)

        a1, a2 = 2, 4
        z = {h: logits(h) for h in range(min(a2, nh))}
        kept = {h: keep(h, z.pop(h)) for h in range(min(a1, nh))}
        for h in range(nh):
            if h + a1 < nh:
                kept[h + a1] = keep(h + a1, z.pop(h + a1))
            if h + a2 < nh:
                z[h + a2] = logits(h + a2)
            weigh(h, *kept.pop(h))

    chunk(nkc - 1, True)

    def body(r, carry):
        chunk(nkc - 1 - r, False)
        return carry

    lax.fori_loop(1, nkc, body, 0)
    for h in range(nh):
        o_ref[:, h * HEAD_DIM:(h + 1) * HEAD_DIM] = acc_ref[h].T.astype(o_ref.dtype)


def _suffix_matrix(n):
    return jnp.asarray(np.tril(np.ones((n, n), np.float32), -1), dtype=BF16)


def _sb_prompt(q, k, vt, nb, s_len, tq, ck, ng):
    assert ck % tq == 0
    n_kv = k.shape[1] // HEAD_DIM
    nq = s_len // tq
    gw = KV_GROUP * HEAD_DIM
    ln = jnp.concatenate([-_suffix_matrix(ck).T, -jnp.ones((8, ck), BF16)], axis=0)
    return pl.pallas_call(
        functools.partial(_sb_prompt_kernel, tq=tq, ck=ck, ng=ng),
        grid=(nb, nq, n_kv // ng),
        in_specs=[pl.BlockSpec((tq, ng * gw), lambda b, i, g: (b * nq + i, g)),
                  pl.BlockSpec((s_len, ng * HEAD_DIM), lambda b, i, g: (b, g)),
                  pl.BlockSpec((None, s_len // ck, ng * HEAD_DIM, ck), lambda b, i, g: (b, 0, g, 0)),
                  pl.BlockSpec((ck + 8, ck), lambda b, i, g: (0, 0))],
        out_specs=pl.BlockSpec((tq, ng * gw), lambda b, i, g: (b * nq + i, g)),
        out_shape=jax.ShapeDtypeStruct(q.shape, BF16),
        scratch_shapes=[pltpu.VMEM((ng * KV_GROUP, 1, tq), F32),
                        pltpu.VMEM((ng * KV_GROUP, HEAD_DIM, tq), F32)],
        compiler_params=_cparams(("arbitrary", "arbitrary", "arbitrary")),
        name="sb_prompt",
    )(q, k, vt, ln)


def _page_consts(page, n_kv, rows):
    lane = np.arange(page * n_kv)
    expand = (lane[None, :] // n_kv == np.arange(page)[:, None]).astype(np.float32)
    valid = (lane[None, :] % n_kv == (np.arange(rows)[:, None] // (rows // n_kv))).astype(np.float32)
    return jnp.asarray(expand, BF16), jnp.asarray(expand.T.copy(), BF16), jnp.asarray(valid, F32)


def _idx_sample_kernel(pt_ref, iq_ref, w_ref, ikn_ref, *rest, pp, idx_scale):
    pages, (past_ref, new_ref) = rest[:pp], rest[pp:]
    hi, lo = _split_bf16(iq_ref[...])
    iqcat = jnp.concatenate([hi, hi, lo], axis=1)
    w = w_ref[...]
    rows = iqcat.shape[0]

    def logits(ik):
        ikh, ikl = _split_bf16(ik)
        return lax.dot_general(iqcat, jnp.concatenate([ikh, ikl, ikh], axis=1), _NT, preferred_element_type=F32)

    def score(lg):
        x = jnp.maximum(lg, 0.0) * w
        return jnp.sum(x.reshape(IDX_HEADS, rows // IDX_HEADS, x.shape[1]), axis=0) * idx_scale

    lgs = [logits(pages[j][...]) for j in range(pp)]
    for j in range(pp):
        past_ref[:, j * LANES:(j + 1) * LANES] = score(lgs[j])

    @pl.when(pl.program_id(1) == 0)
    def _():
        new_ref[...] = score(logits(ikn_ref[...]))


def _idx_sample(cache_idx_k, layer, page_table, iq_rows, w_rows, ik_new_pad, pp):
    db, n_pages = page_table.shape
    page = cache_idx_k.shape[2]
    rows = iq_rows.shape[1]
    r8 = rows // IDX_HEADS
    page_specs = [pl.BlockSpec((None, None, page, IDX_DIM),
                               functools.partial(lambda b, p, pt, j: (layer, pt[b, p * pp + j], 0, 0), j=j))
                  for j in range(pp)]
    return pl.pallas_call(
        functools.partial(_idx_sample_kernel, pp=pp, idx_scale=(IDX_HEADS ** -0.5) * (IDX_DIM ** -0.5)),
        grid_spec=pltpu.PrefetchScalarGridSpec(
            num_scalar_prefetch=1,
            grid=(db, n_pages // pp),
            in_specs=[pl.BlockSpec((None, rows, IDX_DIM), lambda b, p, pt: (b, 0, 0)),
                      pl.BlockSpec((None, rows, 1), lambda b, p, pt: (b, 0, 0)),
                      pl.BlockSpec((None, page, IDX_DIM), lambda b, p, pt: (b, 0, 0))] + page_specs,
            out_specs=[pl.BlockSpec((None, r8, pp * page), lambda b, p, pt: (b, 0, p)),
                       pl.BlockSpec((None, r8, page), lambda b, p, pt: (b, 0, 0))]),
        out_shape=[jax.ShapeDtypeStruct((db, r8, n_pages * page), F32),
                   jax.ShapeDtypeStruct((db, r8, page), F32)],
        compiler_params=_cparams(("arbitrary", "arbitrary")),
        name="idx_sample",
    )(page_table, iq_rows, w_rows, ik_new_pad, *([cache_idx_k] * pp))


def _select_sample_kernel(s_ref, o_ref, *, topk, nbits, past_len):
    sc = s_ref[...]
    rows, length = sc.shape
    qpos = past_len + lax.broadcasted_iota(I32, (rows, 1), 0) // KV_GROUP
    kpos = lax.broadcasted_iota(I32, (rows, length), 1)
    causal = kpos <= qpos
    key = _sortable(jnp.where(causal, sc, -jnp.inf))

    def count(pred):
        return jnp.sum(jnp.where(pred, 1.0, 0.0), axis=1, keepdims=True)

    thr = _kth_threshold(lambda cand: count(key >= cand), float(topk), (rows, 1))
    need = float(topk) - count(key > thr)
    cut = _tie_cutoff(lambda cand: count((key == thr) & (kpos < cand)), need, nbits, (rows, 1))
    sel = ((key > thr) | ((key == thr) & (kpos <= cut))) & causal
    o_ref[...] = jnp.where(sel, 1.0, 0.0)


def _select_sample(scores, past_len, n_new):
    db, rows, length = scores.shape
    topk = min(TOPK_MAX, (past_len + n_new) // 4)
    return pl.pallas_call(
        functools.partial(_select_sample_kernel, topk=topk, nbits=max(1, (length - 1).bit_length()), past_len=past_len),
        grid=(db,),
        in_specs=[pl.BlockSpec((None, rows, length), lambda b: (b, 0, 0))],
        out_specs=pl.BlockSpec((None, rows, length), lambda b: (b, 0, 0)),
        out_shape=jax.ShapeDtypeStruct(scores.shape, F32),
        compiler_params=_cparams(("arbitrary",)),
        name="select_sample",
    )(scores)


def _flat_bf16(ref):
    x = ref[...]
    return x.reshape(x.shape[0] * x.shape[1], x.shape[2]).astype(BF16)


def _dsa_sample_kernel(pt_ref, q_ref, mask_ref, e_ref, valid_ref, kn_ref, vn_ref, *rest,
                       pp, n_steps):
    kpages, vpages = rest[:pp], rest[pp:2 * pp]
    o_ref, m_ref, l_ref, acc_ref = rest[2 * pp:]
    p = pl.program_id(1)
    q = q_ref[...]
    rows = q.shape[0]
    reps = rows // mask_ref.shape[0]
    valid = valid_ref[...]

    @pl.when(p == 0)
    def _():
        m_ref[...] = jnp.full(m_ref.shape, NEG, F32)
        l_ref[...] = jnp.zeros(l_ref.shape, F32)
        acc_ref[...] = jnp.zeros(acc_ref.shape, F32)

    def pages(krefs, vrefs):
        ss = []
        for j, kref in enumerate(krefs):
            s = lax.dot_general(q, _flat_bf16(kref), _NT, preferred_element_type=F32)
            mrows = jnp.concatenate([mask_ref[:, j * LANES:(j + 1) * LANES]] * reps, axis=0).astype(BF16)
            keep = jnp.dot(mrows, e_ref[...], preferred_element_type=F32) * valid > 0.5
            ss.append(jnp.where(keep, s, NEG))
        m = m_ref[...]
        m_new = m
        for s in ss:
            m_new = jnp.maximum(m_new, jnp.max(s, axis=1, keepdims=True))
        alpha = jnp.exp2(m - m_new)
        l = alpha * l_ref[...]
        acc = alpha * acc_ref[...]
        for s, vref in zip(ss, vrefs):
            pr = jnp.exp2(s - m_new)
            l = l + jnp.sum(pr, axis=1, keepdims=True)
            acc = acc + jnp.dot(pr.astype(BF16), _flat_bf16(vref), preferred_element_type=F32)
        l_ref[...] = l
        acc_ref[...] = acc
        m_ref[...] = m_new

    @pl.when(p < n_steps - 1)
    def _():
        pages(kpages, vpages)

    @pl.when(p == n_steps - 1)
    def _():
        pages([kn_ref], [vn_ref])
        o_ref[...] = (acc_ref[...] / l_ref[...]).astype(o_ref.dtype)


def _dsa_sample(cache_k, cache_v, layer, page_table, q_rows, mask, k_new_pad, v_new_pad, pp):
    db, n_pages = page_table.shape
    page, n_kv = cache_k.shape[2], cache_k.shape[3]
    rows = q_rows.shape[1]
    n_steps = n_pages // pp + 1
    expand, _, valid = _page_consts(page, n_kv, rows)

    def cache_spec(j):
        return pl.BlockSpec((None, None, page, n_kv, HEAD_DIM),
                            lambda b, p, pt: (layer, pt[b, jnp.minimum(p, n_steps - 2) * pp + j], 0, 0, 0))

    new_spec = pl.BlockSpec((None, page, n_kv, HEAD_DIM), lambda b, p, pt: (b, 0, 0, 0))
    mask_spec = pl.BlockSpec((None, mask.shape[1], pp * page),
                             lambda b, p, pt: (b, 0, jnp.where(p == n_steps - 1, n_pages // pp, p)))
    return pl.pallas_call(
        functools.partial(_dsa_sample_kernel, pp=pp, n_steps=n_steps),
        grid_spec=pltpu.PrefetchScalarGridSpec(
            num_scalar_prefetch=1,
            grid=(db, n_steps),
            in_specs=[pl.BlockSpec((None, rows, HEAD_DIM), lambda b, p, pt: (b, 0, 0)),
                      mask_spec,
                      pl.BlockSpec(expand.shape, lambda b, p, pt: (0, 0)),
                      pl.BlockSpec(valid.shape, lambda b, p, pt: (0, 0)),
                      new_spec, new_spec] + [cache_spec(j) for j in range(pp)] * 2,
            out_specs=pl.BlockSpec((None, rows, HEAD_DIM), lambda b, p, pt: (b, 0, 0)),
            scratch_shapes=[pltpu.VMEM((rows, 1), F32), pltpu.VMEM((rows, 1), F32), pltpu.VMEM((rows, HEAD_DIM), F32)]),
        out_shape=jax.ShapeDtypeStruct((db, rows, HEAD_DIM), BF16),
        compiler_params=_cparams(("arbitrary", "arbitrary")),
        name="dsa_sample",
    )(page_table, q_rows, mask, expand, valid, k_new_pad, v_new_pad, *([cache_k] * pp), *([cache_v] * pp))


def _sb_sample_kernel(pt_ref, q_ref, e_ref, c_ref, valid_ref, u_ref, kn_ref, vn_ref, *rest,
                      pp, n_pages, past_len):
    kpages, vpages = rest[:pp], rest[pp:2 * pp]
    o_ref, tail_ref, acc_ref = rest[2 * pp:]
    p = pl.program_id(1)
    q = q_ref[...]
    rows = q.shape[0]
    n_kv = valid_ref.shape[1] // e_ref.shape[0]
    valid = valid_ref[...]
    page_len = e_ref.shape[0]
    qpos = past_len + (lax.broadcasted_iota(I32, (rows, 1), 0) % (rows // n_kv)) // KV_GROUP

    @pl.when(p == 0)
    def _():
        tail_ref[...] = jnp.zeros(tail_ref.shape, F32)
        acc_ref[...] = jnp.zeros(acc_ref.shape, F32)

    def pages(krefs, vrefs, starts):
        zfs = [lax.dot_general(q, _flat_bf16(kref), _NT, preferred_element_type=F32) * valid for kref in krefs]
        zs = []
        for zf in zfs:
            hi, lo = _split_bf16(zf)
            zz = jnp.dot(jnp.concatenate([hi, lo], axis=0), c_ref[...], preferred_element_type=F32)
            zs.append(zz[:rows] + zz[rows:])
        parts = []
        tail = tail_ref[...]
        for z, start in zip(zs, starts):
            mask = (start + lax.broadcasted_iota(I32, (rows, page_len), 1)) < qpos
            sp = _softplus2(z)
            lk = jnp.where(mask, -sp, 0.0)
            hi, lo = _split_bf16(lk)
            ll = jnp.dot(jnp.concatenate([hi, lo], axis=0), u_ref[...], preferred_element_type=F32)
            parts.append((mask, z - sp + tail, ll))
            tail = tail + jnp.sum(lk, axis=1, keepdims=True)
        tail_ref[...] = tail
        acc = acc_ref[...]
        for (mask, base, ll), vref in zip(parts, vrefs):
            a = jnp.where(mask, jnp.exp2(base + ll[:rows] + ll[rows:]), 0.0).astype(BF16)
            ae = (jnp.dot(a, e_ref[...], preferred_element_type=F32) * valid).astype(BF16)
            acc = acc + jnp.dot(ae, _flat_bf16(vref), preferred_element_type=F32)
        acc_ref[...] = acc

    @pl.when(p == 0)
    def _():
        pages([kn_ref], [vn_ref], [past_len])

    @pl.when(p > 0)
    def _():
        pages(kpages, vpages, [(n_pages - 1 - ((p - 1) * pp + j)) * page_len for j in range(pp)])

    @pl.when(p == pl.num_programs(1) - 1)
    def _():
        o_ref[...] = acc_ref[...].astype(o_ref.dtype)


def _sb_sample(cache_k, cache_v, layer, page_table, q_rows, k_new_pad, v_new_pad, pp):
    db, n_pages = page_table.shape
    page, n_kv = cache_k.shape[2], cache_k.shape[3]
    rows = q_rows.shape[1]
    n_steps = n_pages // pp + 1
    expand, compact, valid = _page_consts(page, n_kv, rows)

    def cache_spec(j):
        return pl.BlockSpec((None, None, page, n_kv, HEAD_DIM),
                            lambda b, p, pt: (layer, pt[b, n_pages - 1 - (jnp.maximum(p - 1, 0) * pp + j)], 0, 0, 0))

    new_spec = pl.BlockSpec((None, page, n_kv, HEAD_DIM), lambda b, p, pt: (b, 0, 0, 0))
    const = lambda a: pl.BlockSpec(a.shape, lambda b, p, pt: (0, 0))
    u = _suffix_matrix(page)
    return pl.pallas_call(
        functools.partial(_sb_sample_kernel, pp=pp, n_pages=n_pages, past_len=n_pages * page),
        grid_spec=pltpu.PrefetchScalarGridSpec(
            num_scalar_prefetch=1,
            grid=(db, n_steps),
            in_specs=[pl.BlockSpec((None, rows, HEAD_DIM), lambda b, p, pt: (b, 0, 0)),
                      const(expand), const(compact), const(valid), const(u),
                      new_spec, new_spec] + [cache_spec(j) for j in range(pp)] * 2,
            out_specs=pl.BlockSpec((None, rows, HEAD_DIM), lambda b, p, pt: (b, 0, 0)),
            scratch_shapes=[pltpu.VMEM((rows, 1), F32), pltpu.VMEM((rows, HEAD_DIM), F32)]),
        out_shape=jax.ShapeDtypeStruct((db, rows, HEAD_DIM), BF16),
        compiler_params=_cparams(("arbitrary", "arbitrary")),
        name="sb_sample",
    )(page_table, q_rows, expand, compact, valid, u, k_new_pad, v_new_pad, *([cache_k] * pp), *([cache_v] * pp))


def _rope_tables(pos):
    half = HEAD_DIM // 2
    inv = ROPE_THETA ** (-jnp.arange(half, dtype=F32) / half)
    ang = pos.astype(F32)[:, None] * inv[None, :]
    cos, sin = jnp.cos(ang), jnp.sin(ang)
    return jnp.concatenate([cos, cos], axis=1), jnp.concatenate([-sin, sin], axis=1)


def _pick_tile(n, prefs):
    for t in prefs:
        if n % t == 0:
            return t
    return n


def _rows_to_heads(x, db, t, n_kv):
    return x.reshape(db, t, n_kv, KV_GROUP, HEAD_DIM).transpose(0, 2, 1, 3, 4).reshape(db, n_kv * t * KV_GROUP, HEAD_DIM)


def _heads_to_rows(x, db, t, n_kv):
    return x.reshape(db, n_kv, t, KV_GROUP, HEAD_DIM).transpose(0, 2, 1, 3, 4).reshape(db * t, n_kv * KV_GROUP * HEAD_DIM)


def _chunk_t(v, nb, s_len, ck):
    return v.reshape(nb, s_len // ck, ck, v.shape[1]).transpose(0, 1, 3, 2)


def _pad_page(x, page):
    return jnp.pad(x, [(0, 0), (0, page - x.shape[1])] + [(0, 0)] * (x.ndim - 2))


def kernel(x_prompt, x_sample, c_prompt, c_sample, cache_k, cache_v, cache_idx_k, page_table, norm_mix_g, norm_ffn_g,
           w_ada, b_ada, w_in_dsa, w_in_sb, w_out, w_gate, w_up, w_down, norm_final_g):
    nb, s_len, d = x_prompt.shape
    db, t_new, _ = x_sample.shape
    depth = w_ada.shape[0]
    page, n_kv = cache_k.shape[2], cache_k.shape[3]
    n_pages = page_table.shape[1]
    past_len = n_pages * page
    attn_w = w_out.shape[1]
    kv_w = n_kv * HEAD_DIM
    idx_w = IDX_HEADS * IDX_DIM
    mp, ms = nb * s_len, db * t_new

    tm = _pick_tile(s_len, (1024, 512, 256, 128))
    tm2 = _pick_tile(s_len, (512, 256, 128))
    tq_dsa, tq_sb, ck = 256, 256, 256
    ng = _pick_tile(n_kv, (4, 2, 1))
    pp = _pick_tile(n_pages, (8, 4, 2, 1))
    attn_scale = HEAD_DIM ** -0.5 * LOG2E

    n_c = nb + db
    c_all = jnp.pad(jnp.concatenate([c_prompt, c_sample], axis=0), ((0, (-n_c) % 16), (0, 0)))
    mod = _adaln(c_all, w_ada, b_ada).reshape(depth, c_all.shape[0], 6, d)

    cos_p, sin_p = _rope_tables(jnp.arange(s_len))
    cos_s, sin_s = _rope_tables(jnp.tile(past_len + jnp.arange(t_new), db))

    xp = x_prompt.reshape(mp, d)
    xs = x_sample.reshape(ms, d)
    outs = {n: [] for n in ("kp", "vp", "ikp", "ks", "vs", "iks")}

    for l in range(depth):
        mp_l = [mod[l, :nb, j].reshape(nb, 1, d) for j in range(6)]
        ms_l = [jnp.repeat(mod[l, nb:n_c, j], t_new, axis=0) for j in range(6)]
        hp = _norm(xp, norm_mix_g[l], mp_l[1], mp_l[0], tm, s_len)
        hs = _norm(xs, norm_mix_g[l], ms_l[1], ms_l[0], ms, 1)
        i = l // 2
        if l % 2 == 0:
            w = w_in_dsa
            w_tail = jnp.pad(w[i, :, attn_w + 2 * kv_w + idx_w:], ((0, 0), (0, 2 * LANES - IDX_DIM - IDX_HEADS)))[None]

            def project(h, cos, sin, tm_):
                tn = 1024
                (q,) = _proj(h, w, i, 0, attn_w, cos, sin, tn // HEAD_DIM, (BF16,), tm_, tn, attn_scale)
                k32, k16 = _proj(h, w, i, attn_w, kv_w, cos, sin, tn // HEAD_DIM, (F32, BF16), tm_, tn)
                v32, v16 = _proj(h, w, i, attn_w + kv_w, kv_w, cos, sin, 0, (F32, BF16), tm_, tn)
                (iq,) = _proj(h, w, i, attn_w + 2 * kv_w, idx_w, cos, sin, tn // HEAD_DIM, (F32,), tm_, tn)
                (ikiw,) = _proj(h, w_tail, 0, 0, 2 * LANES, cos, sin, 1, (F32,), tm_, 2 * LANES)
                return q, k32, k16, v32, v16, iq, ikiw

            q, k32, k16, v32, v16, iq, ikiw = project(hp, cos_p, sin_p, tm2)
            mix_p = _dsa_prompt(q, k16, _chunk_t(v16, nb, s_len, ck), iq, ikiw, ikiw[:, IDX_DIM:IDX_DIM + IDX_HEADS].T,
                                nb, s_len, tq_dsa, ck, ng)
            outs["ikp"].append(ikiw[:, :IDX_DIM].reshape(nb, s_len, IDX_DIM))

            qs, ks32, _, vs32, _, iqs, ikiw_s = project(hs, cos_s, sin_s, ms)
            iks = ikiw_s[:, :IDX_DIM].reshape(db, t_new, IDX_DIM)
            iws = ikiw_s[:, IDX_DIM:IDX_DIM + IDX_HEADS].reshape(db, t_new, IDX_HEADS)
            iq_rows = jnp.broadcast_to(iqs.reshape(db, t_new, IDX_HEADS, 1, IDX_DIM).transpose(0, 2, 1, 3, 4),
                                       (db, IDX_HEADS, t_new, KV_GROUP, IDX_DIM)).reshape(db, -1, IDX_DIM)
            w_rows = jnp.broadcast_to(iws.transpose(0, 2, 1)[..., None], (db, IDX_HEADS, t_new, KV_GROUP)).reshape(db, -1, 1)
            sc_past, sc_new = _idx_sample(cache_idx_k, i, page_table, iq_rows, w_rows, _pad_page(iks, page),
                                          _pick_tile(n_pages, (16, 8, 4, 2, 1)))
            scores = jnp.concatenate([sc_past, sc_new, jnp.zeros((db, sc_new.shape[1], (pp - 1) * page), F32)], axis=2)
            mask = _select_sample(scores, past_len, t_new)
            ks4 = ks32.reshape(db, t_new, n_kv, HEAD_DIM)
            vs4 = vs32.reshape(db, t_new, n_kv, HEAD_DIM)
            mix_s = _dsa_sample(cache_k, cache_v, l, page_table, _rows_to_heads(qs, db, t_new, n_kv), mask,
                                _pad_page(ks4, page), _pad_page(vs4, page), pp)
            mix_s = _heads_to_rows(mix_s, db, t_new, n_kv)
            outs["iks"].append(iks)
        else:
            w = w_in_sb

            def project(h, cos, sin, tm_):
                tn = 1024
                (q,) = _proj(h, w, i, 0, attn_w, cos, sin, 0, (BF16,), tm_, tn, attn_scale)
                k32, k16 = _proj(h, w, i, attn_w, kv_w, cos, sin, 0, (F32, BF16), tm_, tn)
                v32, v16 = _proj(h, w, i, attn_w + kv_w, kv_w, cos, sin, 0, (F32, BF16), tm_, tn)
                return q, k32, k16, v32, v16

            q, k32, k16, v32, v16 = project(hp, cos_p, sin_p, tm2)
            mix_p = _sb_prompt(q, k16, _chunk_t(v16, nb, s_len, ck), nb, s_len, tq_sb, ck, ng)
            qs, ks32, _, vs32, _ = project(hs, cos_s, sin_s, ms)
            ks4 = ks32.reshape(db, t_new, n_kv, HEAD_DIM)
            vs4 = vs32.reshape(db, t_new, n_kv, HEAD_DIM)
            mix_s = _sb_sample(cache_k, cache_v, l, page_table, _rows_to_heads(qs, db, t_new, n_kv),
                               _pad_page(ks4, page), _pad_page(vs4, page), pp)
            mix_s = _heads_to_rows(mix_s, db, t_new, n_kv)
        outs["kp"].append(k32.reshape(nb, s_len, n_kv, HEAD_DIM))
        outs["vp"].append(v32.reshape(nb, s_len, n_kv, HEAD_DIM))
        outs["ks"].append(ks4)
        outs["vs"].append(vs4)

        xp = _gres(mix_p, w_out, l, xp, mp_l[2], tm2, 1024, s_len)
        xs = _gres(mix_s, w_out, l, xs, ms_l[2], ms, 1024, 1)
        hp = _norm(xp, norm_ffn_g[l], mp_l[4], mp_l[3], tm, s_len)
        hs = _norm(xs, norm_ffn_g[l], ms_l[4], ms_l[3], ms, 1)
        tn_ff = _pick_tile(w_gate.shape[2], (512, 256, 128))
        xp = _gres(_swiglu(hp, w_gate, w_up, l, tm, tn_ff), w_down, l, xp, mp_l[5], tm2, 512, s_len)
        xs = _gres(_swiglu(hs, w_gate, w_up, l, ms, tn_ff), w_down, l, xs, ms_l[5], ms, 512, 1)

    y_prompt = _final_norm(xp, norm_final_g, tm).reshape(nb, s_len, d)
    y_sample = _final_norm(xs, norm_final_g, ms).reshape(db, t_new, d)
    return (y_prompt, y_sample, jnp.stack(outs["kp"]), jnp.stack(outs["vp"]), jnp.stack(outs["ikp"]),
            jnp.stack(outs["ks"]), jnp.stack(outs["vs"]), jnp.stack(outs["iks"]))
```

```python
import functools

import numpy as np
import jax
import jax.numpy as jnp
from jax import lax
from jax.experimental import pallas as pl
from jax.experimental.pallas import tpu as pltpu

F32 = jnp.float32
BF16 = jnp.bfloat16
I32 = jnp.int32

LANES = 128
HEAD_DIM = 128
KV_GROUP = 2
IDX_HEADS = 16
IDX_DIM = 128
TOPK_MAX = 256
ROPE_THETA = 10000.0
EPS = 1e-6
NEG = -1e30
INT_MIN = -2 ** 31
KEY_NEG_INF = -2139095041
LOG2E = 1.4426950408889634
EXP2_UNDERFLOW = -160.0
VMEM_LIMIT = 56 * 1024 * 1024

_NT = (((1,), (1,)), ((), ()))


def _cparams(sem):
    return pltpu.CompilerParams(dimension_semantics=sem, vmem_limit_bytes=VMEM_LIMIT)


def _split_bf16(x):
    hi = x.astype(BF16)
    lo = (x - hi.astype(F32)).astype(BF16)
    return hi, lo


def _sortable(x):
    bits = lax.bitcast_convert_type(x, I32)
    return bits ^ ((bits >> 31) & 0x7FFFFFFF)


def _adaln_kernel(c_ref, w_ref, b_ref, o_ref):
    c = c_ref[...]
    s = (c * jax.nn.sigmoid(c)).astype(BF16)
    o_ref[...] = jnp.dot(s, w_ref[...].astype(BF16), preferred_element_type=F32) + b_ref[...]


def _adaln(c_pad, w_ada, b_ada):
    depth, d, n = w_ada.shape
    r = c_pad.shape[0]
    tn = 1024
    return pl.pallas_call(
        _adaln_kernel,
        grid=(depth, n // tn),
        in_specs=[pl.BlockSpec((r, d), lambda l, j: (0, 0)),
                  pl.BlockSpec((None, d, tn), lambda l, j: (l, 0, j)),
                  pl.BlockSpec((None, 1, tn), lambda l, j: (l, 0, j))],
        out_specs=pl.BlockSpec((None, r, tn), lambda l, j: (l, 0, j)),
        out_shape=jax.ShapeDtypeStruct((depth, r, n), F32),
        compiler_params=_cparams(("arbitrary", "arbitrary")),
        name="adaln",
    )(c_pad, w_ada, b_ada.reshape(depth, 1, n))


def _norm_kernel(x_ref, g_ref, sc_ref, sh_ref, o_ref):
    x = x_ref[...]
    y = x * lax.rsqrt(jnp.mean(x * x, axis=-1, keepdims=True) + EPS) * g_ref[...]
    o_ref[...] = (y * (1.0 + sc_ref[...]) + sh_ref[...]).astype(o_ref.dtype)


def _final_norm_kernel(x_ref, g_ref, o_ref):
    x = x_ref[...]
    o_ref[...] = x * lax.rsqrt(jnp.mean(x * x, axis=-1, keepdims=True) + EPS) * g_ref[...]


def _mod_spec(mod, tm, width, rows_per_batch, col_of):
    if mod.ndim == 3:
        return pl.BlockSpec((None, 1, width), lambda *ij: ((ij[0] if col_of is None else ij[1]) * tm // rows_per_batch, 0,
                                                           0 if col_of is None else ij[0]))
    return pl.BlockSpec((tm, width), lambda *ij: ((ij[0] if col_of is None else ij[1]), 0 if col_of is None else ij[0]))


def _norm(x, g, sc, sh, tm, rows_per_batch):
    m, d = x.shape
    return pl.pallas_call(
        _norm_kernel,
        grid=(m // tm,),
        in_specs=[pl.BlockSpec((tm, d), lambda i: (i, 0)),
                  pl.BlockSpec((1, d), lambda i: (0, 0)),
                  _mod_spec(sc, tm, d, rows_per_batch, None),
                  _mod_spec(sh, tm, d, rows_per_batch, None)],
        out_specs=pl.BlockSpec((tm, d), lambda i: (i, 0)),
        out_shape=jax.ShapeDtypeStruct((m, d), BF16),
        compiler_params=_cparams(("arbitrary",)),
        name="norm_mod",
    )(x, g.reshape(1, d), sc, sh)


def _final_norm(x, g, tm):
    m, d = x.shape
    return pl.pallas_call(
        _final_norm_kernel,
        grid=(m // tm,),
        in_specs=[pl.BlockSpec((tm, d), lambda i: (i, 0)), pl.BlockSpec((1, d), lambda i: (0, 0))],
        out_specs=pl.BlockSpec((tm, d), lambda i: (i, 0)),
        out_shape=jax.ShapeDtypeStruct((m, d), F32),
        compiler_params=_cparams(("arbitrary",)),
        name="final_norm",
    )(x, g.reshape(1, d))


def _proj_kernel(h_ref, w_ref, cos_ref, sin_ref, *rest, rope, n_out, scale):
    outs, wb_ref = rest[:n_out], rest[n_out]

    @pl.when(pl.program_id(1) == 0)
    def _():
        wb_ref[...] = w_ref[...].astype(BF16)

    acc = jnp.dot(h_ref[...], wb_ref[...], preferred_element_type=F32)
    if scale != 1.0:
        acc = acc * scale
    tn = acc.shape[1]
    if rope:
        cos = cos_ref[...]
        sin = sin_ref[...]
        for c in range(tn // HEAD_DIM):
            y = acc[:, c * HEAD_DIM:(c + 1) * HEAD_DIM]
            if c < rope:
                y = y * cos + pltpu.roll(y, HEAD_DIM // 2, 1) * sin
            for o in outs:
                o[:, c * HEAD_DIM:(c + 1) * HEAD_DIM] = y.astype(o.dtype)
    else:
        for o in outs:
            o[...] = acc.astype(o.dtype)


def _proj(h, w, layer, col0, ncols, cos, sin, rope, out_dtypes, tm, tn, scale=1.0):
    m, k = h.shape
    npb = cos.shape[0] // tm
    j0 = col0 // tn
    n_out = len(out_dtypes)
    return pl.pallas_call(
        functools.partial(_proj_kernel, rope=rope, n_out=n_out, scale=scale),
        grid=(ncols // tn, m // tm),
        in_specs=[pl.BlockSpec((tm, k), lambda j, i: (i, 0)),
                  pl.BlockSpec((None, k, tn), lambda j, i: (layer, 0, j + j0)),
                  pl.BlockSpec((tm, HEAD_DIM), lambda j, i: (i % npb, 0)),
                  pl.BlockSpec((tm, HEAD_DIM), lambda j, i: (i % npb, 0))],
        out_specs=[pl.BlockSpec((tm, tn), lambda j, i: (i, j)) for _ in out_dtypes],
        out_shape=[jax.ShapeDtypeStruct((m, ncols), dt) for dt in out_dtypes],
        scratch_shapes=[pltpu.VMEM((k, tn), BF16)],
        compiler_params=_cparams(("arbitrary", "arbitrary")),
        name="proj_rope" if rope else "proj",
    )(h, w, cos, sin)


def _gres_kernel(h_ref, w_ref, x_ref, g_ref, o_ref, wb_ref):
    @pl.when(pl.program_id(1) == 0)
    def _():
        wb_ref[...] = w_ref[...].astype(BF16)

    acc = jnp.dot(h_ref[...], wb_ref[...], preferred_element_type=F32)
    o_ref[...] = x_ref[...] + g_ref[...] * acc


def _gres(h, w, layer, x, gate, tm, tn, rows_per_batch):
    m, k = h.shape
    n = w.shape[2]
    return pl.pallas_call(
        _gres_kernel,
        grid=(n // tn, m // tm),
        in_specs=[pl.BlockSpec((tm, k), lambda j, i: (i, 0)),
                  pl.BlockSpec((None, k, tn), lambda j, i: (layer, 0, j)),
                  pl.BlockSpec((tm, tn), lambda j, i: (i, j)),
                  _mod_spec(gate, tm, tn, rows_per_batch, True)],
        out_specs=pl.BlockSpec((tm, tn), lambda j, i: (i, j)),
        out_shape=jax.ShapeDtypeStruct((m, n), F32),
        scratch_shapes=[pltpu.VMEM((k, tn), BF16)],
        compiler_params=_cparams(("arbitrary", "arbitrary")),
        name="gated_residual",
    )(h, w, x, gate)


def _cast_kernel(w_ref, o_ref):
    o_ref[...] = w_ref[...].astype(o_ref.dtype)


def _cast_bf16(w, tk):
    depth, k, n = w.shape
    return pl.pallas_call(
        _cast_kernel,
        grid=(depth, k // tk),
        in_specs=[pl.BlockSpec((None, tk, n), lambda l, i: (l, i, 0))],
        out_specs=pl.BlockSpec((None, tk, n), lambda l, i: (l, i, 0)),
        out_shape=jax.ShapeDtypeStruct(w.shape, BF16),
        compiler_params=_cparams(("arbitrary", "arbitrary")),
        name="cast_bf16",
    )(w)


def _out_norm_kernel(m_ref, w_ref, x_ref, gate_ref, g_ref, sc_ref, sh_ref, xo_ref, ho_ref):
    x = x_ref[...] + gate_ref[...] * jnp.dot(m_ref[...], w_ref[...], preferred_element_type=F32)
    xo_ref[...] = x
    y = x * lax.rsqrt(jnp.mean(x * x, axis=-1, keepdims=True) + EPS) * g_ref[...]
    ho_ref[...] = (y * (1.0 + sc_ref[...]) + sh_ref[...]).astype(ho_ref.dtype)


def _out_norm(mix, wb, layer, x, gate, g, sc, sh, tm, rows_per_batch):
    m, k = mix.shape
    d = wb.shape[2]
    row = lambda i: (i, 0)
    return pl.pallas_call(
        _out_norm_kernel,
        grid=(m // tm,),
        in_specs=[pl.BlockSpec((tm, k), row),
                  pl.BlockSpec((None, k, d), lambda i: (layer, 0, 0)),
                  pl.BlockSpec((tm, d), row),
                  _mod_spec(gate, tm, d, rows_per_batch, None),
                  pl.BlockSpec((1, d), lambda i: (0, 0)),
                  _mod_spec(sc, tm, d, rows_per_batch, None),
                  _mod_spec(sh, tm, d, rows_per_batch, None)],
        out_specs=[pl.BlockSpec((tm, d), row), pl.BlockSpec((tm, d), row)],
        out_shape=[jax.ShapeDtypeStruct((m, d), F32), jax.ShapeDtypeStruct((m, d), BF16)],
        compiler_params=_cparams(("arbitrary",)),
        name="out_proj_norm",
    )(mix, wb, x, gate, g.reshape(1, d), sc, sh)


def _swiglu_kernel(h_ref, wg_ref, wu_ref, o_ref, wgb_ref, wub_ref):
    @pl.when(pl.program_id(1) == 0)
    def _():
        wgb_ref[...] = wg_ref[...].astype(BF16)
        wub_ref[...] = wu_ref[...].astype(BF16)

    h = h_ref[...]
    a = jnp.dot(h, wgb_ref[...], preferred_element_type=F32)
    b = jnp.dot(h, wub_ref[...], preferred_element_type=F32)
    o_ref[...] = (a * jax.nn.sigmoid(a) * b).astype(o_ref.dtype)


def _swiglu(h, wg, wu, layer, tm, tn):
    m, k = h.shape
    n = wg.shape[2]
    return pl.pallas_call(
        _swiglu_kernel,
        grid=(n // tn, m // tm),
        in_specs=[pl.BlockSpec((tm, k), lambda j, i: (i, 0)),
                  pl.BlockSpec((None, k, tn), lambda j, i: (layer, 0, j)),
                  pl.BlockSpec((None, k, tn), lambda j, i: (layer, 0, j))],
        out_specs=pl.BlockSpec((tm, tn), lambda j, i: (i, j)),
        out_shape=jax.ShapeDtypeStruct((m, n), BF16),
        scratch_shapes=[pltpu.VMEM((k, tn), BF16), pltpu.VMEM((k, tn), BF16)],
        compiler_params=_cparams(("arbitrary", "arbitrary")),
        name="swiglu",
    )(h, wg, wu)


def _kth_threshold(count_ge, topk, shape):
    t0 = jnp.where(count_ge(jnp.zeros(shape, I32)) >= topk, 0, INT_MIN).astype(I32)

    def body(it, t):
        cand = t | jnp.left_shift(jnp.int32(1), 30 - it)
        return jnp.where(count_ge(cand) >= topk, cand, t)

    return lax.fori_loop(0, 31, body, t0)


def _tie_cutoff(count_tie_below, need, nbits, shape):
    def body(it, a):
        cand = a | jnp.left_shift(jnp.int32(1), nbits - 1 - it)
        return jnp.where(count_tie_below(cand) < need, cand, a)

    return lax.fori_loop(0, nbits, body, jnp.zeros(shape, I32))


def _dsa_prompt_kernel(q_ref, k_ref, vt_ref, iq_ref, ik_ref, iwt_ref, o_ref,
                       iqcat_ref, keys_ref, bias_ref, cut_ref, m_ref, l_ref, acc_ref,
                       *, tq, ck, ng, topk, nbits, idx_scale):
    i = pl.program_id(1)
    nkc = ((i + 1) * tq + ck - 1) // ck
    qpos = i * tq + lax.broadcasted_iota(I32, (1, tq), 1)

    def kpos_of(c):
        return c * ck + lax.broadcasted_iota(I32, (ck, tq), 0)

    @pl.when(pl.program_id(2) == 0)
    def _index():
        for h in range(IDX_HEADS):
            hi, lo = _split_bf16(iq_ref[:, h * IDX_DIM:(h + 1) * IDX_DIM])
            iqcat_ref[:, h * 2 * IDX_DIM:h * 2 * IDX_DIM + IDX_DIM] = hi
            iqcat_ref[:, h * 2 * IDX_DIM + IDX_DIM:(h + 1) * 2 * IDX_DIM] = lo

        def score_chunk(c, carry):
            ikh = ik_ref[pl.ds(pl.multiple_of(c * ck, ck), ck), :].astype(BF16)
            ikcat = jnp.concatenate([ikh, ikh], axis=1)
            acc = jnp.zeros((ck, tq), F32)
            for h in range(IDX_HEADS):
                lg = lax.dot_general(ikcat, iqcat_ref[:, h * 2 * IDX_DIM:(h + 1) * 2 * IDX_DIM], _NT,
                                     preferred_element_type=F32)
                acc = acc + iwt_ref[h:h + 1, :] * jnp.maximum(lg, 0.0)
            sc = jnp.where(kpos_of(c) <= qpos, acc * idx_scale, -jnp.inf)
            keys_ref[c] = _sortable(sc)
            return carry

        lax.fori_loop(0, nkc, score_chunk, 0)

        def count(pred):
            def body(c, part):
                m = jnp.where(pred(keys_ref[c], c), 1.0, 0.0)
                return part + jnp.sum(m.reshape(ck // 32, 32, tq), axis=0)
            part = lax.fori_loop(0, nkc, body, jnp.zeros((32, tq), F32))
            return jnp.sum(part, axis=0, keepdims=True)

        thr = _kth_threshold(lambda cand: count(lambda kc, c: kc >= cand), float(topk), (1, tq))
        n_gt = count(lambda kc, c: kc > thr)
        n_ge = count(lambda kc, c: kc >= thr)
        need = float(topk) - n_gt
        crowded = (n_ge > float(topk)) & (thr > KEY_NEG_INF)
        cut_ref[...] = jnp.full((1, tq), 2 ** 30, I32)

        @pl.when(jnp.max(jnp.where(crowded, 1.0, 0.0)) > 0.5)
        def _():
            cut_ref[...] = _tie_cutoff(lambda cand: count(lambda kc, c: (kc == thr) & (kpos_of(c) < cand)),
                                       need, nbits, (1, tq))

        cut = cut_ref[...]

        def bias_chunk(c, carry):
            kc = keys_ref[c]
            kpos = kpos_of(c)
            sel = ((kc > thr) | ((kc == thr) & (kpos <= cut))) & (kpos <= qpos)
            bias_ref[c] = jnp.where(sel, 0.0, NEG)
            return carry

        lax.fori_loop(0, nkc, bias_chunk, 0)

    m_ref[...] = jnp.full(m_ref.shape, NEG, F32)
    l_ref[...] = jnp.zeros(l_ref.shape, F32)
    acc_ref[...] = jnp.zeros(acc_ref.shape, F32)

    nh = ng * KV_GROUP

    def attend(c, carry):
        off = pl.multiple_of(c * ck, ck)
        b = bias_ref[c]

        def scores(h):
            g = h // KV_GROUP
            return lax.dot_general(k_ref[pl.ds(off, ck), g * HEAD_DIM:(g + 1) * HEAD_DIM],
                                   q_ref[:, h * HEAD_DIM:(h + 1) * HEAD_DIM], _NT,
                                   preferred_element_type=F32) + b

        def softmax(h, s):
            m = m_ref[h]
            m_new = jnp.maximum(m, jnp.max(s, axis=0, keepdims=True))
            alpha = jnp.exp2(m - m_new)
            p = jnp.exp2(s - m_new)
            l_ref[h] = alpha * l_ref[h] + jnp.sum(p, axis=0, keepdims=True)
            m_ref[h] = m_new
            return p.astype(BF16), alpha

        def weigh(h, p, alpha):
            g = h // KV_GROUP
            acc_ref[h] = alpha * acc_ref[h] + jnp.dot(vt_ref[c, g * HEAD_DIM:(g + 1) * HEAD_DIM, :], p,
                                                      preferred_element_type=F32)

        ahead = 4
        s = {h: scores(h) for h in range(min(ahead, nh))}
        for h in range(nh):
            p, alpha = softmax(h, s.pop(h))
            if h + ahead < nh:
                s[h + ahead] = scores(h + ahead)
            weigh(h, p, alpha)
        return carry

    lax.fori_loop(0, nkc, attend, 0)
    for h in range(nh):
        out = acc_ref[h] * (1.0 / l_ref[h])
        o_ref[:, h * HEAD_DIM:(h + 1) * HEAD_DIM] = out.T.astype(o_ref.dtype)


def _dsa_prompt(q, k, vt, iq, ikiw, iwt, nb, s_len, tq, ck, ng):
    n_kv = k.shape[1] // HEAD_DIM
    nq = s_len // tq
    topk = min(TOPK_MAX, s_len // 4)
    nh = ng * KV_GROUP
    kern = functools.partial(_dsa_prompt_kernel, tq=tq, ck=ck, ng=ng, topk=topk, nbits=max(1, (s_len - 1).bit_length()),
                             idx_scale=(IDX_HEADS ** -0.5) * (IDX_DIM ** -0.5))
    return pl.pallas_call(
        kern,
        grid=(nb, nq, n_kv // ng),
        in_specs=[pl.BlockSpec((tq, nh * HEAD_DIM), lambda b, i, g: (b * nq + i, g)),
                  pl.BlockSpec((s_len, ng * HEAD_DIM), lambda b, i, g: (b, g)),
                  pl.BlockSpec((None, s_len // ck, ng * HEAD_DIM, ck), lambda b, i, g: (b, 0, g, 0)),
                  pl.BlockSpec((tq, IDX_HEADS * IDX_DIM), lambda b, i, g: (b * nq + i, 0)),
                  pl.BlockSpec((s_len, IDX_DIM), lambda b, i, g: (b, 0)),
                  pl.BlockSpec((IDX_HEADS, tq), lambda b, i, g: (0, b * nq + i))],
        out_specs=pl.BlockSpec((tq, nh * HEAD_DIM), lambda b, i, g: (b * nq + i, g)),
        out_shape=jax.ShapeDtypeStruct(q.shape, BF16),
        scratch_shapes=[pltpu.VMEM((tq, IDX_HEADS * 2 * IDX_DIM), BF16),
                        pltpu.VMEM((s_len // ck, ck, tq), I32),
                        pltpu.VMEM((s_len // ck, ck, tq), F32),
                        pltpu.VMEM((1, tq), I32),
                        pltpu.VMEM((nh, 1, tq), F32),
                        pltpu.VMEM((nh, 1, tq), F32),
                        pltpu.VMEM((nh, HEAD_DIM, tq), F32)],
        compiler_params=_cparams(("arbitrary", "arbitrary", "arbitrary")),
        name="dsa_prompt",
    )(q, k, vt, iq, ikiw, iwt)


def _softplus2(z):
    return jnp.maximum(z, 0.0) + jnp.log2(1.0 + jnp.exp2(-jnp.abs(z)))


def _sb_prompt_kernel(q_ref, k_ref, vt_ref, ln_ref, o_ref, tail_ref, acc_ref, *, tq, ck, ng):
    i = pl.program_id(1)
    nkc = ((i + 1) * tq + ck - 1) // ck
    nh = ng * KV_GROUP
    qpos = i * tq + lax.broadcasted_iota(I32, (1, tq), 1)
    tail_ref[...] = jnp.zeros(tail_ref.shape, F32)
    acc_ref[...] = jnp.zeros(acc_ref.shape, F32)

    def chunk(c, masked):
        off = pl.multiple_of(c * ck, ck)
        mask = (c * ck + lax.broadcasted_iota(I32, (ck, tq), 0)) < qpos if masked else None

        def logits(h):
            g = h // KV_GROUP
            return lax.dot_general(k_ref[pl.ds(off, ck), g * HEAD_DIM:(g + 1) * HEAD_DIM],
                                   q_ref[:, h * HEAD_DIM:(h + 1) * HEAD_DIM], _NT, preferred_element_type=F32)

        def keep(h, z):
            sp = jnp.where(z > 64.0, z, jnp.log2(1.0 + jnp.exp2(z)))
            spm = jnp.where(mask, sp, 0.0) if masked else sp
            la = jnp.dot(ln_ref[...], spm.astype(BF16), preferred_element_type=F32)
            tail = tail_ref[h]
            tail_ref[h] = tail + la[ck:ck + 1]
            return z - sp + tail, la

        def weigh(h, base, la):
            g = h // KV_GROUP
            a = jnp.exp2(base + la[:ck])
            if masked:
                a = jnp.where(mask, a, 0.0)
            acc_ref[h] += jnp.dot(vt_ref[c, g * HEAD_DIM:(g + 1) * HEAD_DIM, :], a.astype(BF16),
                                  preferred_element_type=F32)

        a1, a2 = 2, 4
        z = {h: logits(h) for h in range(min(a2, nh))}
        kept = {h: keep(h, z.pop(h)) for h in range(min(a1, nh))}
        for h in range(nh):
            if h + a1 < nh:
                kept[h + a1] = keep(h + a1, z.pop(h + a1))
            if h + a2 < nh:
                z[h + a2] = logits(h + a2)
            weigh(h, *kept.pop(h))

    chunk(nkc - 1, True)

    def live(carry):
        r, top = carry
        return (r < nkc) & (top > EXP2_UNDERFLOW)

    def body(carry):
        r, _ = carry
        chunk(nkc - 1 - r, False)
        return r + 1, jnp.max(tail_ref[...])

    lax.while_loop(live, body, (jnp.int32(1), jnp.max(tail_ref[...])))
    for h in range(nh):
        o_ref[:, h * HEAD_DIM:(h + 1) * HEAD_DIM] = acc_ref[h].T.astype(o_ref.dtype)


def _suffix_matrix(n):
    return jnp.asarray(np.tril(np.ones((n, n), np.float32), -1), dtype=BF16)


def _sb_prompt(q, k, vt, nb, s_len, tq, ck, ng):
    assert ck % tq == 0
    n_kv = k.shape[1] // HEAD_DIM
    nq = s_len // tq
    gw = KV_GROUP * HEAD_DIM
    ln = jnp.concatenate([-_suffix_matrix(ck).T, -jnp.ones((8, ck), BF16)], axis=0)
    return pl.pallas_call(
        functools.partial(_sb_prompt_kernel, tq=tq, ck=ck, ng=ng),
        grid=(nb, nq, n_kv // ng),
        in_specs=[pl.BlockSpec((tq, ng * gw), lambda b, i, g: (b * nq + i, g)),
                  pl.BlockSpec((s_len, ng * HEAD_DIM), lambda b, i, g: (b, g)),
                  pl.BlockSpec((None, s_len // ck, ng * HEAD_DIM, ck), lambda b, i, g: (b, 0, g, 0)),
                  pl.BlockSpec((ck + 8, ck), lambda b, i, g: (0, 0))],
        out_specs=pl.BlockSpec((tq, ng * gw), lambda b, i, g: (b * nq + i, g)),
        out_shape=jax.ShapeDtypeStruct(q.shape, BF16),
        scratch_shapes=[pltpu.VMEM((ng * KV_GROUP, 1, tq), F32),
                        pltpu.VMEM((ng * KV_GROUP, HEAD_DIM, tq), F32)],
        compiler_params=_cparams(("arbitrary", "arbitrary", "arbitrary")),
        name="sb_prompt",
    )(q, k, vt, ln)


def _page_consts(page, n_kv, rows):
    lane = np.arange(page * n_kv)
    expand = (lane[None, :] // n_kv == np.arange(page)[:, None]).astype(np.float32)
    valid = (lane[None, :] % n_kv == (np.arange(rows)[:, None] // (rows // n_kv))).astype(np.float32)
    return jnp.asarray(expand, BF16), jnp.asarray(expand.T.copy(), BF16), jnp.asarray(valid, F32)


def _idx_sample_kernel(pt_ref, iq_ref, w_ref, ikn_ref, *rest, pp, idx_scale):
    pages, (past_ref, new_ref) = rest[:pp], rest[pp:]
    hi, lo = _split_bf16(iq_ref[...])
    iqcat = jnp.concatenate([hi, hi, lo], axis=1)
    w = w_ref[...]
    rows = iqcat.shape[0]

    def logits(ik):
        ikh, ikl = _split_bf16(ik)
        return lax.dot_general(iqcat, jnp.concatenate([ikh, ikl, ikh], axis=1), _NT, preferred_element_type=F32)

    def score(lg):
        x = jnp.maximum(lg, 0.0) * w
        return jnp.sum(x.reshape(IDX_HEADS, rows // IDX_HEADS, x.shape[1]), axis=0) * idx_scale

    lgs = [logits(pages[j][...]) for j in range(pp)]
    for j in range(pp):
        past_ref[:, j * LANES:(j + 1) * LANES] = score(lgs[j])

    @pl.when(pl.program_id(1) == 0)
    def _():
        new_ref[...] = score(logits(ikn_ref[...]))


def _idx_sample(cache_idx_k, layer, page_table, iq_rows, w_rows, ik_new_pad, pp):
    db, n_pages = page_table.shape
    page = cache_idx_k.shape[2]
    rows = iq_rows.shape[1]
    r8 = rows // IDX_HEADS
    page_specs = [pl.BlockSpec((None, None, page, IDX_DIM),
                               functools.partial(lambda b, p, pt, j: (layer, pt[b, p * pp + j], 0, 0), j=j))
                  for j in range(pp)]
    return pl.pallas_call(
        functools.partial(_idx_sample_kernel, pp=pp, idx_scale=(IDX_HEADS ** -0.5) * (IDX_DIM ** -0.5)),
        grid_spec=pltpu.PrefetchScalarGridSpec(
            num_scalar_prefetch=1,
            grid=(db, n_pages // pp),
            in_specs=[pl.BlockSpec((None, rows, IDX_DIM), lambda b, p, pt: (b, 0, 0)),
                      pl.BlockSpec((None, rows, 1), lambda b, p, pt: (b, 0, 0)),
                      pl.BlockSpec((None, page, IDX_DIM), lambda b, p, pt: (b, 0, 0))] + page_specs,
            out_specs=[pl.BlockSpec((None, r8, pp * page), lambda b, p, pt: (b, 0, p)),
                       pl.BlockSpec((None, r8, page), lambda b, p, pt: (b, 0, 0))]),
        out_shape=[jax.ShapeDtypeStruct((db, r8, n_pages * page), F32),
                   jax.ShapeDtypeStruct((db, r8, page), F32)],
        compiler_params=_cparams(("arbitrary", "arbitrary")),
        name="idx_sample",
    )(page_table, iq_rows, w_rows, ik_new_pad, *([cache_idx_k] * pp))


def _select_sample_kernel(s_ref, o_ref, *, topk, nbits, past_len):
    sc = s_ref[...]
    rows, length = sc.shape
    qpos = past_len + lax.broadcasted_iota(I32, (rows, 1), 0) // KV_GROUP
    kpos = lax.broadcasted_iota(I32, (rows, length), 1)
    causal = kpos <= qpos
    key = _sortable(jnp.where(causal, sc, -jnp.inf))

    def count(pred):
        return jnp.sum(jnp.where(pred, 1.0, 0.0), axis=1, keepdims=True)

    thr = _kth_threshold(lambda cand: count(key >= cand), float(topk), (rows, 1))
    need = float(topk) - count(key > thr)
    cut = _tie_cutoff(lambda cand: count((key == thr) & (kpos < cand)), need, nbits, (rows, 1))
    sel = ((key > thr) | ((key == thr) & (kpos <= cut))) & causal
    o_ref[...] = jnp.where(sel, 1.0, 0.0)


def _select_sample(scores, past_len, n_new):
    db, rows, length = scores.shape
    topk = min(TOPK_MAX, (past_len + n_new) // 4)
    return pl.pallas_call(
        functools.partial(_select_sample_kernel, topk=topk, nbits=max(1, (length - 1).bit_length()), past_len=past_len),
        grid=(db,),
        in_specs=[pl.BlockSpec((None, rows, length), lambda b: (b, 0, 0))],
        out_specs=pl.BlockSpec((None, rows, length), lambda b: (b, 0, 0)),
        out_shape=jax.ShapeDtypeStruct(scores.shape, F32),
        compiler_params=_cparams(("arbitrary",)),
        name="select_sample",
    )(scores)


def _flat_bf16(ref):
    x = ref[...]
    return x.reshape(x.shape[0] * x.shape[1], x.shape[2]).astype(BF16)


def _dsa_sample_kernel(pt_ref, q_ref, mask_ref, e_ref, valid_ref, kn_ref, vn_ref, *rest,
                       pp, n_steps):
    kpages, vpages = rest[:pp], rest[pp:2 * pp]
    o_ref, m_ref, l_ref, acc_ref = rest[2 * pp:]
    p = pl.program_id(1)
    q = q_ref[...]
    rows = q.shape[0]
    reps = rows // mask_ref.shape[0]
    valid = valid_ref[...]

    @pl.when(p == 0)
    def _():
        m_ref[...] = jnp.full(m_ref.shape, NEG, F32)
        l_ref[...] = jnp.zeros(l_ref.shape, F32)
        acc_ref[...] = jnp.zeros(acc_ref.shape, F32)

    def pages(krefs, vrefs):
        ss = []
        for j, kref in enumerate(krefs):
            s = lax.dot_general(q, _flat_bf16(kref), _NT, preferred_element_type=F32)
            mrows = jnp.concatenate([mask_ref[:, j * LANES:(j + 1) * LANES]] * reps, axis=0).astype(BF16)
            keep = jnp.dot(mrows, e_ref[...], preferred_element_type=F32) * valid > 0.5
            ss.append(jnp.where(keep, s, NEG))
        m = m_ref[...]
        m_new = m
        for s in ss:
            m_new = jnp.maximum(m_new, jnp.max(s, axis=1, keepdims=True))
        alpha = jnp.exp2(m - m_new)
        l = alpha * l_ref[...]
        acc = alpha * acc_ref[...]
        for s, vref in zip(ss, vrefs):
            pr = jnp.exp2(s - m_new)
            l = l + jnp.sum(pr, axis=1, keepdims=True)
            acc = acc + jnp.dot(pr.astype(BF16), _flat_bf16(vref), preferred_element_type=F32)
        l_ref[...] = l
        acc_ref[...] = acc
        m_ref[...] = m_new

    @pl.when(p < n_steps - 1)
    def _():
        pages(kpages, vpages)

    @pl.when(p == n_steps - 1)
    def _():
        pages([kn_ref], [vn_ref])
        o_ref[...] = (acc_ref[...] / l_ref[...]).astype(o_ref.dtype)


def _dsa_sample(cache_k, cache_v, layer, page_table, q_rows, mask, k_new_pad, v_new_pad, pp):
    db, n_pages = page_table.shape
    page, n_kv = cache_k.shape[2], cache_k.shape[3]
    rows = q_rows.shape[1]
    n_steps = n_pages // pp + 1
    expand, _, valid = _page_consts(page, n_kv, rows)

    def cache_spec(j):
        return pl.BlockSpec((None, None, page, n_kv, HEAD_DIM),
                            lambda b, p, pt: (layer, pt[b, jnp.minimum(p, n_steps - 2) * pp + j], 0, 0, 0))

    new_spec = pl.BlockSpec((None, page, n_kv, HEAD_DIM), lambda b, p, pt: (b, 0, 0, 0))
    mask_spec = pl.BlockSpec((None, mask.shape[1], pp * page),
                             lambda b, p, pt: (b, 0, jnp.where(p == n_steps - 1, n_pages // pp, p)))
    return pl.pallas_call(
        functools.partial(_dsa_sample_kernel, pp=pp, n_steps=n_steps),
        grid_spec=pltpu.PrefetchScalarGridSpec(
            num_scalar_prefetch=1,
            grid=(db, n_steps),
            in_specs=[pl.BlockSpec((None, rows, HEAD_DIM), lambda b, p, pt: (b, 0, 0)),
                      mask_spec,
                      pl.BlockSpec(expand.shape, lambda b, p, pt: (0, 0)),
                      pl.BlockSpec(valid.shape, lambda b, p, pt: (0, 0)),
                      new_spec, new_spec] + [cache_spec(j) for j in range(pp)] * 2,
            out_specs=pl.BlockSpec((None, rows, HEAD_DIM), lambda b, p, pt: (b, 0, 0)),
            scratch_shapes=[pltpu.VMEM((rows, 1), F32), pltpu.VMEM((rows, 1), F32), pltpu.VMEM((rows, HEAD_DIM), F32)]),
        out_shape=jax.ShapeDtypeStruct((db, rows, HEAD_DIM), BF16),
        compiler_params=_cparams(("arbitrary", "arbitrary")),
        name="dsa_sample",
    )(page_table, q_rows, mask, expand, valid, k_new_pad, v_new_pad, *([cache_k] * pp), *([cache_v] * pp))


def _sb_sample_kernel(pt_ref, q_ref, e_ref, c_ref, valid_ref, u_ref, kn_ref, vn_ref, tail_in_ref, acc_in_ref, *rest,
                      pp, top, with_new, past_len):
    kpages, vpages = rest[:pp], rest[pp:2 * pp]
    tail_ref, acc_ref = rest[2 * pp:]
    p = pl.program_id(1)
    q = q_ref[...]
    rows = q.shape[0]
    n_kv = valid_ref.shape[1] // e_ref.shape[0]
    valid = valid_ref[...]
    page_len = e_ref.shape[0]
    qpos = past_len + (lax.broadcasted_iota(I32, (rows, 1), 0) % (rows // n_kv)) // KV_GROUP

    @pl.when(p == 0)
    def _():
        tail_ref[...] = tail_in_ref[...]
        acc_ref[...] = acc_in_ref[...]

    def pages(krefs, vrefs, starts):
        zfs = [lax.dot_general(q, _flat_bf16(kref), _NT, preferred_element_type=F32) * valid for kref in krefs]
        zs = []
        for zf in zfs:
            hi, lo = _split_bf16(zf)
            zz = jnp.dot(jnp.concatenate([hi, lo], axis=0), c_ref[...], preferred_element_type=F32)
            zs.append(zz[:rows] + zz[rows:])
        parts = []
        tail = tail_ref[...]
        for z, start in zip(zs, starts):
            mask = (start + lax.broadcasted_iota(I32, (rows, page_len), 1)) < qpos
            sp = _softplus2(z)
            lk = jnp.where(mask, -sp, 0.0)
            hi, lo = _split_bf16(lk)
            ll = jnp.dot(jnp.concatenate([hi, lo], axis=0), u_ref[...], preferred_element_type=F32)
            parts.append((mask, z - sp + tail, ll))
            tail = tail + jnp.sum(lk, axis=1, keepdims=True)
        tail_ref[...] = tail
        acc = acc_ref[...]
        for (mask, base, ll), vref in zip(parts, vrefs):
            a = jnp.where(mask, jnp.exp2(base + ll[:rows] + ll[rows:]), 0.0).astype(BF16)
            ae = (jnp.dot(a, e_ref[...], preferred_element_type=F32) * valid).astype(BF16)
            acc = acc + jnp.dot(ae, _flat_bf16(vref), preferred_element_type=F32)
        acc_ref[...] = acc

    if with_new:
        @pl.when(p == 0)
        def _():
            pages([kn_ref], [vn_ref], [past_len])

    @pl.when(p >= int(with_new))
    def _():
        pages(kpages, vpages, [(top - 1 - ((p - int(with_new)) * pp + j)) * page_len for j in range(pp)])


def _sb_sample_part(cache_k, cache_v, layer, page_table, q_rows, k_new_pad, v_new_pad, tail, acc, pp, top, count, with_new):
    db, n_pages = page_table.shape
    page, n_kv = cache_k.shape[2], cache_k.shape[3]
    rows = q_rows.shape[1]
    first = int(with_new)
    expand, compact, valid = _page_consts(page, n_kv, rows)

    def cache_spec(j):
        return pl.BlockSpec((None, None, page, n_kv, HEAD_DIM),
                            lambda b, p, pt: (layer, pt[b, top - 1 - (jnp.maximum(p - first, 0) * pp + j)], 0, 0, 0))

    new_spec = pl.BlockSpec((None, page, n_kv, HEAD_DIM), lambda b, p, pt: (b, 0, 0, 0))
    const = lambda a: pl.BlockSpec(a.shape, lambda b, p, pt: (0, 0))
    per_b = lambda a: pl.BlockSpec((None,) + a.shape[1:], lambda b, p, pt: (b, 0, 0))
    u = _suffix_matrix(page)
    return pl.pallas_call(
        functools.partial(_sb_sample_kernel, pp=pp, top=top, with_new=with_new, past_len=n_pages * page),
        grid_spec=pltpu.PrefetchScalarGridSpec(
            num_scalar_prefetch=1,
            grid=(db, count // pp + first),
            in_specs=[per_b(q_rows), const(expand), const(compact), const(valid), const(u),
                      new_spec, new_spec, per_b(tail), per_b(acc)] + [cache_spec(j) for j in range(pp)] * 2,
            out_specs=[per_b(tail), per_b(acc)]),
        out_shape=[jax.ShapeDtypeStruct(tail.shape, F32), jax.ShapeDtypeStruct(acc.shape, F32)],
        compiler_params=_cparams(("arbitrary", "arbitrary")),
        name="sb_sample",
    )(page_table, q_rows, expand, compact, valid, u, k_new_pad, v_new_pad, tail, acc,
      *([cache_k] * pp), *([cache_v] * pp))


def _sb_sample(cache_k, cache_v, layer, page_table, q_rows, k_new_pad, v_new_pad, pp):
    db, n_pages = page_table.shape
    rows = q_rows.shape[1]
    args = (cache_k, cache_v, layer, page_table, q_rows, k_new_pad, v_new_pad)
    state = (jnp.zeros((db, rows, 1), F32), jnp.zeros((db, rows, HEAD_DIM), F32))
    state = tuple(_sb_sample_part(*args, *state, pp, n_pages, pp, True))
    rest = n_pages - pp
    if rest > 0:
        state = lax.cond(jnp.max(state[0]) > EXP2_UNDERFLOW,
                         lambda s: tuple(_sb_sample_part(*args, *s, pp, rest, rest, False)), lambda s: s, state)
    return state[1].astype(BF16)


def _rope_tables(pos):
    half = HEAD_DIM // 2
    inv = ROPE_THETA ** (-jnp.arange(half, dtype=F32) / half)
    ang = pos.astype(F32)[:, None] * inv[None, :]
    cos, sin = jnp.cos(ang), jnp.sin(ang)
    return jnp.concatenate([cos, cos], axis=1), jnp.concatenate([-sin, sin], axis=1)


def _pick_tile(n, prefs):
    for t in prefs:
        if n % t == 0:
            return t
    return n


def _rows_to_heads(x, db, t, n_kv):
    return x.reshape(db, t, n_kv, KV_GROUP, HEAD_DIM).transpose(0, 2, 1, 3, 4).reshape(db, n_kv * t * KV_GROUP, HEAD_DIM)


def _heads_to_rows(x, db, t, n_kv):
    return x.reshape(db, n_kv, t, KV_GROUP, HEAD_DIM).transpose(0, 2, 1, 3, 4).reshape(db * t, n_kv * KV_GROUP * HEAD_DIM)


def _chunk_t(v, nb, s_len, ck):
    return v.reshape(nb, s_len // ck, ck, v.shape[1]).transpose(0, 1, 3, 2)


def _pad_page(x, page):
    return jnp.pad(x, [(0, 0), (0, page - x.shape[1])] + [(0, 0)] * (x.ndim - 2))


def kernel(x_prompt, x_sample, c_prompt, c_sample, cache_k, cache_v, cache_idx_k, page_table, norm_mix_g, norm_ffn_g,
           w_ada, b_ada, w_in_dsa, w_in_sb, w_out, w_gate, w_up, w_down, norm_final_g):
    nb, s_len, d = x_prompt.shape
    db, t_new, _ = x_sample.shape
    depth = w_ada.shape[0]
    page, n_kv = cache_k.shape[2], cache_k.shape[3]
    n_pages = page_table.shape[1]
    past_len = n_pages * page
    attn_w = w_out.shape[1]
    kv_w = n_kv * HEAD_DIM
    idx_w = IDX_HEADS * IDX_DIM
    mp, ms = nb * s_len, db * t_new

    tm = _pick_tile(s_len, (1024, 512, 256, 128))
    tm2 = _pick_tile(s_len, (512, 256, 128))
    tm3 = _pick_tile(s_len, (256, 128))
    w_out16 = _cast_bf16(w_out, _pick_tile(w_out.shape[1], (512, 256, 128)))
    tq_dsa, tq_sb, ck = 256, 256, 256
    ng = _pick_tile(n_kv, (4, 2, 1))
    pp = _pick_tile(n_pages, (8, 4, 2, 1))
    attn_scale = HEAD_DIM ** -0.5 * LOG2E

    n_c = nb + db
    c_all = jnp.pad(jnp.concatenate([c_prompt, c_sample], axis=0), ((0, (-n_c) % 16), (0, 0)))
    mod = _adaln(c_all, w_ada, b_ada).reshape(depth, c_all.shape[0], 6, d)

    cos_p, sin_p = _rope_tables(jnp.arange(s_len))
    cos_s, sin_s = _rope_tables(jnp.tile(past_len + jnp.arange(t_new), db))

    xp = x_prompt.reshape(mp, d)
    xs = x_sample.reshape(ms, d)
    outs = {n: [] for n in ("kp", "vp", "ikp", "ks", "vs", "iks")}

    for l in range(depth):
        mp_l = [mod[l, :nb, j].reshape(nb, 1, d) for j in range(6)]
        ms_l = [jnp.repeat(mod[l, nb:n_c, j], t_new, axis=0) for j in range(6)]
        hp = _norm(xp, norm_mix_g[l], mp_l[1], mp_l[0], tm, s_len)
        hs = _norm(xs, norm_mix_g[l], ms_l[1], ms_l[0], ms, 1)
        i = l // 2
        if l % 2 == 0:
            w = w_in_dsa
            w_tail = jnp.pad(w[i, :, attn_w + 2 * kv_w + idx_w:], ((0, 0), (0, 2 * LANES - IDX_DIM - IDX_HEADS)))[None]

            def project(h, cos, sin, tm_):
                tn = 1024
                (q,) = _proj(h, w, i, 0, attn_w, cos, sin, tn // HEAD_DIM, (BF16,), tm_, tn, attn_scale)
                k32, k16 = _proj(h, w, i, attn_w, kv_w, cos, sin, tn // HEAD_DIM, (F32, BF16), tm_, tn)
                v32, v16 = _proj(h, w, i, attn_w + kv_w, kv_w, cos, sin, 0, (F32, BF16), tm_, tn)
                (iq,) = _proj(h, w, i, attn_w + 2 * kv_w, idx_w, cos, sin, tn // HEAD_DIM, (F32,), tm_, tn)
                (ikiw,) = _proj(h, w_tail, 0, 0, 2 * LANES, cos, sin, 1, (F32,), tm_, 2 * LANES)
                return q, k32, k16, v32, v16, iq, ikiw

            q, k32, k16, v32, v16, iq, ikiw = project(hp, cos_p, sin_p, tm2)
            mix_p = _dsa_prompt(q, k16, _chunk_t(v16, nb, s_len, ck), iq, ikiw, ikiw[:, IDX_DIM:IDX_DIM + IDX_HEADS].T,
                                nb, s_len, tq_dsa, ck, ng)
            outs["ikp"].append(ikiw[:, :IDX_DIM].reshape(nb, s_len, IDX_DIM))

            qs, ks32, _, vs32, _, iqs, ikiw_s = project(hs, cos_s, sin_s, ms)
            iks = ikiw_s[:, :IDX_DIM].reshape(db, t_new, IDX_DIM)
            iws = ikiw_s[:, IDX_DIM:IDX_DIM + IDX_HEADS].reshape(db, t_new, IDX_HEADS)
            iq_rows = jnp.broadcast_to(iqs.reshape(db, t_new, IDX_HEADS, 1, IDX_DIM).transpose(0, 2, 1, 3, 4),
                                       (db, IDX_HEADS, t_new, KV_GROUP, IDX_DIM)).reshape(db, -1, IDX_DIM)
            w_rows = jnp.broadcast_to(iws.transpose(0, 2, 1)[..., None], (db, IDX_HEADS, t_new, KV_GROUP)).reshape(db, -1, 1)
            sc_past, sc_new = _idx_sample(cache_idx_k, i, page_table, iq_rows, w_rows, _pad_page(iks, page),
                                          _pick_tile(n_pages, (16, 8, 4, 2, 1)))
            scores = jnp.concatenate([sc_past, sc_new, jnp.zeros((db, sc_new.shape[1], (pp - 1) * page), F32)], axis=2)
            mask = _select_sample(scores, past_len, t_new)
            ks4 = ks32.reshape(db, t_new, n_kv, HEAD_DIM)
            vs4 = vs32.reshape(db, t_new, n_kv, HEAD_DIM)
            mix_s = _dsa_sample(cache_k, cache_v, l, page_table, _rows_to_heads(qs, db, t_new, n_kv), mask,
                                _pad_page(ks4, page), _pad_page(vs4, page), pp)
            mix_s = _heads_to_rows(mix_s, db, t_new, n_kv)
            outs["iks"].append(iks)
        else:
            w = w_in_sb

            def project(h, cos, sin, tm_):
                tn = 1024
                (q,) = _proj(h, w, i, 0, attn_w, cos, sin, 0, (BF16,), tm_, tn, attn_scale)
                k32, k16 = _proj(h, w, i, attn_w, kv_w, cos, sin, 0, (F32, BF16), tm_, tn)
                v32, v16 = _proj(h, w, i, attn_w + kv_w, kv_w, cos, sin, 0, (F32, BF16), tm_, tn)
                return q, k32, k16, v32, v16

            q, k32, k16, v32, v16 = project(hp, cos_p, sin_p, tm2)
            mix_p = _sb_prompt(q, k16, _chunk_t(v16, nb, s_len, ck), nb, s_len, tq_sb, ck, ng)
            qs, ks32, _, vs32, _ = project(hs, cos_s, sin_s, ms)
            ks4 = ks32.reshape(db, t_new, n_kv, HEAD_DIM)
            vs4 = vs32.reshape(db, t_new, n_kv, HEAD_DIM)
            mix_s = _sb_sample(cache_k, cache_v, l, page_table, _rows_to_heads(qs, db, t_new, n_kv),
                               _pad_page(ks4, page), _pad_page(vs4, page), pp)
            mix_s = _heads_to_rows(mix_s, db, t_new, n_kv)
        outs["kp"].append(k32.reshape(nb, s_len, n_kv, HEAD_DIM))
        outs["vp"].append(v32.reshape(nb, s_len, n_kv, HEAD_DIM))
        outs["ks"].append(ks4)
        outs["vs"].append(vs4)

        xp, hp = _out_norm(mix_p, w_out16, l, xp, mp_l[2], norm_ffn_g[l], mp_l[4], mp_l[3], tm3, s_len)
        xs, hs = _out_norm(mix_s, w_out16, l, xs, ms_l[2], norm_ffn_g[l], ms_l[4], ms_l[3], ms, 1)
        tn_ff = _pick_tile(w_gate.shape[2], (512, 256, 128))
        xp = _gres(_swiglu(hp, w_gate, w_up, l, tm, tn_ff), w_down, l, xp, mp_l[5], tm2, 512, s_len)
        xs = _gres(_swiglu(hs, w_gate, w_up, l, ms, tn_ff), w_down, l, xs, ms_l[5], ms, 512, 1)

    y_prompt = _final_norm(xp, norm_final_g, tm).reshape(nb, s_len, d)
    y_sample = _final_norm(xs, norm_final_g, ms).reshape(db, t_new, d)
    return (y_prompt, y_sample, jnp.stack(outs["kp"]), jnp.stack(outs["vp"]), jnp.stack(outs["ikp"]),
            jnp.stack(outs["ks"]), jnp.stack(outs["vs"]), jnp.stack(outs["iks"]))
```

```python
import functools

import numpy as np
import jax
import jax.numpy as jnp
from jax import lax
from jax.experimental import pallas as pl
from jax.experimental.pallas import tpu as pltpu

F32 = jnp.float32
BF16 = jnp.bfloat16
I32 = jnp.int32

LANES = 128
HEAD_DIM = 128
KV_GROUP = 2
IDX_HEADS = 16
IDX_DIM = 128
TOPK_MAX = 256
ROPE_THETA = 10000.0
EPS = 1e-6
NEG = -1e30
INT_MIN = -2 ** 31
KEY_NEG_INF = -2139095041
LOG2E = 1.4426950408889634
EXP2_UNDERFLOW = -160.0
VMEM_LIMIT = 56 * 1024 * 1024

_NT = (((1,), (1,)), ((), ()))


def _cparams(sem):
    return pltpu.CompilerParams(dimension_semantics=sem, vmem_limit_bytes=VMEM_LIMIT)


def _split_bf16(x):
    hi = x.astype(BF16)
    lo = (x - hi.astype(F32)).astype(BF16)
    return hi, lo


def _sortable(x):
    bits = lax.bitcast_convert_type(x, I32)
    return bits ^ ((bits >> 31) & 0x7FFFFFFF)


def _adaln_kernel(c_ref, w_ref, b_ref, o_ref):
    c = c_ref[...]
    s = (c * jax.nn.sigmoid(c)).astype(BF16)
    o_ref[...] = jnp.dot(s, w_ref[...].astype(BF16), preferred_element_type=F32) + b_ref[...]


def _adaln(c_pad, w_ada, b_ada):
    depth, d, n = w_ada.shape
    r = c_pad.shape[0]
    tn = 1024
    return pl.pallas_call(
        _adaln_kernel,
        grid=(depth, n // tn),
        in_specs=[pl.BlockSpec((r, d), lambda l, j: (0, 0)),
                  pl.BlockSpec((None, d, tn), lambda l, j: (l, 0, j)),
                  pl.BlockSpec((None, 1, tn), lambda l, j: (l, 0, j))],
        out_specs=pl.BlockSpec((None, r, tn), lambda l, j: (l, 0, j)),
        out_shape=jax.ShapeDtypeStruct((depth, r, n), F32),
        compiler_params=_cparams(("arbitrary", "arbitrary")),
        name="adaln",
    )(c_pad, w_ada, b_ada.reshape(depth, 1, n))


def _norm_kernel(x_ref, g_ref, sc_ref, sh_ref, o_ref):
    x = x_ref[...]
    y = x * lax.rsqrt(jnp.mean(x * x, axis=-1, keepdims=True) + EPS) * g_ref[...]
    o_ref[...] = (y * (1.0 + sc_ref[...]) + sh_ref[...]).astype(o_ref.dtype)


def _final_norm_kernel(x_ref, g_ref, o_ref):
    x = x_ref[...]
    o_ref[...] = x * lax.rsqrt(jnp.mean(x * x, axis=-1, keepdims=True) + EPS) * g_ref[...]


def _mod_spec(mod, tm, width, rows_per_batch, col_of):
    if mod.ndim == 3:
        return pl.BlockSpec((None, 1, width), lambda *ij: ((ij[0] if col_of is None else ij[1]) * tm // rows_per_batch, 0,
                                                           0 if col_of is None else ij[0]))
    return pl.BlockSpec((tm, width), lambda *ij: ((ij[0] if col_of is None else ij[1]), 0 if col_of is None else ij[0]))


def _norm(x, g, sc, sh, tm, rows_per_batch):
    m, d = x.shape
    return pl.pallas_call(
        _norm_kernel,
        grid=(m // tm,),
        in_specs=[pl.BlockSpec((tm, d), lambda i: (i, 0)),
                  pl.BlockSpec((1, d), lambda i: (0, 0)),
                  _mod_spec(sc, tm, d, rows_per_batch, None),
                  _mod_spec(sh, tm, d, rows_per_batch, None)],
        out_specs=pl.BlockSpec((tm, d), lambda i: (i, 0)),
        out_shape=jax.ShapeDtypeStruct((m, d), BF16),
        compiler_params=_cparams(("arbitrary",)),
        name="norm_mod",
    )(x, g.reshape(1, d), sc, sh)


def _final_norm(x, g, tm):
    m, d = x.shape
    return pl.pallas_call(
        _final_norm_kernel,
        grid=(m // tm,),
        in_specs=[pl.BlockSpec((tm, d), lambda i: (i, 0)), pl.BlockSpec((1, d), lambda i: (0, 0))],
        out_specs=pl.BlockSpec((tm, d), lambda i: (i, 0)),
        out_shape=jax.ShapeDtypeStruct((m, d), F32),
        compiler_params=_cparams(("arbitrary",)),
        name="final_norm",
    )(x, g.reshape(1, d))


def _proj_kernel(h_ref, w_ref, cos_ref, sin_ref, *rest, rope, plain_tiles, n_out, scale, w_rows):
    outs, wb_ref = rest[:n_out], rest[n_out]

    @pl.when(pl.program_id(1) == 0)
    def _():
        wb_ref[...] = w_ref[...].astype(BF16)

    if w_rows:
        acc = lax.dot_general(h_ref[...], wb_ref[...], _NT, preferred_element_type=F32)
    else:
        acc = jnp.dot(h_ref[...], wb_ref[...], preferred_element_type=F32)
    if scale != 1.0:
        acc = acc * scale
    tn = acc.shape[1]

    def emit(n_rope):
        if not n_rope:
            for o in outs:
                o[...] = acc.astype(o.dtype)
            return
        cos = cos_ref[...]
        sin = sin_ref[...]
        for c in range(tn // HEAD_DIM):
            y = acc[:, c * HEAD_DIM:(c + 1) * HEAD_DIM]
            if c < n_rope:
                y = y * cos + pltpu.roll(y, HEAD_DIM // 2, 1) * sin
            for o in outs:
                o[:, c * HEAD_DIM:(c + 1) * HEAD_DIM] = y.astype(o.dtype)

    if rope and plain_tiles:
        j = pl.program_id(0)
        plain = functools.reduce(jnp.logical_or, [j == t for t in plain_tiles])
        pl.when(plain)(lambda: emit(0))
        pl.when(jnp.logical_not(plain))(lambda: emit(rope))
    else:
        emit(rope)


def _proj(h, w, layer, col0, ncols, cos, sin, rope, out_dtypes, tm, tn, scale=1.0, plain_tiles=(), w_rows=False):
    m, k = h.shape
    npb = cos.shape[0] // tm
    j0 = col0 // tn
    n_out = len(out_dtypes)
    if w_rows:
        w_spec = pl.BlockSpec((None, tn, k), lambda j, i: (layer, j + j0, 0))
    else:
        w_spec = pl.BlockSpec((None, k, tn), lambda j, i: (layer, 0, j + j0))
    return pl.pallas_call(
        functools.partial(_proj_kernel, rope=rope, plain_tiles=tuple(plain_tiles), n_out=n_out, scale=scale,
                          w_rows=w_rows),
        grid=(ncols // tn, m // tm),
        in_specs=[pl.BlockSpec((tm, k), lambda j, i: (i, 0)),
                  w_spec,
                  pl.BlockSpec((tm, HEAD_DIM), lambda j, i: (i % npb, 0)),
                  pl.BlockSpec((tm, HEAD_DIM), lambda j, i: (i % npb, 0))],
        out_specs=[pl.BlockSpec((tm, tn), lambda j, i: (i, j)) for _ in out_dtypes],
        out_shape=[jax.ShapeDtypeStruct((m, ncols), dt) for dt in out_dtypes],
        scratch_shapes=[pltpu.VMEM((tn, k) if w_rows else (k, tn), BF16)],
        compiler_params=_cparams(("arbitrary", "arbitrary")),
        name="proj_rope" if rope else "proj",
    )(h, w, cos, sin)


def _gres_kernel(h_ref, w_ref, x_ref, g_ref, hs_ref, xs_ref, gs_ref, o_ref, os_ref, wb_ref):
    @pl.when(pl.program_id(1) == 0)
    def _():
        wb_ref[...] = w_ref[...].astype(BF16)
        os_ref[...] = xs_ref[...] + gs_ref[...] * jnp.dot(hs_ref[...], wb_ref[...], preferred_element_type=F32)

    acc = jnp.dot(h_ref[...], wb_ref[...], preferred_element_type=F32)
    o_ref[...] = x_ref[...] + g_ref[...] * acc


def _gres(h, w, layer, x, gate, tm, tn, rows_per_batch, h_s, x_s, gate_s):
    m, k = h.shape
    n = w.shape[2]
    m_s = h_s.shape[0]
    return pl.pallas_call(
        _gres_kernel,
        grid=(n // tn, m // tm),
        in_specs=[pl.BlockSpec((tm, k), lambda j, i: (i, 0)),
                  pl.BlockSpec((None, k, tn), lambda j, i: (layer, 0, j)),
                  pl.BlockSpec((tm, tn), lambda j, i: (i, j)),
                  _mod_spec(gate, tm, tn, rows_per_batch, True),
                  pl.BlockSpec((m_s, k), lambda j, i: (0, 0)),
                  pl.BlockSpec((m_s, tn), lambda j, i: (0, j)),
                  pl.BlockSpec((m_s, tn), lambda j, i: (0, j))],
        out_specs=[pl.BlockSpec((tm, tn), lambda j, i: (i, j)), pl.BlockSpec((m_s, tn), lambda j, i: (0, j))],
        out_shape=[jax.ShapeDtypeStruct((m, n), F32), jax.ShapeDtypeStruct((m_s, n), F32)],
        scratch_shapes=[pltpu.VMEM((k, tn), BF16)],
        compiler_params=_cparams(("arbitrary", "arbitrary")),
        name="gated_residual",
    )(h, w, x, gate, h_s, x_s, gate_s)


def _cast_kernel(w_ref, o_ref):
    o_ref[...] = w_ref[...].astype(o_ref.dtype)


def _cast_bf16(w, tk):
    depth, k, n = w.shape
    return pl.pallas_call(
        _cast_kernel,
        grid=(depth, k // tk),
        in_specs=[pl.BlockSpec((None, tk, n), lambda l, i: (l, i, 0))],
        out_specs=pl.BlockSpec((None, tk, n), lambda l, i: (l, i, 0)),
        out_shape=jax.ShapeDtypeStruct(w.shape, BF16),
        compiler_params=_cparams(("arbitrary", "arbitrary")),
        name="cast_bf16",
    )(w)


def _out_norm_kernel(m_ref, w_ref, x_ref, gate_ref, g_ref, sc_ref, sh_ref, xo_ref, ho_ref):
    x = x_ref[...] + gate_ref[...] * jnp.dot(m_ref[...], w_ref[...], preferred_element_type=F32)
    xo_ref[...] = x
    y = x * lax.rsqrt(jnp.mean(x * x, axis=-1, keepdims=True) + EPS) * g_ref[...]
    ho_ref[...] = (y * (1.0 + sc_ref[...]) + sh_ref[...]).astype(ho_ref.dtype)


def _out_norm(mix, wb, layer, x, gate, g, sc, sh, tm, rows_per_batch):
    m, k = mix.shape
    d = wb.shape[2]
    row = lambda i: (i, 0)
    return pl.pallas_call(
        _out_norm_kernel,
        grid=(m // tm,),
        in_specs=[pl.BlockSpec((tm, k), row),
                  pl.BlockSpec((None, k, d), lambda i: (layer, 0, 0)),
                  pl.BlockSpec((tm, d), row),
                  _mod_spec(gate, tm, d, rows_per_batch, None),
                  pl.BlockSpec((1, d), lambda i: (0, 0)),
                  _mod_spec(sc, tm, d, rows_per_batch, None),
                  _mod_spec(sh, tm, d, rows_per_batch, None)],
        out_specs=[pl.BlockSpec((tm, d), row), pl.BlockSpec((tm, d), row)],
        out_shape=[jax.ShapeDtypeStruct((m, d), F32), jax.ShapeDtypeStruct((m, d), BF16)],
        compiler_params=_cparams(("arbitrary",)),
        name="out_proj_norm",
    )(mix, wb, x, gate, g.reshape(1, d), sc, sh)


def _swiglu_kernel(h_ref, wg_ref, wu_ref, hs_ref, o_ref, os_ref, wgb_ref, wub_ref):
    def act(h):
        a = jnp.dot(h, wgb_ref[...], preferred_element_type=F32)
        b = jnp.dot(h, wub_ref[...], preferred_element_type=F32)
        return (a * jax.nn.sigmoid(a) * b).astype(BF16)

    @pl.when(pl.program_id(1) == 0)
    def _():
        wgb_ref[...] = wg_ref[...].astype(BF16)
        wub_ref[...] = wu_ref[...].astype(BF16)
        os_ref[...] = act(hs_ref[...])

    o_ref[...] = act(h_ref[...])


def _swiglu(h, wg, wu, layer, tm, tn, h_s):
    m, k = h.shape
    n = wg.shape[2]
    m_s = h_s.shape[0]
    return pl.pallas_call(
        _swiglu_kernel,
        grid=(n // tn, m // tm),
        in_specs=[pl.BlockSpec((tm, k), lambda j, i: (i, 0)),
                  pl.BlockSpec((None, k, tn), lambda j, i: (layer, 0, j)),
                  pl.BlockSpec((None, k, tn), lambda j, i: (layer, 0, j)),
                  pl.BlockSpec((m_s, k), lambda j, i: (0, 0))],
        out_specs=[pl.BlockSpec((tm, tn), lambda j, i: (i, j)), pl.BlockSpec((m_s, tn), lambda j, i: (0, j))],
        out_shape=[jax.ShapeDtypeStruct((m, n), BF16), jax.ShapeDtypeStruct((m_s, n), BF16)],
        scratch_shapes=[pltpu.VMEM((k, tn), BF16), pltpu.VMEM((k, tn), BF16)],
        compiler_params=_cparams(("arbitrary", "arbitrary")),
        name="swiglu",
    )(h, wg, wu, h_s)


def _kth_threshold(count_ge, topk, shape):
    t0 = jnp.where(count_ge(jnp.zeros(shape, I32)) >= topk, 0, INT_MIN).astype(I32)

    def body(it, t):
        cand = t | jnp.left_shift(jnp.int32(1), 30 - it)
        return jnp.where(count_ge(cand) >= topk, cand, t)

    return lax.fori_loop(0, 31, body, t0)


def _tie_cutoff(count_tie_below, need, nbits, shape):
    def body(it, a):
        cand = a | jnp.left_shift(jnp.int32(1), nbits - 1 - it)
        return jnp.where(count_tie_below(cand) < need, cand, a)

    return lax.fori_loop(0, nbits, body, jnp.zeros(shape, I32))


def _dsa_prompt_kernel(q_ref, k_ref, vt_ref, iq_ref, ik_ref, iwt_ref, o_ref,
                       iqcat_ref, keys_ref, bias_ref, cut_ref, m_ref, l_ref, acc_ref,
                       *, tq, ck, ng, topk, nbits, idx_scale):
    i = pl.program_id(1)
    nkc = ((i + 1) * tq + ck - 1) // ck
    qpos = i * tq + lax.broadcasted_iota(I32, (1, tq), 1)

    def kpos_of(c):
        return c * ck + lax.broadcasted_iota(I32, (ck, tq), 0)

    @pl.when(pl.program_id(2) == 0)
    def _index():
        for h in range(IDX_HEADS):
            hi, lo = _split_bf16(iq_ref[:, h * IDX_DIM:(h + 1) * IDX_DIM])
            iqcat_ref[:, h * 2 * IDX_DIM:h * 2 * IDX_DIM + IDX_DIM] = hi
            iqcat_ref[:, h * 2 * IDX_DIM + IDX_DIM:(h + 1) * 2 * IDX_DIM] = lo

        def score_chunk(c, carry):
            ikh = ik_ref[pl.ds(pl.multiple_of(c * ck, ck), ck), :].astype(BF16)
            ikcat = jnp.concatenate([ikh, ikh], axis=1)
            acc = jnp.zeros((ck, tq), F32)
            for h in range(IDX_HEADS):
                lg = lax.dot_general(ikcat, iqcat_ref[:, h * 2 * IDX_DIM:(h + 1) * 2 * IDX_DIM], _NT,
                                     preferred_element_type=F32)
                acc = acc + iwt_ref[h:h + 1, :] * jnp.maximum(lg, 0.0)
            sc = jnp.where(kpos_of(c) <= qpos, acc * idx_scale, -jnp.inf)
            keys_ref[c] = _sortable(sc)
            return carry

        lax.fori_loop(0, nkc, score_chunk, 0)

        def count(pred):
            def body(c, part):
                m = jnp.where(pred(keys_ref[c], c), 1.0, 0.0)
                return part + jnp.sum(m.reshape(ck // 32, 32, tq), axis=0)
            part = lax.fori_loop(0, nkc, body, jnp.zeros((32, tq), F32))
            return jnp.sum(part, axis=0, keepdims=True)

        thr = _kth_threshold(lambda cand: count(lambda kc, c: kc >= cand), float(topk), (1, tq))
        n_gt = count(lambda kc, c: kc > thr)
        n_ge = count(lambda kc, c: kc >= thr)
        need = float(topk) - n_gt
        crowded = (n_ge > float(topk)) & (thr > KEY_NEG_INF)
        cut_ref[...] = jnp.full((1, tq), 2 ** 30, I32)

        @pl.when(jnp.max(jnp.where(crowded, 1.0, 0.0)) > 0.5)
        def _():
            cut_ref[...] = _tie_cutoff(lambda cand: count(lambda kc, c: (kc == thr) & (kpos_of(c) < cand)),
                                       need, nbits, (1, tq))

        cut = cut_ref[...]

        def bias_chunk(c, carry):
            kc = keys_ref[c]
            kpos = kpos_of(c)
            sel = ((kc > thr) | ((kc == thr) & (kpos <= cut))) & (kpos <= qpos)
            bias_ref[c] = jnp.where(sel, 0.0, NEG)
            return carry

        lax.fori_loop(0, nkc, bias_chunk, 0)

    m_ref[...] = jnp.full(m_ref.shape, NEG, F32)
    l_ref[...] = jnp.zeros(l_ref.shape, F32)
    acc_ref[...] = jnp.zeros(acc_ref.shape, F32)

    nh = ng * KV_GROUP

    def attend(c, carry):
        off = pl.multiple_of(c * ck, ck)
        b = bias_ref[c]

        def scores(h):
            g = h // KV_GROUP
            return lax.dot_general(k_ref[pl.ds(off, ck), g * HEAD_DIM:(g + 1) * HEAD_DIM],
                                   q_ref[:, h * HEAD_DIM:(h + 1) * HEAD_DIM], _NT,
                                   preferred_element_type=F32) + b

        def softmax(h, s):
            m = m_ref[h]
            m_new = jnp.maximum(m, jnp.max(s, axis=0, keepdims=True))
            alpha = jnp.exp2(m - m_new)
            p = jnp.exp2(s - m_new)
            l_ref[h] = alpha * l_ref[h] + jnp.sum(p, axis=0, keepdims=True)
            m_ref[h] = m_new
            return p.astype(BF16), alpha

        def weigh(h, p, alpha):
            g = h // KV_GROUP
            acc_ref[h] = alpha * acc_ref[h] + jnp.dot(vt_ref[c, g * HEAD_DIM:(g + 1) * HEAD_DIM, :], p,
                                                      preferred_element_type=F32)

        ahead = 4
        s = {h: scores(h) for h in range(min(ahead, nh))}
        for h in range(nh):
            p, alpha = softmax(h, s.pop(h))
            if h + ahead < nh:
                s[h + ahead] = scores(h + ahead)
            weigh(h, p, alpha)
        return carry

    lax.fori_loop(0, nkc, attend, 0)
    for h in range(nh):
        out = acc_ref[h] * (1.0 / l_ref[h])
        o_ref[:, h * HEAD_DIM:(h + 1) * HEAD_DIM] = out.T.astype(o_ref.dtype)


def _dsa_prompt(q, k, vt, iq, ikiw, iwt, nb, s_len, tq, ck, ng):
    n_kv = k.shape[1] // HEAD_DIM
    nq = s_len // tq
    topk = min(TOPK_MAX, s_len // 4)
    nh = ng * KV_GROUP
    kern = functools.partial(_dsa_prompt_kernel, tq=tq, ck=ck, ng=ng, topk=topk, nbits=max(1, (s_len - 1).bit_length()),
                             idx_scale=(IDX_HEADS ** -0.5) * (IDX_DIM ** -0.5))
    return pl.pallas_call(
        kern,
        grid=(nb, nq, n_kv // ng),
        in_specs=[pl.BlockSpec((tq, nh * HEAD_DIM), lambda b, i, g: (b * nq + i, g)),
                  pl.BlockSpec((s_len, ng * HEAD_DIM), lambda b, i, g: (b, g)),
                  pl.BlockSpec((None, s_len // ck, ng * HEAD_DIM, ck), lambda b, i, g: (b, 0, g, 0)),
                  pl.BlockSpec((tq, IDX_HEADS * IDX_DIM), lambda b, i, g: (b * nq + i, 0)),
                  pl.BlockSpec((s_len, IDX_DIM), lambda b, i, g: (b, 0)),
                  pl.BlockSpec((IDX_HEADS, tq), lambda b, i, g: (0, b * nq + i))],
        out_specs=pl.BlockSpec((tq, nh * HEAD_DIM), lambda b, i, g: (b * nq + i, g)),
        out_shape=jax.ShapeDtypeStruct(q.shape, BF16),
        scratch_shapes=[pltpu.VMEM((tq, IDX_HEADS * 2 * IDX_DIM), BF16),
                        pltpu.VMEM((s_len // ck, ck, tq), I32),
                        pltpu.VMEM((s_len // ck, ck, tq), F32),
                        pltpu.VMEM((1, tq), I32),
                        pltpu.VMEM((nh, 1, tq), F32),
                        pltpu.VMEM((nh, 1, tq), F32),
                        pltpu.VMEM((nh, HEAD_DIM, tq), F32)],
        compiler_params=_cparams(("arbitrary", "arbitrary", "arbitrary")),
        name="dsa_prompt",
    )(q, k, vt, iq, ikiw, iwt)


def _softplus2(z):
    return jnp.maximum(z, 0.0) + jnp.log2(1.0 + jnp.exp2(-jnp.abs(z)))


def _sb_prompt_kernel(q_ref, k_ref, vt_ref, ln_ref, o_ref, tail_ref, acc_ref, *, tq, ck, ng):
    i = pl.program_id(1)
    nkc = ((i + 1) * tq + ck - 1) // ck
    nh = ng * KV_GROUP
    qpos = i * tq + lax.broadcasted_iota(I32, (1, tq), 1)
    tail_ref[...] = jnp.zeros(tail_ref.shape, F32)
    acc_ref[...] = jnp.zeros(acc_ref.shape, F32)

    def chunk(c, masked):
        off = pl.multiple_of(c * ck, ck)
        mask = (c * ck + lax.broadcasted_iota(I32, (ck, tq), 0)) < qpos if masked else None

        def logits(h):
            g = h // KV_GROUP
            return lax.dot_general(k_ref[pl.ds(off, ck), g * HEAD_DIM:(g + 1) * HEAD_DIM],
                                   q_ref[:, h * HEAD_DIM:(h + 1) * HEAD_DIM], _NT, preferred_element_type=F32)

        def keep(h, z):
            sp = jnp.where(z > 64.0, z, jnp.log2(1.0 + jnp.exp2(z)))
            spm = jnp.where(mask, sp, 0.0) if masked else sp
            la = jnp.dot(ln_ref[...], spm.astype(BF16), preferred_element_type=F32)
            tail = tail_ref[h]
            tail_ref[h] = tail + la[ck:ck + 1]
            return z - sp + tail, la

        def weigh(h, base, la):
            g = h // KV_GROUP
            a = jnp.exp2(base + la[:ck])
            if masked:
                a = jnp.where(mask, a, 0.0)
            acc_ref[h] += jnp.dot(vt_ref[c, g * HEAD_DIM:(g + 1) * HEAD_DIM, :], a.astype(BF16),
                                  preferred_element_type=F32)

        a1, a2 = 2, 4
        z = {h: logits(h) for h in range(min(a2, nh))}
        kept = {h: keep(h, z.pop(h)) for h in range(min(a1, nh))}
        for h in range(nh):
            if h + a1 < nh:
                kept[h + a1] = keep(h + a1, z.pop(h + a1))
            if h + a2 < nh:
                z[h + a2] = logits(h + a2)
            weigh(h, *kept.pop(h))

    chunk(nkc - 1, True)

    def live(carry):
        r, top = carry
        return (r < nkc) & (top > EXP2_UNDERFLOW)

    def body(carry):
        r, _ = carry
        chunk(nkc - 1 - r, False)
        return r + 1, jnp.max(tail_ref[...])

    lax.while_loop(live, body, (jnp.int32(1), jnp.max(tail_ref[...])))
    for h in range(nh):
        o_ref[:, h * HEAD_DIM:(h + 1) * HEAD_DIM] = acc_ref[h].T.astype(o_ref.dtype)


def _suffix_matrix(n):
    return jnp.asarray(np.tril(np.ones((n, n), np.float32), -1), dtype=BF16)


def _sb_prompt(q, k, vt, nb, s_len, tq, ck, ng):
    assert ck % tq == 0
    n_kv = k.shape[1] // HEAD_DIM
    nq = s_len // tq
    gw = KV_GROUP * HEAD_DIM
    ln = jnp.concatenate([-_suffix_matrix(ck).T, -jnp.ones((8, ck), BF16)], axis=0)
    return pl.pallas_call(
        functools.partial(_sb_prompt_kernel, tq=tq, ck=ck, ng=ng),
        grid=(nb, nq, n_kv // ng),
        in_specs=[pl.BlockSpec((tq, ng * gw), lambda b, i, g: (b * nq + i, g)),
                  pl.BlockSpec((s_len, ng * HEAD_DIM), lambda b, i, g: (b, g)),
                  pl.BlockSpec((None, s_len // ck, ng * HEAD_DIM, ck), lambda b, i, g: (b, 0, g, 0)),
                  pl.BlockSpec((ck + 8, ck), lambda b, i, g: (0, 0))],
        out_specs=pl.BlockSpec((tq, ng * gw), lambda b, i, g: (b * nq + i, g)),
        out_shape=jax.ShapeDtypeStruct(q.shape, BF16),
        scratch_shapes=[pltpu.VMEM((ng * KV_GROUP, 1, tq), F32),
                        pltpu.VMEM((ng * KV_GROUP, HEAD_DIM, tq), F32)],
        compiler_params=_cparams(("arbitrary", "arbitrary", "arbitrary")),
        name="sb_prompt",
    )(q, k, vt, ln)


def _page_consts(page, n_kv, rows):
    lane = np.arange(page * n_kv)
    expand = (lane[None, :] // n_kv == np.arange(page)[:, None]).astype(np.float32)
    valid = (lane[None, :] % n_kv == (np.arange(rows)[:, None] // (rows // n_kv))).astype(np.float32)
    return jnp.asarray(expand, BF16), jnp.asarray(expand.T.copy(), BF16), jnp.asarray(valid, F32)


def _idx_sample_kernel(pt_ref, iq_ref, w_ref, ikn_ref, *rest, pp, idx_scale):
    pages, (past_ref, new_ref) = rest[:pp], rest[pp:]
    hi, lo = _split_bf16(iq_ref[...])
    iqcat = jnp.concatenate([hi, hi, lo], axis=1)
    w = w_ref[...]
    rows = iqcat.shape[0]

    def logits(ik):
        ikh, ikl = _split_bf16(ik)
        return lax.dot_general(iqcat, jnp.concatenate([ikh, ikl, ikh], axis=1), _NT, preferred_element_type=F32)

    def score(lg):
        x = jnp.maximum(lg, 0.0) * w
        return jnp.sum(x.reshape(IDX_HEADS, rows // IDX_HEADS, x.shape[1]), axis=0) * idx_scale

    lgs = [logits(pages[j][...]) for j in range(pp)]
    for j in range(pp):
        past_ref[:, j * LANES:(j + 1) * LANES] = score(lgs[j])

    @pl.when(pl.program_id(1) == 0)
    def _():
        new_ref[...] = score(logits(ikn_ref[...]))


def _idx_sample(cache_idx_k, layer, page_table, iq_rows, w_rows, ik_new_pad, pp):
    db, n_pages = page_table.shape
    page = cache_idx_k.shape[2]
    rows = iq_rows.shape[1]
    r8 = rows // IDX_HEADS
    page_specs = [pl.BlockSpec((None, None, page, IDX_DIM),
                               functools.partial(lambda b, p, pt, j: (layer, pt[b, p * pp + j], 0, 0), j=j))
                  for j in range(pp)]
    return pl.pallas_call(
        functools.partial(_idx_sample_kernel, pp=pp, idx_scale=(IDX_HEADS ** -0.5) * (IDX_DIM ** -0.5)),
        grid_spec=pltpu.PrefetchScalarGridSpec(
            num_scalar_prefetch=1,
            grid=(db, n_pages // pp),
            in_specs=[pl.BlockSpec((None, rows, IDX_DIM), lambda b, p, pt: (b, 0, 0)),
                      pl.BlockSpec((None, rows, 1), lambda b, p, pt: (b, 0, 0)),
                      pl.BlockSpec((None, page, IDX_DIM), lambda b, p, pt: (b, 0, 0))] + page_specs,
            out_specs=[pl.BlockSpec((None, r8, pp * page), lambda b, p, pt: (b, 0, p)),
                       pl.BlockSpec((None, r8, page), lambda b, p, pt: (b, 0, 0))]),
        out_shape=[jax.ShapeDtypeStruct((db, r8, n_pages * page), F32),
                   jax.ShapeDtypeStruct((db, r8, page), F32)],
        compiler_params=_cparams(("arbitrary", "arbitrary")),
        name="idx_sample",
    )(page_table, iq_rows, w_rows, ik_new_pad, *([cache_idx_k] * pp))


def _select_sample_kernel(s_ref, o_ref, *, topk, nbits, past_len):
    sc = s_ref[...]
    rows, length = sc.shape
    qpos = past_len + lax.broadcasted_iota(I32, (rows, 1), 0) // KV_GROUP
    kpos = lax.broadcasted_iota(I32, (rows, length), 1)
    causal = kpos <= qpos
    key = _sortable(jnp.where(causal, sc, -jnp.inf))

    def count(pred):
        return jnp.sum(jnp.where(pred, 1.0, 0.0), axis=1, keepdims=True)

    thr = _kth_threshold(lambda cand: count(key >= cand), float(topk), (rows, 1))
    need = float(topk) - count(key > thr)
    cut = _tie_cutoff(lambda cand: count((key == thr) & (kpos < cand)), need, nbits, (rows, 1))
    sel = ((key > thr) | ((key == thr) & (kpos <= cut))) & causal
    o_ref[...] = jnp.where(sel, 1.0, 0.0)


def _select_sample(scores, past_len, n_new):
    db, rows, length = scores.shape
    topk = min(TOPK_MAX, (past_len + n_new) // 4)
    return pl.pallas_call(
        functools.partial(_select_sample_kernel, topk=topk, nbits=max(1, (length - 1).bit_length()), past_len=past_len),
        grid=(db,),
        in_specs=[pl.BlockSpec((None, rows, length), lambda b: (b, 0, 0))],
        out_specs=pl.BlockSpec((None, rows, length), lambda b: (b, 0, 0)),
        out_shape=jax.ShapeDtypeStruct(scores.shape, F32),
        compiler_params=_cparams(("arbitrary",)),
        name="select_sample",
    )(scores)


def _flat_bf16(ref):
    x = ref[...]
    return x.reshape(x.shape[0] * x.shape[1], x.shape[2]).astype(BF16)


def _dsa_sample_kernel(pt_ref, q_ref, mask_ref, e_ref, valid_ref, kn_ref, vn_ref, *rest,
                       pp, n_steps):
    kpages, vpages = rest[:pp], rest[pp:2 * pp]
    o_ref, m_ref, l_ref, acc_ref = rest[2 * pp:]
    p = pl.program_id(1)
    q = q_ref[...]
    rows = q.shape[0]
    reps = rows // mask_ref.shape[0]
    valid = valid_ref[...]

    @pl.when(p == 0)
    def _():
        m_ref[...] = jnp.full(m_ref.shape, NEG, F32)
        l_ref[...] = jnp.zeros(l_ref.shape, F32)
        acc_ref[...] = jnp.zeros(acc_ref.shape, F32)

    def pages(krefs, vrefs):
        ss = []
        for j, kref in enumerate(krefs):
            s = lax.dot_general(q, _flat_bf16(kref), _NT, preferred_element_type=F32)
            mrows = jnp.concatenate([mask_ref[:, j * LANES:(j + 1) * LANES]] * reps, axis=0).astype(BF16)
            keep = jnp.dot(mrows, e_ref[...], preferred_element_type=F32) * valid > 0.5
            ss.append(jnp.where(keep, s, NEG))
        m = m_ref[...]
        m_new = m
        for s in ss:
            m_new = jnp.maximum(m_new, jnp.max(s, axis=1, keepdims=True))
        alpha = jnp.exp2(m - m_new)
        l = alpha * l_ref[...]
        acc = alpha * acc_ref[...]
        for s, vref in zip(ss, vrefs):
            pr = jnp.exp2(s - m_new)
            l = l + jnp.sum(pr, axis=1, keepdims=True)
            acc = acc + jnp.dot(pr.astype(BF16), _flat_bf16(vref), preferred_element_type=F32)
        l_ref[...] = l
        acc_ref[...] = acc
        m_ref[...] = m_new

    @pl.when(p < n_steps - 1)
    def _():
        pages(kpages, vpages)

    @pl.when(p == n_steps - 1)
    def _():
        pages([kn_ref], [vn_ref])
        o_ref[...] = (acc_ref[...] / l_ref[...]).astype(o_ref.dtype)


def _dsa_sample(cache_k, cache_v, layer, page_table, q_rows, mask, k_new_pad, v_new_pad, pp):
    db, n_pages = page_table.shape
    page, n_kv = cache_k.shape[2], cache_k.shape[3]
    rows = q_rows.shape[1]
    n_steps = n_pages // pp + 1
    expand, _, valid = _page_consts(page, n_kv, rows)

    def cache_spec(j):
        return pl.BlockSpec((None, None, page, n_kv, HEAD_DIM),
                            lambda b, p, pt: (layer, pt[b, jnp.minimum(p, n_steps - 2) * pp + j], 0, 0, 0))

    new_spec = pl.BlockSpec((None, page, n_kv, HEAD_DIM), lambda b, p, pt: (b, 0, 0, 0))
    mask_spec = pl.BlockSpec((None, mask.shape[1], pp * page),
                             lambda b, p, pt: (b, 0, jnp.where(p == n_steps - 1, n_pages // pp, p)))
    return pl.pallas_call(
        functools.partial(_dsa_sample_kernel, pp=pp, n_steps=n_steps),
        grid_spec=pltpu.PrefetchScalarGridSpec(
            num_scalar_prefetch=1,
            grid=(db, n_steps),
            in_specs=[pl.BlockSpec((None, rows, HEAD_DIM), lambda b, p, pt: (b, 0, 0)),
                      mask_spec,
                      pl.BlockSpec(expand.shape, lambda b, p, pt: (0, 0)),
                      pl.BlockSpec(valid.shape, lambda b, p, pt: (0, 0)),
                      new_spec, new_spec] + [cache_spec(j) for j in range(pp)] * 2,
            out_specs=pl.BlockSpec((None, rows, HEAD_DIM), lambda b, p, pt: (b, 0, 0)),
            scratch_shapes=[pltpu.VMEM((rows, 1), F32), pltpu.VMEM((rows, 1), F32), pltpu.VMEM((rows, HEAD_DIM), F32)]),
        out_shape=jax.ShapeDtypeStruct((db, rows, HEAD_DIM), BF16),
        compiler_params=_cparams(("arbitrary", "arbitrary")),
        name="dsa_sample",
    )(page_table, q_rows, mask, expand, valid, k_new_pad, v_new_pad, *([cache_k] * pp), *([cache_v] * pp))


def _sb_sample_kernel(pt_ref, q_ref, e_ref, c_ref, valid_ref, u_ref, kn_ref, vn_ref, tail_in_ref, acc_in_ref, *rest,
                      pp, top, with_new, past_len):
    kpages, vpages = rest[:pp], rest[pp:2 * pp]
    tail_ref, acc_ref = rest[2 * pp:]
    p = pl.program_id(1)
    q = q_ref[...]
    rows = q.shape[0]
    n_kv = valid_ref.shape[1] // e_ref.shape[0]
    valid = valid_ref[...]
    page_len = e_ref.shape[0]
    qpos = past_len + (lax.broadcasted_iota(I32, (rows, 1), 0) % (rows // n_kv)) // KV_GROUP

    @pl.when(p == 0)
    def _():
        tail_ref[...] = tail_in_ref[...]
        acc_ref[...] = acc_in_ref[...]

    def pages(krefs, vrefs, starts):
        zfs = [lax.dot_general(q, _flat_bf16(kref), _NT, preferred_element_type=F32) * valid for kref in krefs]
        zs = []
        for zf in zfs:
            hi, lo = _split_bf16(zf)
            zz = jnp.dot(jnp.concatenate([hi, lo], axis=0), c_ref[...], preferred_element_type=F32)
            zs.append(zz[:rows] + zz[rows:])
        parts = []
        tail = tail_ref[...]
        for z, start in zip(zs, starts):
            mask = (start + lax.broadcasted_iota(I32, (rows, page_len), 1)) < qpos
            sp = _softplus2(z)
            lk = jnp.where(mask, -sp, 0.0)
            hi, lo = _split_bf16(lk)
            ll = jnp.dot(jnp.concatenate([hi, lo], axis=0), u_ref[...], preferred_element_type=F32)
            parts.append((mask, z - sp + tail, ll))
            tail = tail + jnp.sum(lk, axis=1, keepdims=True)
        tail_ref[...] = tail
        acc = acc_ref[...]
        for (mask, base, ll), vref in zip(parts, vrefs):
            a = jnp.where(mask, jnp.exp2(base + ll[:rows] + ll[rows:]), 0.0).astype(BF16)
            ae = (jnp.dot(a, e_ref[...], preferred_element_type=F32) * valid).astype(BF16)
            acc = acc + jnp.dot(ae, _flat_bf16(vref), preferred_element_type=F32)
        acc_ref[...] = acc

    if with_new:
        @pl.when(p == 0)
        def _():
            pages([kn_ref], [vn_ref], [past_len])

    @pl.when(p >= int(with_new))
    def _():
        pages(kpages, vpages, [(top - 1 - ((p - int(with_new)) * pp + j)) * page_len for j in range(pp)])


def _sb_sample_part(cache_k, cache_v, layer, page_table, q_rows, k_new_pad, v_new_pad, tail, acc, pp, top, count, with_new):
    db, n_pages = page_table.shape
    page, n_kv = cache_k.shape[2], cache_k.shape[3]
    rows = q_rows.shape[1]
    first = int(with_new)
    expand, compact, valid = _page_consts(page, n_kv, rows)

    def cache_spec(j):
        return pl.BlockSpec((None, None, page, n_kv, HEAD_DIM),
                            lambda b, p, pt: (layer, pt[b, top - 1 - (jnp.maximum(p - first, 0) * pp + j)], 0, 0, 0))

    new_spec = pl.BlockSpec((None, page, n_kv, HEAD_DIM), lambda b, p, pt: (b, 0, 0, 0))
    const = lambda a: pl.BlockSpec(a.shape, lambda b, p, pt: (0, 0))
    per_b = lambda a: pl.BlockSpec((None,) + a.shape[1:], lambda b, p, pt: (b, 0, 0))
    u = _suffix_matrix(page)
    return pl.pallas_call(
        functools.partial(_sb_sample_kernel, pp=pp, top=top, with_new=with_new, past_len=n_pages * page),
        grid_spec=pltpu.PrefetchScalarGridSpec(
            num_scalar_prefetch=1,
            grid=(db, count // pp + first),
            in_specs=[per_b(q_rows), const(expand), const(compact), const(valid), const(u),
                      new_spec, new_spec, per_b(tail), per_b(acc)] + [cache_spec(j) for j in range(pp)] * 2,
            out_specs=[per_b(tail), per_b(acc)]),
        out_shape=[jax.ShapeDtypeStruct(tail.shape, F32), jax.ShapeDtypeStruct(acc.shape, F32)],
        compiler_params=_cparams(("arbitrary", "arbitrary")),
        name="sb_sample",
    )(page_table, q_rows, expand, compact, valid, u, k_new_pad, v_new_pad, tail, acc,
      *([cache_k] * pp), *([cache_v] * pp))


def _sb_sample(cache_k, cache_v, layer, page_table, q_rows, k_new_pad, v_new_pad, pp):
    db, n_pages = page_table.shape
    rows = q_rows.shape[1]
    args = (cache_k, cache_v, layer, page_table, q_rows, k_new_pad, v_new_pad)
    state = (jnp.zeros((db, rows, 1), F32), jnp.zeros((db, rows, HEAD_DIM), F32))
    state = tuple(_sb_sample_part(*args, *state, pp, n_pages, pp, True))
    rest = n_pages - pp
    if rest > 0:
        state = lax.cond(jnp.max(state[0]) > EXP2_UNDERFLOW,
                         lambda s: tuple(_sb_sample_part(*args, *s, pp, rest, rest, False)), lambda s: s, state)
    return state[1].astype(BF16)


def _rope_tables(pos):
    half = HEAD_DIM // 2
    inv = np.float32(ROPE_THETA) ** (-np.arange(half, dtype=np.float32) / np.float32(half))
    ang = pos.astype(np.float32)[:, None] * inv[None, :].astype(np.float32)
    cos, sin = np.cos(ang), np.sin(ang)
    return jnp.asarray(np.concatenate([cos, cos], axis=1)), jnp.asarray(np.concatenate([-sin, sin], axis=1))


def _pick_tile(n, prefs):
    for t in prefs:
        if n % t == 0:
            return t
    return n


def _rows_to_heads(x, db, t, n_kv):
    return x.reshape(db, t, n_kv, KV_GROUP, HEAD_DIM).transpose(0, 2, 1, 3, 4).reshape(db, n_kv * t * KV_GROUP, HEAD_DIM)


def _heads_to_rows(x, db, t, n_kv):
    return x.reshape(db, n_kv, t, KV_GROUP, HEAD_DIM).transpose(0, 2, 1, 3, 4).reshape(db * t, n_kv * KV_GROUP * HEAD_DIM)


def _chunk_t(v, nb, s_len, ck):
    return v.reshape(nb, s_len // ck, ck, v.shape[1]).transpose(0, 1, 3, 2)


def _pad_page(x, page):
    return jnp.pad(x, [(0, 0), (0, page - x.shape[1])] + [(0, 0)] * (x.ndim - 2))


def kernel(x_prompt, x_sample, c_prompt, c_sample, cache_k, cache_v, cache_idx_k, page_table, norm_mix_g, norm_ffn_g,
           w_ada, b_ada, w_in_dsa, w_in_sb, w_out, w_gate, w_up, w_down, norm_final_g):
    nb, s_len, d = x_prompt.shape
    db, t_new, _ = x_sample.shape
    depth = w_ada.shape[0]
    page, n_kv = cache_k.shape[2], cache_k.shape[3]
    n_pages = page_table.shape[1]
    past_len = n_pages * page
    attn_w = w_out.shape[1]
    kv_w = n_kv * HEAD_DIM
    idx_w = IDX_HEADS * IDX_DIM
    mp, ms = nb * s_len, db * t_new

    tm = _pick_tile(s_len, (1024, 512, 256, 128))
    tm2 = _pick_tile(s_len, (512, 256, 128))
    tm3 = _pick_tile(s_len, (256, 128))
    w_out16 = _cast_bf16(w_out, _pick_tile(w_out.shape[1], (512, 256, 128)))
    tq_dsa, tq_sb, ck = 256, 256, 256
    ng = _pick_tile(n_kv, (4, 2, 1))
    pp = _pick_tile(n_pages, (8, 4, 2, 1))
    attn_scale = HEAD_DIM ** -0.5 * LOG2E

    n_c = nb + db
    c_all = jnp.pad(jnp.concatenate([c_prompt, c_sample], axis=0), ((0, (-n_c) % 16), (0, 0)))
    mod = _adaln(c_all, w_ada, b_ada).reshape(depth, c_all.shape[0], 6, d)

    cos_p, sin_p = _rope_tables(np.arange(s_len))
    cos_s, sin_s = _rope_tables(np.tile(past_len + np.arange(t_new), db))

    xp = x_prompt.reshape(mp, d)
    xs = x_sample.reshape(ms, d)
    outs = {n: [] for n in ("kp", "vp", "ikp", "ks", "vs", "iks")}

    for l in range(depth):
        mp_l = [mod[l, :nb, j].reshape(nb, 1, d) for j in range(6)]
        ms_l = [jnp.repeat(mod[l, nb:n_c, j], t_new, axis=0) for j in range(6)]
        hp = _norm(xp, norm_mix_g[l], mp_l[1], mp_l[0], tm, s_len)
        hs = _norm(xs, norm_mix_g[l], ms_l[1], ms_l[0], ms, 1)
        i = l // 2
        if l % 2 == 0:
            w = jnp.swapaxes(w_in_dsa, 1, 2)
            tail0 = attn_w + 2 * kv_w + idx_w
            w_tail = jnp.pad(w[i, tail0:, :], ((0, 2 * LANES - IDX_DIM - IDX_HEADS), (0, 0)))[None]
            tn = 1024
            hd = tn // HEAD_DIM
            (q,) = _proj(hp, w, i, 0, attn_w, cos_p, sin_p, hd, (BF16,), tm2, tn, attn_scale, w_rows=True)
            k32, k16 = _proj(hp, w, i, attn_w, kv_w, cos_p, sin_p, hd, (F32, BF16), tm2, tn, w_rows=True)
            v32, v16 = _proj(hp, w, i, attn_w + kv_w, kv_w, cos_p, sin_p, 0, (F32, BF16), tm2, tn, w_rows=True)
            (iq,) = _proj(hp, w, i, attn_w + 2 * kv_w, idx_w, cos_p, sin_p, hd, (F32,), tm2, tn, w_rows=True)
            (ikiw,) = _proj(hp, w_tail, 0, 0, 2 * LANES, cos_p, sin_p, 1, (F32,), tm2, 2 * LANES, w_rows=True)
            mix_p = _dsa_prompt(q, k16, _chunk_t(v16, nb, s_len, ck), iq, ikiw, ikiw[:, IDX_DIM:IDX_DIM + IDX_HEADS].T,
                                nb, s_len, tq_dsa, ck, ng)
            outs["ikp"].append(ikiw[:, :IDX_DIM].reshape(nb, s_len, IDX_DIM))

            (all_s,) = _proj(hs, w, i, 0, tail0, cos_s, sin_s, hd, (F32,), ms, tn, plain_tiles=((attn_w + kv_w) // tn,),
                             w_rows=True)
            (ikiw_s,) = _proj(hs, w_tail, 0, 0, 2 * LANES, cos_s, sin_s, 1, (F32,), ms, 2 * LANES, w_rows=True)
            qs = (all_s[:, :attn_w] * attn_scale).astype(BF16)
            ks32 = all_s[:, attn_w:attn_w + kv_w]
            vs32 = all_s[:, attn_w + kv_w:attn_w + 2 * kv_w]
            iqs = all_s[:, attn_w + 2 * kv_w:]
            iks = ikiw_s[:, :IDX_DIM].reshape(db, t_new, IDX_DIM)
            iws = ikiw_s[:, IDX_DIM:IDX_DIM + IDX_HEADS].reshape(db, t_new, IDX_HEADS)
            iq_rows = jnp.broadcast_to(iqs.reshape(db, t_new, IDX_HEADS, 1, IDX_DIM).transpose(0, 2, 1, 3, 4),
                                       (db, IDX_HEADS, t_new, KV_GROUP, IDX_DIM)).reshape(db, -1, IDX_DIM)
            w_rows = jnp.broadcast_to(iws.transpose(0, 2, 1)[..., None], (db, IDX_HEADS, t_new, KV_GROUP)).reshape(db, -1, 1)
            sc_past, sc_new = _idx_sample(cache_idx_k, i, page_table, iq_rows, w_rows, _pad_page(iks, page),
                                          _pick_tile(n_pages, (16, 8, 4, 2, 1)))
            scores = jnp.concatenate([sc_past, sc_new, jnp.zeros((db, sc_new.shape[1], (pp - 1) * page), F32)], axis=2)
            mask = _select_sample(scores, past_len, t_new)
            ks4 = ks32.reshape(db, t_new, n_kv, HEAD_DIM)
            vs4 = vs32.reshape(db, t_new, n_kv, HEAD_DIM)
            mix_s = _dsa_sample(cache_k, cache_v, l, page_table, _rows_to_heads(qs, db, t_new, n_kv), mask,
                                _pad_page(ks4, page), _pad_page(vs4, page), pp)
            mix_s = _heads_to_rows(mix_s, db, t_new, n_kv)
            outs["iks"].append(iks)
        else:
            w = w_in_sb
            tn = 1024
            (q,) = _proj(hp, w, i, 0, attn_w, cos_p, sin_p, 0, (BF16,), tm2, tn, attn_scale)
            k32, k16 = _proj(hp, w, i, attn_w, kv_w, cos_p, sin_p, 0, (F32, BF16), tm2, tn)
            v32, v16 = _proj(hp, w, i, attn_w + kv_w, kv_w, cos_p, sin_p, 0, (F32, BF16), tm2, tn)
            mix_p = _sb_prompt(q, k16, _chunk_t(v16, nb, s_len, ck), nb, s_len, tq_sb, ck, ng)
            (all_s,) = _proj(hs, w, i, 0, attn_w + 2 * kv_w, cos_s, sin_s, 0, (F32,), ms, tn)
            qs = (all_s[:, :attn_w] * attn_scale).astype(BF16)
            ks32 = all_s[:, attn_w:attn_w + kv_w]
            vs32 = all_s[:, attn_w + kv_w:]
            ks4 = ks32.reshape(db, t_new, n_kv, HEAD_DIM)
            vs4 = vs32.reshape(db, t_new, n_kv, HEAD_DIM)
            mix_s = _sb_sample(cache_k, cache_v, l, page_table, _rows_to_heads(qs, db, t_new, n_kv),
                               _pad_page(ks4, page), _pad_page(vs4, page), pp)
            mix_s = _heads_to_rows(mix_s, db, t_new, n_kv)
        outs["kp"].append(k32.reshape(nb, s_len, n_kv, HEAD_DIM))
        outs["vp"].append(v32.reshape(nb, s_len, n_kv, HEAD_DIM))
        outs["ks"].append(ks4)
        outs["vs"].append(vs4)

        xp, hp = _out_norm(mix_p, w_out16, l, xp, mp_l[2], norm_ffn_g[l], mp_l[4], mp_l[3], tm3, s_len)
        xs, hs = _out_norm(mix_s, w_out16, l, xs, ms_l[2], norm_ffn_g[l], ms_l[4], ms_l[3], ms, 1)
        tn_ff = _pick_tile(w_gate.shape[2], (512, 256, 128))
        act_p, act_s = _swiglu(hp, w_gate, w_up, l, tm, tn_ff, hs)
        xp, xs = _gres(act_p, w_down, l, xp, mp_l[5], tm2, 512, s_len, act_s, xs, ms_l[5])

    y_prompt = _final_norm(xp, norm_final_g, tm).reshape(nb, s_len, d)
    y_sample = _final_norm(xs, norm_final_g, ms).reshape(db, t_new, d)
    return (y_prompt, y_sample, jnp.stack(outs["kp"]), jnp.stack(outs["vp"]), jnp.stack(outs["ikp"]),
            jnp.stack(outs["ks"]), jnp.stack(outs["vs"]), jnp.stack(outs["iks"]))
```

```python
import functools

import numpy as np
import jax
import jax.numpy as jnp
from jax import lax
from jax.experimental import pallas as pl
from jax.experimental.pallas import tpu as pltpu

F32 = jnp.float32
BF16 = jnp.bfloat16
I32 = jnp.int32

LANES = 128
HEAD_DIM = 128
KV_GROUP = 2
IDX_HEADS = 16
IDX_DIM = 128
TOPK_MAX = 256
ROPE_THETA = 10000.0
EPS = 1e-6
NEG = -1e30
INT_MIN = -2 ** 31
KEY_NEG_INF = -2139095041
ONES_ROWS = 16
LOG2E = 1.4426950408889634
EXP2_UNDERFLOW = -160.0
VMEM_LIMIT = 56 * 1024 * 1024

_NT = (((1,), (1,)), ((), ()))


def _cparams(sem):
    return pltpu.CompilerParams(dimension_semantics=sem, vmem_limit_bytes=VMEM_LIMIT)


def _split_bf16(x):
    hi = x.astype(BF16)
    lo = (x - hi.astype(F32)).astype(BF16)
    return hi, lo


def _sortable(x):
    bits = lax.bitcast_convert_type(x, I32)
    return bits ^ ((bits >> 31) & 0x7FFFFFFF)


def _adaln_kernel(c_ref, w_ref, b_ref, o_ref):
    c = c_ref[...]
    s = (c * jax.nn.sigmoid(c)).astype(BF16)
    o_ref[...] = jnp.dot(s, w_ref[...].astype(BF16), preferred_element_type=F32) + b_ref[...]


def _adaln(c_pad, w_ada, b_ada):
    depth, d, n = w_ada.shape
    r = c_pad.shape[0]
    tn = 1024
    return pl.pallas_call(
        _adaln_kernel,
        grid=(depth, n // tn),
        in_specs=[pl.BlockSpec((r, d), lambda l, j: (0, 0)),
                  pl.BlockSpec((None, d, tn), lambda l, j: (l, 0, j)),
                  pl.BlockSpec((None, 1, tn), lambda l, j: (l, 0, j))],
        out_specs=pl.BlockSpec((None, r, tn), lambda l, j: (l, 0, j)),
        out_shape=jax.ShapeDtypeStruct((depth, r, n), F32),
        compiler_params=_cparams(("arbitrary", "arbitrary")),
        name="adaln",
    )(c_pad, w_ada, b_ada.reshape(depth, 1, n))


def _norm_kernel(x_ref, g_ref, sc_ref, sh_ref, o_ref):
    x = x_ref[...]
    y = x * lax.rsqrt(jnp.mean(x * x, axis=-1, keepdims=True) + EPS) * g_ref[...]
    o_ref[...] = (y * (1.0 + sc_ref[...]) + sh_ref[...]).astype(o_ref.dtype)


def _final_norm_kernel(x_ref, g_ref, o_ref):
    x = x_ref[...]
    o_ref[...] = x * lax.rsqrt(jnp.mean(x * x, axis=-1, keepdims=True) + EPS) * g_ref[...]


def _mod_spec(mod, tm, width, rows_per_batch, col_of):
    if mod.ndim == 3:
        return pl.BlockSpec((None, 1, width), lambda *ij: ((ij[0] if col_of is None else ij[1]) * tm // rows_per_batch, 0,
                                                           0 if col_of is None else ij[0]))
    return pl.BlockSpec((tm, width), lambda *ij: ((ij[0] if col_of is None else ij[1]), 0 if col_of is None else ij[0]))


def _norm(x, g, sc, sh, tm, rows_per_batch):
    m, d = x.shape
    return pl.pallas_call(
        _norm_kernel,
        grid=(m // tm,),
        in_specs=[pl.BlockSpec((tm, d), lambda i: (i, 0)),
                  pl.BlockSpec((1, d), lambda i: (0, 0)),
                  _mod_spec(sc, tm, d, rows_per_batch, None),
                  _mod_spec(sh, tm, d, rows_per_batch, None)],
        out_specs=pl.BlockSpec((tm, d), lambda i: (i, 0)),
        out_shape=jax.ShapeDtypeStruct((m, d), BF16),
        compiler_params=_cparams(("arbitrary",)),
        name="norm_mod",
    )(x, g.reshape(1, d), sc, sh)


def _final_norm(x, g, tm):
    m, d = x.shape
    return pl.pallas_call(
        _final_norm_kernel,
        grid=(m // tm,),
        in_specs=[pl.BlockSpec((tm, d), lambda i: (i, 0)), pl.BlockSpec((1, d), lambda i: (0, 0))],
        out_specs=pl.BlockSpec((tm, d), lambda i: (i, 0)),
        out_shape=jax.ShapeDtypeStruct((m, d), F32),
        compiler_params=_cparams(("arbitrary",)),
        name="final_norm",
    )(x, g.reshape(1, d))


def _proj_kernel(h_ref, w_ref, cos_ref, sin_ref, *rest, rope, plain_tiles, n_out, scale, w_rows, n_skip):
    rest = rest[n_skip:]
    outs, wb_ref = rest[:n_out], rest[n_out]

    @pl.when(pl.program_id(1) == 0)
    def _():
        wb_ref[...] = w_ref[...].astype(BF16)

    if w_rows:
        acc = lax.dot_general(h_ref[...], wb_ref[...], _NT, preferred_element_type=F32)
    else:
        acc = jnp.dot(h_ref[...], wb_ref[...], preferred_element_type=F32)
    if scale != 1.0:
        acc = acc * scale
    tn = acc.shape[1]

    def emit(n_rope):
        if not n_rope:
            for o in outs:
                o[...] = acc.astype(o.dtype)
            return
        cos = cos_ref[...]
        sin = sin_ref[...]
        for c in range(tn // HEAD_DIM):
            y = acc[:, c * HEAD_DIM:(c + 1) * HEAD_DIM]
            if c < n_rope:
                y = y * cos + pltpu.roll(y, HEAD_DIM // 2, 1) * sin
            for o in outs:
                o[:, c * HEAD_DIM:(c + 1) * HEAD_DIM] = y.astype(o.dtype)

    if rope and plain_tiles:
        j = pl.program_id(0)
        plain = functools.reduce(jnp.logical_or, [j == t for t in plain_tiles])
        pl.when(plain)(lambda: emit(0))
        pl.when(jnp.logical_not(plain))(lambda: emit(rope))
    else:
        emit(rope)


def _proj(h, w, layer, col0, ncols, cos, sin, rope, out_dtypes, tm, tn, scale=1.0, plain_tiles=(), w_rows=False,
          stack=None):
    m, k = h.shape
    npb = cos.shape[0] // tm
    j0 = col0 // tn
    n_out = len(out_dtypes)
    if w_rows:
        w_spec = pl.BlockSpec((None, tn, k), lambda j, i: (layer, j + j0, 0))
    else:
        w_spec = pl.BlockSpec((None, k, tn), lambda j, i: (layer, 0, j + j0))
    in_specs = [pl.BlockSpec((tm, k), lambda j, i: (i, 0)),
                w_spec,
                pl.BlockSpec((tm, HEAD_DIM), lambda j, i: (i % npb, 0)),
                pl.BlockSpec((tm, HEAD_DIM), lambda j, i: (i % npb, 0))]
    out_specs = [pl.BlockSpec((tm, tn), lambda j, i: (i, j)) for _ in out_dtypes]
    out_shape = [jax.ShapeDtypeStruct((m, ncols), dt) for dt in out_dtypes]
    args, aliases = [h, w, cos, sin], {}
    if stack is not None:
        buf, slot, n_slots = stack
        out_specs[0] = pl.BlockSpec((None, tm, tn), lambda j, i: (slot, i, j))
        out_shape[0] = jax.ShapeDtypeStruct((n_slots, m, ncols), out_dtypes[0])
        if buf is not None:
            in_specs.append(pl.BlockSpec(memory_space=pl.ANY))
            args.append(buf)
            aliases = {len(args) - 1: 0}
    return pl.pallas_call(
        functools.partial(_proj_kernel, rope=rope, plain_tiles=tuple(plain_tiles), n_out=n_out, scale=scale,
                          w_rows=w_rows, n_skip=len(aliases)),
        grid=(ncols // tn, m // tm),
        in_specs=in_specs,
        out_specs=out_specs,
        out_shape=out_shape,
        scratch_shapes=[pltpu.VMEM((tn, k) if w_rows else (k, tn), BF16)],
        input_output_aliases=aliases,
        compiler_params=_cparams(("arbitrary", "arbitrary")),
        name="proj_rope" if rope else "proj",
    )(*args)


def _gres_kernel(h_ref, w_ref, x_ref, g_ref, hs_ref, xs_ref, gs_ref, o_ref, os_ref, wb_ref):
    @pl.when(pl.program_id(1) == 0)
    def _():
        wb_ref[...] = w_ref[...].astype(BF16)
        os_ref[...] = xs_ref[...] + gs_ref[...] * jnp.dot(hs_ref[...], wb_ref[...], preferred_element_type=F32)

    acc = jnp.dot(h_ref[...], wb_ref[...], preferred_element_type=F32)
    o_ref[...] = x_ref[...] + g_ref[...] * acc


def _gres(h, w, layer, x, gate, tm, tn, rows_per_batch, h_s, x_s, gate_s):
    m, k = h.shape
    n = w.shape[2]
    m_s = h_s.shape[0]
    return pl.pallas_call(
        _gres_kernel,
        grid=(n // tn, m // tm),
        in_specs=[pl.BlockSpec((tm, k), lambda j, i: (i, 0)),
                  pl.BlockSpec((None, k, tn), lambda j, i: (layer, 0, j)),
                  pl.BlockSpec((tm, tn), lambda j, i: (i, j)),
                  _mod_spec(gate, tm, tn, rows_per_batch, True),
                  pl.BlockSpec((m_s, k), lambda j, i: (0, 0)),
                  pl.BlockSpec((m_s, tn), lambda j, i: (0, j)),
                  pl.BlockSpec((m_s, tn), lambda j, i: (0, j))],
        out_specs=[pl.BlockSpec((tm, tn), lambda j, i: (i, j)), pl.BlockSpec((m_s, tn), lambda j, i: (0, j))],
        out_shape=[jax.ShapeDtypeStruct((m, n), F32), jax.ShapeDtypeStruct((m_s, n), F32)],
        scratch_shapes=[pltpu.VMEM((k, tn), BF16)],
        compiler_params=_cparams(("arbitrary", "arbitrary")),
        name="gated_residual",
    )(h, w, x, gate, h_s, x_s, gate_s)


def _cast_kernel(w_ref, o_ref):
    o_ref[...] = w_ref[...].astype(o_ref.dtype)


def _cast_bf16(w, tk):
    depth, k, n = w.shape
    return pl.pallas_call(
        _cast_kernel,
        grid=(depth, k // tk),
        in_specs=[pl.BlockSpec((None, tk, n), lambda l, i: (l, i, 0))],
        out_specs=pl.BlockSpec((None, tk, n), lambda l, i: (l, i, 0)),
        out_shape=jax.ShapeDtypeStruct(w.shape, BF16),
        compiler_params=_cparams(("arbitrary", "arbitrary")),
        name="cast_bf16",
    )(w)


def _out_norm_kernel(m_ref, w_ref, x_ref, gate_ref, g_ref, sc_ref, sh_ref, xo_ref, ho_ref):
    x = x_ref[...] + gate_ref[...] * jnp.dot(m_ref[...], w_ref[...], preferred_element_type=F32)
    xo_ref[...] = x
    y = x * lax.rsqrt(jnp.mean(x * x, axis=-1, keepdims=True) + EPS) * g_ref[...]
    ho_ref[...] = (y * (1.0 + sc_ref[...]) + sh_ref[...]).astype(ho_ref.dtype)


def _out_norm(mix, wb, layer, x, gate, g, sc, sh, tm, rows_per_batch):
    m, k = mix.shape
    d = wb.shape[2]
    row = lambda i: (i, 0)
    return pl.pallas_call(
        _out_norm_kernel,
        grid=(m // tm,),
        in_specs=[pl.BlockSpec((tm, k), row),
                  pl.BlockSpec((None, k, d), lambda i: (layer, 0, 0)),
                  pl.BlockSpec((tm, d), row),
                  _mod_spec(gate, tm, d, rows_per_batch, None),
                  pl.BlockSpec((1, d), lambda i: (0, 0)),
                  _mod_spec(sc, tm, d, rows_per_batch, None),
                  _mod_spec(sh, tm, d, rows_per_batch, None)],
        out_specs=[pl.BlockSpec((tm, d), row), pl.BlockSpec((tm, d), row)],
        out_shape=[jax.ShapeDtypeStruct((m, d), F32), jax.ShapeDtypeStruct((m, d), BF16)],
        compiler_params=_cparams(("arbitrary",)),
        name="out_proj_norm",
    )(mix, wb, x, gate, g.reshape(1, d), sc, sh)


def _swiglu_kernel(h_ref, wg_ref, wu_ref, hs_ref, o_ref, os_ref, wgb_ref, wub_ref):
    def act(h):
        a = jnp.dot(h, wgb_ref[...], preferred_element_type=F32)
        b = jnp.dot(h, wub_ref[...], preferred_element_type=F32)
        return (a * jax.nn.sigmoid(a) * b).astype(BF16)

    @pl.when(pl.program_id(1) == 0)
    def _():
        wgb_ref[...] = wg_ref[...].astype(BF16)
        wub_ref[...] = wu_ref[...].astype(BF16)
        os_ref[...] = act(hs_ref[...])

    o_ref[...] = act(h_ref[...])


def _swiglu(h, wg, wu, layer, tm, tn, h_s):
    m, k = h.shape
    n = wg.shape[2]
    m_s = h_s.shape[0]
    return pl.pallas_call(
        _swiglu_kernel,
        grid=(n // tn, m // tm),
        in_specs=[pl.BlockSpec((tm, k), lambda j, i: (i, 0)),
                  pl.BlockSpec((None, k, tn), lambda j, i: (layer, 0, j)),
                  pl.BlockSpec((None, k, tn), lambda j, i: (layer, 0, j)),
                  pl.BlockSpec((m_s, k), lambda j, i: (0, 0))],
        out_specs=[pl.BlockSpec((tm, tn), lambda j, i: (i, j)), pl.BlockSpec((m_s, tn), lambda j, i: (0, j))],
        out_shape=[jax.ShapeDtypeStruct((m, n), BF16), jax.ShapeDtypeStruct((m_s, n), BF16)],
        scratch_shapes=[pltpu.VMEM((k, tn), BF16), pltpu.VMEM((k, tn), BF16)],
        compiler_params=_cparams(("arbitrary", "arbitrary")),
        name="swiglu",
    )(h, wg, wu, h_s)


def _kth_threshold(count_ge, topk, shape):
    t0 = jnp.where(count_ge(jnp.zeros(shape, I32)) >= topk, 0, INT_MIN).astype(I32)

    def body(it, t):
        cand = t | jnp.left_shift(jnp.int32(1), 30 - it)
        return jnp.where(count_ge(cand) >= topk, cand, t)

    return lax.fori_loop(0, 31, body, t0)


def _tie_cutoff(count_tie_below, need, nbits, shape):
    def body(it, a):
        cand = a | jnp.left_shift(jnp.int32(1), nbits - 1 - it)
        return jnp.where(count_tie_below(cand) < need, cand, a)

    return lax.fori_loop(0, nbits, body, jnp.zeros(shape, I32))


def _dsa_prompt_kernel(q_ref, k_ref, vt_ref, iq_ref, ik_ref, iwt_ref, o_ref,
                       iqcat_ref, keys_ref, bias_ref, cut_ref, m_ref, acc_ref, q2_ref,
                       *, tq, ck, ng, topk, nbits, idx_scale):
    i = pl.program_id(1)
    nkc = ((i + 1) * tq + ck - 1) // ck
    qpos = i * tq + lax.broadcasted_iota(I32, (1, tq), 1)

    def kpos_of(c):
        return c * ck + lax.broadcasted_iota(I32, (ck, tq), 0)

    @pl.when(pl.program_id(2) == 0)
    def _index():
        for h in range(IDX_HEADS):
            hi, lo = _split_bf16(iq_ref[:, h * IDX_DIM:(h + 1) * IDX_DIM])
            iqcat_ref[:, h * 2 * IDX_DIM:h * 2 * IDX_DIM + IDX_DIM] = hi
            iqcat_ref[:, h * 2 * IDX_DIM + IDX_DIM:(h + 1) * 2 * IDX_DIM] = lo

        def score_chunk(c, carry):
            ikh = ik_ref[pl.ds(pl.multiple_of(c * ck, ck), ck), :].astype(BF16)
            ikcat = jnp.concatenate([ikh, ikh], axis=1)
            acc = jnp.zeros((ck, tq), F32)
            for h in range(IDX_HEADS):
                lg = lax.dot_general(ikcat, iqcat_ref[:, h * 2 * IDX_DIM:(h + 1) * 2 * IDX_DIM], _NT,
                                     preferred_element_type=F32)
                acc = acc + iwt_ref[h:h + 1, :] * jnp.maximum(lg, 0.0)
            sc = jnp.where(kpos_of(c) <= qpos, acc * idx_scale, -jnp.inf)
            keys_ref[c] = _sortable(sc)
            return carry

        lax.fori_loop(0, nkc, score_chunk, 0)

        def count(pred):
            def body(c, part):
                m = jnp.where(pred(keys_ref[c], c), 1.0, 0.0)
                return part + jnp.sum(m.reshape(ck // 32, 32, tq), axis=0)
            part = lax.fori_loop(0, nkc, body, jnp.zeros((32, tq), F32))
            return jnp.sum(part, axis=0, keepdims=True)

        thr = _kth_threshold(lambda cand: count(lambda kc, c: kc >= cand), float(topk), (1, tq))
        n_gt = count(lambda kc, c: kc > thr)
        n_ge = count(lambda kc, c: kc >= thr)
        need = float(topk) - n_gt
        crowded = (n_ge > float(topk)) & (thr > KEY_NEG_INF)
        cut_ref[...] = jnp.full((1, tq), 2 ** 30, I32)

        @pl.when(jnp.max(jnp.where(crowded, 1.0, 0.0)) > 0.5)
        def _():
            cut_ref[...] = _tie_cutoff(lambda cand: count(lambda kc, c: (kc == thr) & (kpos_of(c) < cand)),
                                       need, nbits, (1, tq))

        cut = cut_ref[...]

        def bias_chunk(c, carry):
            kc = keys_ref[c]
            kpos = kpos_of(c)
            sel = ((kc > thr) | ((kc == thr) & (kpos <= cut))) & (kpos <= qpos)
            bias_ref[c] = jnp.where(sel, 0.0, NEG).astype(BF16)
            return carry

        lax.fori_loop(0, nkc, bias_chunk, 0)

    m_ref[...] = jnp.full(m_ref.shape, NEG, F32)
    acc_ref[...] = jnp.zeros(acc_ref.shape, F32)
    for g in range(ng):
        for j in range(KV_GROUP):
            h = g * KV_GROUP + j
            q2_ref[g, j * tq:(j + 1) * tq, :] = q_ref[:, h * HEAD_DIM:(h + 1) * HEAD_DIM]
    ones = jnp.ones((ONES_ROWS, ck), BF16)

    def attend(c, carry):
        off = pl.multiple_of(c * ck, ck)
        b = bias_ref[c]
        b2 = jnp.concatenate([b] * KV_GROUP, axis=1)

        def scores(g):
            s = lax.dot_general(k_ref[pl.ds(off, ck), g * HEAD_DIM:(g + 1) * HEAD_DIM], q2_ref[g], _NT,
                                preferred_element_type=F32)
            return s.astype(BF16) + b2

        def softmax(g, s):
            m = m_ref[g]
            m_new = jnp.maximum(m, jnp.max(s, axis=0, keepdims=True).astype(F32))
            m_ref[g] = m_new
            return jnp.exp2(s - m_new.astype(BF16)), jnp.exp2(m - m_new)

        def weigh(g, p, alpha):
            vt = jnp.concatenate([vt_ref[c, g * HEAD_DIM:(g + 1) * HEAD_DIM, :], ones], axis=0)
            acc_ref[g] = alpha * acc_ref[g] + jnp.dot(vt, p, preferred_element_type=F32)

        ahead = 2
        s = {g: scores(g) for g in range(min(ahead, ng))}
        for g in range(ng):
            p, alpha = softmax(g, s.pop(g))
            if g + ahead < ng:
                s[g + ahead] = scores(g + ahead)
            weigh(g, p, alpha)
        return carry

    lax.fori_loop(0, nkc, attend, 0)
    for g in range(ng):
        out = acc_ref[g, :HEAD_DIM] * (1.0 / acc_ref[g, HEAD_DIM:HEAD_DIM + 1])
        for j in range(KV_GROUP):
            h = g * KV_GROUP + j
            o_ref[:, h * HEAD_DIM:(h + 1) * HEAD_DIM] = out[:, j * tq:(j + 1) * tq].T.astype(o_ref.dtype)


def _dsa_prompt(q, k, vt, iq, ikiw, iwt, nb, s_len, tq, ck, ng):
    n_kv = k.shape[1] // HEAD_DIM
    nq = s_len // tq
    topk = min(TOPK_MAX, s_len // 4)
    nh = ng * KV_GROUP
    kern = functools.partial(_dsa_prompt_kernel, tq=tq, ck=ck, ng=ng, topk=topk, nbits=max(1, (s_len - 1).bit_length()),
                             idx_scale=(IDX_HEADS ** -0.5) * (IDX_DIM ** -0.5))
    return pl.pallas_call(
        kern,
        grid=(nb, nq, n_kv // ng),
        in_specs=[pl.BlockSpec((tq, nh * HEAD_DIM), lambda b, i, g: (b * nq + i, g)),
                  pl.BlockSpec((s_len, ng * HEAD_DIM), lambda b, i, g: (b, g)),
                  pl.BlockSpec((None, s_len // ck, ng * HEAD_DIM, ck), lambda b, i, g: (b, 0, g, 0)),
                  pl.BlockSpec((tq, IDX_HEADS * IDX_DIM), lambda b, i, g: (b * nq + i, 0)),
                  pl.BlockSpec((s_len, IDX_DIM), lambda b, i, g: (b, 0)),
                  pl.BlockSpec((IDX_HEADS, tq), lambda b, i, g: (0, b * nq + i))],
        out_specs=pl.BlockSpec((tq, nh * HEAD_DIM), lambda b, i, g: (b * nq + i, g)),
        out_shape=jax.ShapeDtypeStruct(q.shape, BF16),
        scratch_shapes=[pltpu.VMEM((tq, IDX_HEADS * 2 * IDX_DIM), BF16),
                        pltpu.VMEM((s_len // ck, ck, tq), I32),
                        pltpu.VMEM((s_len // ck, ck, tq), BF16),
                        pltpu.VMEM((1, tq), I32),
                        pltpu.VMEM((ng, 1, KV_GROUP * tq), F32),
                        pltpu.VMEM((ng, HEAD_DIM + ONES_ROWS, KV_GROUP * tq), F32),
                        pltpu.VMEM((ng, KV_GROUP * tq, HEAD_DIM), BF16)],
        compiler_params=_cparams(("arbitrary", "arbitrary", "arbitrary")),
        name="dsa_prompt",
    )(q, k, vt, iq, ikiw, iwt)


def _softplus2(z):
    return jnp.maximum(z, 0.0) + jnp.log2(1.0 + jnp.exp2(-jnp.abs(z)))


def _sb_prompt_kernel(q_ref, k_ref, vt_ref, ln_ref, o_ref, tail_ref, acc_ref, *, tq, ck, ng):
    i = pl.program_id(1)
    nkc = ((i + 1) * tq + ck - 1) // ck
    nh = ng * KV_GROUP
    qpos = i * tq + lax.broadcasted_iota(I32, (1, tq), 1)
    tail_ref[...] = jnp.zeros(tail_ref.shape, F32)
    acc_ref[...] = jnp.zeros(acc_ref.shape, F32)

    def chunk(c, masked):
        off = pl.multiple_of(c * ck, ck)
        mask = (c * ck + lax.broadcasted_iota(I32, (ck, tq), 0)) < qpos if masked else None

        def logits(h):
            g = h // KV_GROUP
            return lax.dot_general(k_ref[pl.ds(off, ck), g * HEAD_DIM:(g + 1) * HEAD_DIM],
                                   q_ref[:, h * HEAD_DIM:(h + 1) * HEAD_DIM], _NT, preferred_element_type=F32)

        def keep(h, z):
            sp = jnp.where(z > 64.0, z, jnp.log2(1.0 + jnp.exp2(z)))
            spm = jnp.where(mask, sp, 0.0) if masked else sp
            la = jnp.dot(ln_ref[...], spm.astype(BF16), preferred_element_type=F32)
            tail = tail_ref[h]
            tail_ref[h] = tail + la[ck:ck + 1]
            return z - sp + tail, la

        def weigh(h, base, la):
            g = h // KV_GROUP
            a = jnp.exp2(base + la[:ck])
            if masked:
                a = jnp.where(mask, a, 0.0)
            acc_ref[h] += jnp.dot(vt_ref[c, g * HEAD_DIM:(g + 1) * HEAD_DIM, :], a.astype(BF16),
                                  preferred_element_type=F32)

        a1, a2 = 2, 4
        z = {h: logits(h) for h in range(min(a2, nh))}
        kept = {h: keep(h, z.pop(h)) for h in range(min(a1, nh))}
        for h in range(nh):
            if h + a1 < nh:
                kept[h + a1] = keep(h + a1, z.pop(h + a1))
            if h + a2 < nh:
                z[h + a2] = logits(h + a2)
            weigh(h, *kept.pop(h))

    chunk(nkc - 1, True)

    def live(carry):
        r, top = carry
        return (r < nkc) & (top > EXP2_UNDERFLOW)

    def body(carry):
        r, _ = carry
        chunk(nkc - 1 - r, False)
        return r + 1, jnp.max(tail_ref[...])

    lax.while_loop(live, body, (jnp.int32(1), jnp.max(tail_ref[...])))
    for h in range(nh):
        o_ref[:, h * HEAD_DIM:(h + 1) * HEAD_DIM] = acc_ref[h].T.astype(o_ref.dtype)


def _suffix_matrix(n):
    return jnp.asarray(np.tril(np.ones((n, n), np.float32), -1), dtype=BF16)


def _sb_prompt(q, k, vt, nb, s_len, tq, ck, ng):
    assert ck % tq == 0
    n_kv = k.shape[1] // HEAD_DIM
    nq = s_len // tq
    gw = KV_GROUP * HEAD_DIM
    ln = jnp.concatenate([-_suffix_matrix(ck).T, -jnp.ones((8, ck), BF16)], axis=0)
    return pl.pallas_call(
        functools.partial(_sb_prompt_kernel, tq=tq, ck=ck, ng=ng),
        grid=(nb, nq, n_kv // ng),
        in_specs=[pl.BlockSpec((tq, ng * gw), lambda b, i, g: (b * nq + i, g)),
                  pl.BlockSpec((s_len, ng * HEAD_DIM), lambda b, i, g: (b, g)),
                  pl.BlockSpec((None, s_len // ck, ng * HEAD_DIM, ck), lambda b, i, g: (b, 0, g, 0)),
                  pl.BlockSpec((ck + 8, ck), lambda b, i, g: (0, 0))],
        out_specs=pl.BlockSpec((tq, ng * gw), lambda b, i, g: (b * nq + i, g)),
        out_shape=jax.ShapeDtypeStruct(q.shape, BF16),
        scratch_shapes=[pltpu.VMEM((ng * KV_GROUP, 1, tq), F32),
                        pltpu.VMEM((ng * KV_GROUP, HEAD_DIM, tq), F32)],
        compiler_params=_cparams(("arbitrary", "arbitrary", "arbitrary")),
        name="sb_prompt",
    )(q, k, vt, ln)


def _page_consts(page, n_kv, rows):
    lane = np.arange(page * n_kv)
    expand = (lane[None, :] // n_kv == np.arange(page)[:, None]).astype(np.float32)
    valid = (lane[None, :] % n_kv == (np.arange(rows)[:, None] // (rows // n_kv))).astype(np.float32)
    return jnp.asarray(expand, BF16), jnp.asarray(expand.T.copy(), BF16), jnp.asarray(valid, F32)


def _idx_sample_kernel(pt_ref, iq_ref, w_ref, ikn_ref, *rest, pp, idx_scale):
    pages, (past_ref, new_ref) = rest[:pp], rest[pp:]
    hi, lo = _split_bf16(iq_ref[...])
    iqcat = jnp.concatenate([hi, hi, lo], axis=1)
    w = w_ref[...]
    rows = iqcat.shape[0]

    def logits(ik):
        ikh, ikl = _split_bf16(ik)
        return lax.dot_general(iqcat, jnp.concatenate([ikh, ikl, ikh], axis=1), _NT, preferred_element_type=F32)

    def score(lg):
        x = jnp.maximum(lg, 0.0) * w
        return jnp.sum(x.reshape(IDX_HEADS, rows // IDX_HEADS, x.shape[1]), axis=0) * idx_scale

    lgs = [logits(pages[j][...]) for j in range(pp)]
    for j in range(pp):
        past_ref[:, j * LANES:(j + 1) * LANES] = score(lgs[j])

    @pl.when(pl.program_id(1) == 0)
    def _():
        new_ref[...] = score(logits(ikn_ref[...]))


def _idx_sample(cache_idx_k, layer, page_table, iq_rows, w_rows, ik_new_pad, pp):
    db, n_pages = page_table.shape
    page = cache_idx_k.shape[2]
    rows = iq_rows.shape[1]
    r8 = rows // IDX_HEADS
    page_specs = [pl.BlockSpec((None, None, page, IDX_DIM),
                               functools.partial(lambda b, p, pt, j: (layer, pt[b, p * pp + j], 0, 0), j=j))
                  for j in range(pp)]
    return pl.pallas_call(
        functools.partial(_idx_sample_kernel, pp=pp, idx_scale=(IDX_HEADS ** -0.5) * (IDX_DIM ** -0.5)),
        grid_spec=pltpu.PrefetchScalarGridSpec(
            num_scalar_prefetch=1,
            grid=(db, n_pages // pp),
            in_specs=[pl.BlockSpec((None, rows, IDX_DIM), lambda b, p, pt: (b, 0, 0)),
                      pl.BlockSpec((None, rows, 1), lambda b, p, pt: (b, 0, 0)),
                      pl.BlockSpec((None, page, IDX_DIM), lambda b, p, pt: (b, 0, 0))] + page_specs,
            out_specs=[pl.BlockSpec((None, r8, pp * page), lambda b, p, pt: (b, 0, p)),
                       pl.BlockSpec((None, r8, page), lambda b, p, pt: (b, 0, 0))]),
        out_shape=[jax.ShapeDtypeStruct((db, r8, n_pages * page), F32),
                   jax.ShapeDtypeStruct((db, r8, page), F32)],
        compiler_params=_cparams(("arbitrary", "arbitrary")),
        name="idx_sample",
    )(page_table, iq_rows, w_rows, ik_new_pad, *([cache_idx_k] * pp))


def _select_sample_kernel(s_ref, o_ref, *, topk, nbits, past_len):
    sc = s_ref[...]
    rows, length = sc.shape
    qpos = past_len + lax.broadcasted_iota(I32, (rows, 1), 0) // KV_GROUP
    kpos = lax.broadcasted_iota(I32, (rows, length), 1)
    causal = kpos <= qpos
    key = _sortable(jnp.where(causal, sc, -jnp.inf))

    def count(pred):
        return jnp.sum(jnp.where(pred, 1.0, 0.0), axis=1, keepdims=True)

    thr = _kth_threshold(lambda cand: count(key >= cand), float(topk), (rows, 1))
    need = float(topk) - count(key > thr)
    cut = _tie_cutoff(lambda cand: count((key == thr) & (kpos < cand)), need, nbits, (rows, 1))
    sel = ((key > thr) | ((key == thr) & (kpos <= cut))) & causal
    o_ref[...] = jnp.where(sel, 1.0, 0.0)


def _select_sample(scores, past_len, n_new):
    db, rows, length = scores.shape
    topk = min(TOPK_MAX, (past_len + n_new) // 4)
    return pl.pallas_call(
        functools.partial(_select_sample_kernel, topk=topk, nbits=max(1, (length - 1).bit_length()), past_len=past_len),
        grid=(db,),
        in_specs=[pl.BlockSpec((None, rows, length), lambda b: (b, 0, 0))],
        out_specs=pl.BlockSpec((None, rows, length), lambda b: (b, 0, 0)),
        out_shape=jax.ShapeDtypeStruct(scores.shape, F32),
        compiler_params=_cparams(("arbitrary",)),
        name="select_sample",
    )(scores)


def _flat_bf16(ref):
    x = ref[...]
    return x.reshape(x.shape[0] * x.shape[1], x.shape[2]).astype(BF16)


def _dsa_sample_kernel(pt_ref, q_ref, mask_ref, e_ref, valid_ref, kn_ref, vn_ref, *rest,
                       pp, n_steps):
    kpages, vpages = rest[:pp], rest[pp:2 * pp]
    o_ref, m_ref, l_ref, acc_ref = rest[2 * pp:]
    p = pl.program_id(1)
    q = q_ref[...]
    rows = q.shape[0]
    reps = rows // mask_ref.shape[0]
    valid = valid_ref[...]

    @pl.when(p == 0)
    def _():
        m_ref[...] = jnp.full(m_ref.shape, NEG, F32)
        l_ref[...] = jnp.zeros(l_ref.shape, F32)
        acc_ref[...] = jnp.zeros(acc_ref.shape, F32)

    def pages(krefs, vrefs):
        ss = []
        for j, kref in enumerate(krefs):
            s = lax.dot_general(q, _flat_bf16(kref), _NT, preferred_element_type=F32)
            mrows = jnp.concatenate([mask_ref[:, j * LANES:(j + 1) * LANES]] * reps, axis=0).astype(BF16)
            keep = jnp.dot(mrows, e_ref[...], preferred_element_type=F32) * valid > 0.5
            ss.append(jnp.where(keep, s, NEG))
        m = m_ref[...]
        m_new = m
        for s in ss:
            m_new = jnp.maximum(m_new, jnp.max(s, axis=1, keepdims=True))
        alpha = jnp.exp2(m - m_new)
        l = alpha * l_ref[...]
        acc = alpha * acc_ref[...]
        for s, vref in zip(ss, vrefs):
            pr = jnp.exp2(s - m_new)
            l = l + jnp.sum(pr, axis=1, keepdims=True)
            acc = acc + jnp.dot(pr.astype(BF16), _flat_bf16(vref), preferred_element_type=F32)
        l_ref[...] = l
        acc_ref[...] = acc
        m_ref[...] = m_new

    @pl.when(p < n_steps - 1)
    def _():
        pages(kpages, vpages)

    @pl.when(p == n_steps - 1)
    def _():
        pages([kn_ref], [vn_ref])
        o_ref[...] = (acc_ref[...] / l_ref[...]).astype(o_ref.dtype)


def _dsa_sample(cache_k, cache_v, layer, page_table, q_rows, mask, k_new_pad, v_new_pad, pp):
    db, n_pages = page_table.shape
    page, n_kv = cache_k.shape[2], cache_k.shape[3]
    rows = q_rows.shape[1]
    n_steps = n_pages // pp + 1
    expand, _, valid = _page_consts(page, n_kv, rows)

    def cache_spec(j):
        return pl.BlockSpec((None, None, page, n_kv, HEAD_DIM),
                            lambda b, p, pt: (layer, pt[b, jnp.minimum(p, n_steps - 2) * pp + j], 0, 0, 0))

    new_spec = pl.BlockSpec((None, page, n_kv, HEAD_DIM), lambda b, p, pt: (b, 0, 0, 0))
    mask_spec = pl.BlockSpec((None, mask.shape[1], pp * page),
                             lambda b, p, pt: (b, 0, jnp.where(p == n_steps - 1, n_pages // pp, p)))
    return pl.pallas_call(
        functools.partial(_dsa_sample_kernel, pp=pp, n_steps=n_steps),
        grid_spec=pltpu.PrefetchScalarGridSpec(
            num_scalar_prefetch=1,
            grid=(db, n_steps),
            in_specs=[pl.BlockSpec((None, rows, HEAD_DIM), lambda b, p, pt: (b, 0, 0)),
                      mask_spec,
                      pl.BlockSpec(expand.shape, lambda b, p, pt: (0, 0)),
                      pl.BlockSpec(valid.shape, lambda b, p, pt: (0, 0)),
                      new_spec, new_spec] + [cache_spec(j) for j in range(pp)] * 2,
            out_specs=pl.BlockSpec((None, rows, HEAD_DIM), lambda b, p, pt: (b, 0, 0)),
            scratch_shapes=[pltpu.VMEM((rows, 1), F32), pltpu.VMEM((rows, 1), F32), pltpu.VMEM((rows, HEAD_DIM), F32)]),
        out_shape=jax.ShapeDtypeStruct((db, rows, HEAD_DIM), BF16),
        compiler_params=_cparams(("arbitrary", "arbitrary")),
        name="dsa_sample",
    )(page_table, q_rows, mask, expand, valid, k_new_pad, v_new_pad, *([cache_k] * pp), *([cache_v] * pp))


def _sb_sample_kernel(pt_ref, q_ref, e_ref, c_ref, valid_ref, u_ref, kn_ref, vn_ref, tail_in_ref, acc_in_ref, *rest,
                      pp, top, with_new, past_len):
    kpages, vpages = rest[:pp], rest[pp:2 * pp]
    tail_ref, acc_ref = rest[2 * pp:]
    p = pl.program_id(1)
    q = q_ref[...]
    rows = q.shape[0]
    n_kv = valid_ref.shape[1] // e_ref.shape[0]
    valid = valid_ref[...]
    page_len = e_ref.shape[0]
    qpos = past_len + (lax.broadcasted_iota(I32, (rows, 1), 0) % (rows // n_kv)) // KV_GROUP

    @pl.when(p == 0)
    def _():
        tail_ref[...] = tail_in_ref[...]
        acc_ref[...] = acc_in_ref[...]

    def pages(krefs, vrefs, starts):
        zfs = [lax.dot_general(q, _flat_bf16(kref), _NT, preferred_element_type=F32) * valid for kref in krefs]
        zs = []
        for zf in zfs:
            hi, lo = _split_bf16(zf)
            zz = jnp.dot(jnp.concatenate([hi, lo], axis=0), c_ref[...], preferred_element_type=F32)
            zs.append(zz[:rows] + zz[rows:])
        parts = []
        tail = tail_ref[...]
        for z, start in zip(zs, starts):
            mask = (start + lax.broadcasted_iota(I32, (rows, page_len), 1)) < qpos
            sp = _softplus2(z)
            lk = jnp.where(mask, -sp, 0.0)
            hi, lo = _split_bf16(lk)
            ll = jnp.dot(jnp.concatenate([hi, lo], axis=0), u_ref[...], preferred_element_type=F32)
            parts.append((mask, z - sp + tail, ll))
            tail = tail + jnp.sum(lk, axis=1, keepdims=True)
        tail_ref[...] = tail
        acc = acc_ref[...]
        for (mask, base, ll), vref in zip(parts, vrefs):
            a = jnp.where(mask, jnp.exp2(base + ll[:rows] + ll[rows:]), 0.0).astype(BF16)
            ae = (jnp.dot(a, e_ref[...], preferred_element_type=F32) * valid).astype(BF16)
            acc = acc + jnp.dot(ae, _flat_bf16(vref), preferred_element_type=F32)
        acc_ref[...] = acc

    if with_new:
        @pl.when(p == 0)
        def _():
            pages([kn_ref], [vn_ref], [past_len])

    @pl.when(p >= int(with_new))
    def _():
        pages(kpages, vpages, [(top - 1 - ((p - int(with_new)) * pp + j)) * page_len for j in range(pp)])


def _sb_sample_part(cache_k, cache_v, layer, page_table, q_rows, k_new_pad, v_new_pad, tail, acc, pp, top, count, with_new):
    db, n_pages = page_table.shape
    page, n_kv = cache_k.shape[2], cache_k.shape[3]
    rows = q_rows.shape[1]
    first = int(with_new)
    expand, compact, valid = _page_consts(page, n_kv, rows)

    def cache_spec(j):
        return pl.BlockSpec((None, None, page, n_kv, HEAD_DIM),
                            lambda b, p, pt: (layer, pt[b, top - 1 - (jnp.maximum(p - first, 0) * pp + j)], 0, 0, 0))

    new_spec = pl.BlockSpec((None, page, n_kv, HEAD_DIM), lambda b, p, pt: (b, 0, 0, 0))
    const = lambda a: pl.BlockSpec(a.shape, lambda b, p, pt: (0, 0))
    per_b = lambda a: pl.BlockSpec((None,) + a.shape[1:], lambda b, p, pt: (b, 0, 0))
    u = _suffix_matrix(page)
    return pl.pallas_call(
        functools.partial(_sb_sample_kernel, pp=pp, top=top, with_new=with_new, past_len=n_pages * page),
        grid_spec=pltpu.PrefetchScalarGridSpec(
            num_scalar_prefetch=1,
            grid=(db, count // pp + first),
            in_specs=[per_b(q_rows), const(expand), const(compact), const(valid), const(u),
                      new_spec, new_spec, per_b(tail), per_b(acc)] + [cache_spec(j) for j in range(pp)] * 2,
            out_specs=[per_b(tail), per_b(acc)]),
        out_shape=[jax.ShapeDtypeStruct(tail.shape, F32), jax.ShapeDtypeStruct(acc.shape, F32)],
        compiler_params=_cparams(("arbitrary", "arbitrary")),
        name="sb_sample",
    )(page_table, q_rows, expand, compact, valid, u, k_new_pad, v_new_pad, tail, acc,
      *([cache_k] * pp), *([cache_v] * pp))


def _sb_sample(cache_k, cache_v, layer, page_table, q_rows, k_new_pad, v_new_pad, pp):
    db, n_pages = page_table.shape
    rows = q_rows.shape[1]
    args = (cache_k, cache_v, layer, page_table, q_rows, k_new_pad, v_new_pad)
    state = (jnp.zeros((db, rows, 1), F32), jnp.zeros((db, rows, HEAD_DIM), F32))
    state = tuple(_sb_sample_part(*args, *state, pp, n_pages, pp, True))
    rest = n_pages - pp
    if rest > 0:
        state = lax.cond(jnp.max(state[0]) > EXP2_UNDERFLOW,
                         lambda s: tuple(_sb_sample_part(*args, *s, pp, rest, rest, False)), lambda s: s, state)
    return state[1].astype(BF16)


def _rope_tables(pos):
    half = HEAD_DIM // 2
    inv = np.float32(ROPE_THETA) ** (-np.arange(half, dtype=np.float32) / np.float32(half))
    ang = pos.astype(np.float32)[:, None] * inv[None, :].astype(np.float32)
    cos, sin = np.cos(ang), np.sin(ang)
    return jnp.asarray(np.concatenate([cos, cos], axis=1)), jnp.asarray(np.concatenate([-sin, sin], axis=1))


def _pick_tile(n, prefs):
    for t in prefs:
        if n % t == 0:
            return t
    return n


def _rows_to_heads(x, db, t, n_kv):
    return x.reshape(db, t, n_kv, KV_GROUP, HEAD_DIM).transpose(0, 2, 1, 3, 4).reshape(db, n_kv * t * KV_GROUP, HEAD_DIM)


def _heads_to_rows(x, db, t, n_kv):
    return x.reshape(db, n_kv, t, KV_GROUP, HEAD_DIM).transpose(0, 2, 1, 3, 4).reshape(db * t, n_kv * KV_GROUP * HEAD_DIM)


def _chunk_t(v, nb, s_len, ck):
    return v.reshape(nb, s_len // ck, ck, v.shape[1]).transpose(0, 1, 3, 2)


def _pad_page(x, page):
    return jnp.pad(x, [(0, 0), (0, page - x.shape[1])] + [(0, 0)] * (x.ndim - 2))


def kernel(x_prompt, x_sample, c_prompt, c_sample, cache_k, cache_v, cache_idx_k, page_table, norm_mix_g, norm_ffn_g,
           w_ada, b_ada, w_in_dsa, w_in_sb, w_out, w_gate, w_up, w_down, norm_final_g):
    nb, s_len, d = x_prompt.shape
    db, t_new, _ = x_sample.shape
    depth = w_ada.shape[0]
    page, n_kv = cache_k.shape[2], cache_k.shape[3]
    n_pages = page_table.shape[1]
    past_len = n_pages * page
    attn_w = w_out.shape[1]
    kv_w = n_kv * HEAD_DIM
    idx_w = IDX_HEADS * IDX_DIM
    mp, ms = nb * s_len, db * t_new

    tm = _pick_tile(s_len, (1024, 512, 256, 128))
    tm2 = _pick_tile(s_len, (512, 256, 128))
    tm3 = _pick_tile(s_len, (256, 128))
    w_out16 = _cast_bf16(w_out, _pick_tile(w_out.shape[1], (512, 256, 128)))
    tq_dsa, tq_sb, ck = 256, 256, 256
    ng = _pick_tile(n_kv, (4, 2, 1))
    pp = _pick_tile(n_pages, (8, 4, 2, 1))
    attn_scale = HEAD_DIM ** -0.5 * LOG2E

    n_c = nb + db
    c_all = jnp.pad(jnp.concatenate([c_prompt, c_sample], axis=0), ((0, (-n_c) % 16), (0, 0)))
    mod = _adaln(c_all, w_ada, b_ada).reshape(depth, c_all.shape[0], 6, d)

    cos_p, sin_p = _rope_tables(np.arange(s_len))
    cos_s, sin_s = _rope_tables(np.tile(past_len + np.arange(t_new), db))

    xp = x_prompt.reshape(mp, d)
    xs = x_sample.reshape(ms, d)
    outs = {n: [] for n in ("ikp", "ks", "vs", "iks")}
    k_all = v_all = None

    for l in range(depth):
        mp_l = [mod[l, :nb, j].reshape(nb, 1, d) for j in range(6)]
        ms_l = [jnp.repeat(mod[l, nb:n_c, j], t_new, axis=0) for j in range(6)]
        hp = _norm(xp, norm_mix_g[l], mp_l[1], mp_l[0], tm, s_len)
        hs = _norm(xs, norm_mix_g[l], ms_l[1], ms_l[0], ms, 1)
        i = l // 2
        if l % 2 == 0:
            w = jnp.swapaxes(w_in_dsa, 1, 2)
            tail0 = attn_w + 2 * kv_w + idx_w
            w_tail = jnp.pad(w[i, tail0:, :], ((0, 2 * LANES - IDX_DIM - IDX_HEADS), (0, 0)))[None]
            tn = 1024
            hd = tn // HEAD_DIM
            (q,) = _proj(hp, w, i, 0, attn_w, cos_p, sin_p, hd, (BF16,), tm2, tn, attn_scale, w_rows=True)
            k_all, k16 = _proj(hp, w, i, attn_w, kv_w, cos_p, sin_p, hd, (F32, BF16), tm2, tn, w_rows=True,
                               stack=(k_all, l, depth))
            v_all, v16 = _proj(hp, w, i, attn_w + kv_w, kv_w, cos_p, sin_p, 0, (F32, BF16), tm2, tn, w_rows=True,
                               stack=(v_all, l, depth))
            (iq,) = _proj(hp, w, i, attn_w + 2 * kv_w, idx_w, cos_p, sin_p, hd, (F32,), tm2, tn, w_rows=True)
            (ikiw,) = _proj(hp, w_tail, 0, 0, 2 * LANES, cos_p, sin_p, 1, (F32,), tm2, 2 * LANES, w_rows=True)
            mix_p = _dsa_prompt(q, k16, _chunk_t(v16, nb, s_len, ck), iq, ikiw, ikiw[:, IDX_DIM:IDX_DIM + IDX_HEADS].T,
                                nb, s_len, tq_dsa, ck, ng)
            outs["ikp"].append(ikiw[:, :IDX_DIM].reshape(nb, s_len, IDX_DIM))

            (all_s,) = _proj(hs, w, i, 0, tail0, cos_s, sin_s, hd, (F32,), ms, tn, plain_tiles=((attn_w + kv_w) // tn,),
                             w_rows=True)
            (ikiw_s,) = _proj(hs, w_tail, 0, 0, 2 * LANES, cos_s, sin_s, 1, (F32,), ms, 2 * LANES, w_rows=True)
            qs = (all_s[:, :attn_w] * attn_scale).astype(BF16)
            ks32 = all_s[:, attn_w:attn_w + kv_w]
            vs32 = all_s[:, attn_w + kv_w:attn_w + 2 * kv_w]
            iqs = all_s[:, attn_w + 2 * kv_w:]
            iks = ikiw_s[:, :IDX_DIM].reshape(db, t_new, IDX_DIM)
            iws = ikiw_s[:, IDX_DIM:IDX_DIM + IDX_HEADS].reshape(db, t_new, IDX_HEADS)
            iq_rows = jnp.broadcast_to(iqs.reshape(db, t_new, IDX_HEADS, 1, IDX_DIM).transpose(0, 2, 1, 3, 4),
                                       (db, IDX_HEADS, t_new, KV_GROUP, IDX_DIM)).reshape(db, -1, IDX_DIM)
            w_rows = jnp.broadcast_to(iws.transpose(0, 2, 1)[..., None], (db, IDX_HEADS, t_new, KV_GROUP)).reshape(db, -1, 1)
            sc_past, sc_new = _idx_sample(cache_idx_k, i, page_table, iq_rows, w_rows, _pad_page(iks, page),
                                          _pick_tile(n_pages, (16, 8, 4, 2, 1)))
            scores = jnp.concatenate([sc_past, sc_new, jnp.zeros((db, sc_new.shape[1], (pp - 1) * page), F32)], axis=2)
            mask = _select_sample(scores, past_len, t_new)
            ks4 = ks32.reshape(db, t_new, n_kv, HEAD_DIM)
            vs4 = vs32.reshape(db, t_new, n_kv, HEAD_DIM)
            mix_s = _dsa_sample(cache_k, cache_v, l, page_table, _rows_to_heads(qs, db, t_new, n_kv), mask,
                                _pad_page(ks4, page), _pad_page(vs4, page), pp)
            mix_s = _heads_to_rows(mix_s, db, t_new, n_kv)
            outs["iks"].append(iks)
        else:
            w = w_in_sb
            tn = 1024
            (q,) = _proj(hp, w, i, 0, attn_w, cos_p, sin_p, 0, (BF16,), tm2, tn, attn_scale)
            k_all, k16 = _proj(hp, w, i, attn_w, kv_w, cos_p, sin_p, 0, (F32, BF16), tm2, tn, stack=(k_all, l, depth))
            v_all, v16 = _proj(hp, w, i, attn_w + kv_w, kv_w, cos_p, sin_p, 0, (F32, BF16), tm2, tn,
                               stack=(v_all, l, depth))
            mix_p = _sb_prompt(q, k16, _chunk_t(v16, nb, s_len, ck), nb, s_len, tq_sb, ck, ng)
            (all_s,) = _proj(hs, w, i, 0, attn_w + 2 * kv_w, cos_s, sin_s, 0, (F32,), ms, tn)
            qs = (all_s[:, :attn_w] * attn_scale).astype(BF16)
            ks32 = all_s[:, attn_w:attn_w + kv_w]
            vs32 = all_s[:, attn_w + kv_w:]
            ks4 = ks32.reshape(db, t_new, n_kv, HEAD_DIM)
            vs4 = vs32.reshape(db, t_new, n_kv, HEAD_DIM)
            mix_s = _sb_sample(cache_k, cache_v, l, page_table, _rows_to_heads(qs, db, t_new, n_kv),
                               _pad_page(ks4, page), _pad_page(vs4, page), pp)
            mix_s = _heads_to_rows(mix_s, db, t_new, n_kv)
        outs["ks"].append(ks4)
        outs["vs"].append(vs4)

        xp, hp = _out_norm(mix_p, w_out16, l, xp, mp_l[2], norm_ffn_g[l], mp_l[4], mp_l[3], tm3, s_len)
        xs, hs = _out_norm(mix_s, w_out16, l, xs, ms_l[2], norm_ffn_g[l], ms_l[4], ms_l[3], ms, 1)
        tn_ff = _pick_tile(w_gate.shape[2], (512, 256, 128))
        act_p, act_s = _swiglu(hp, w_gate, w_up, l, tm, tn_ff, hs)
        xp, xs = _gres(act_p, w_down, l, xp, mp_l[5], tm2, 512, s_len, act_s, xs, ms_l[5])

    y_prompt = _final_norm(xp, norm_final_g, tm).reshape(nb, s_len, d)
    y_sample = _final_norm(xs, norm_final_g, ms).reshape(db, t_new, d)
    kv_shape = (depth, nb, s_len, n_kv, HEAD_DIM)
    return (y_prompt, y_sample, k_all.reshape(kv_shape), v_all.reshape(kv_shape), jnp.stack(outs["ikp"]),
            jnp.stack(outs["ks"]), jnp.stack(outs["vs"]), jnp.stack(outs["iks"]))
```

```python
import functools

import numpy as np
import jax
import jax.numpy as jnp
from jax import lax
from jax.experimental import pallas as pl
from jax.experimental.pallas import tpu as pltpu

F32 = jnp.float32
BF16 = jnp.bfloat16
I32 = jnp.int32

LANES = 128
HEAD_DIM = 128
KV_GROUP = 2
IDX_HEADS = 16
IDX_DIM = 128
TOPK_MAX = 256
ROPE_THETA = 10000.0
EPS = 1e-6
NEG = -1e30
INT_MIN = -2 ** 31
KEY_NEG_INF = -2139095041
ONES_ROWS = 16
LOG2E = 1.4426950408889634
EXP2_UNDERFLOW = -160.0
VMEM_LIMIT = 56 * 1024 * 1024

_NT = (((1,), (1,)), ((), ()))


def _cparams(sem):
    return pltpu.CompilerParams(dimension_semantics=sem, vmem_limit_bytes=VMEM_LIMIT)


def _split_bf16(x):
    hi = x.astype(BF16)
    lo = (x - hi.astype(F32)).astype(BF16)
    return hi, lo


def _sortable(x):
    bits = lax.bitcast_convert_type(x, I32)
    return bits ^ ((bits >> 31) & 0x7FFFFFFF)


def _adaln_kernel(c_ref, w_ref, b_ref, o_ref):
    c = c_ref[...]
    s = (c * jax.nn.sigmoid(c)).astype(BF16)
    o_ref[...] = jnp.dot(s, w_ref[...].astype(BF16), preferred_element_type=F32) + b_ref[...]


def _adaln(c_pad, w_ada, b_ada):
    depth, d, n = w_ada.shape
    r = c_pad.shape[0]
    tn = 1024
    return pl.pallas_call(
        _adaln_kernel,
        grid=(depth, n // tn),
        in_specs=[pl.BlockSpec((r, d), lambda l, j: (0, 0)),
                  pl.BlockSpec((None, d, tn), lambda l, j: (l, 0, j)),
                  pl.BlockSpec((None, 1, tn), lambda l, j: (l, 0, j))],
        out_specs=pl.BlockSpec((None, r, tn), lambda l, j: (l, 0, j)),
        out_shape=jax.ShapeDtypeStruct((depth, r, n), F32),
        compiler_params=_cparams(("arbitrary", "arbitrary")),
        name="adaln",
    )(c_pad, w_ada, b_ada.reshape(depth, 1, n))


def _norm_kernel(x_ref, g_ref, sc_ref, sh_ref, o_ref):
    x = x_ref[...]
    y = x * lax.rsqrt(jnp.mean(x * x, axis=-1, keepdims=True) + EPS) * g_ref[...]
    o_ref[...] = (y * (1.0 + sc_ref[...]) + sh_ref[...]).astype(o_ref.dtype)


def _final_norm_kernel(x_ref, g_ref, o_ref):
    x = x_ref[...]
    o_ref[...] = x * lax.rsqrt(jnp.mean(x * x, axis=-1, keepdims=True) + EPS) * g_ref[...]


def _mod_spec(mod, tm, width, rows_per_batch, col_of):
    if mod.ndim == 3:
        return pl.BlockSpec((None, 1, width), lambda *ij: ((ij[0] if col_of is None else ij[1]) * tm // rows_per_batch, 0,
                                                           0 if col_of is None else ij[0]))
    return pl.BlockSpec((tm, width), lambda *ij: ((ij[0] if col_of is None else ij[1]), 0 if col_of is None else ij[0]))


def _norm(x, g, sc, sh, tm, rows_per_batch):
    m, d = x.shape
    return pl.pallas_call(
        _norm_kernel,
        grid=(m // tm,),
        in_specs=[pl.BlockSpec((tm, d), lambda i: (i, 0)),
                  pl.BlockSpec((1, d), lambda i: (0, 0)),
                  _mod_spec(sc, tm, d, rows_per_batch, None),
                  _mod_spec(sh, tm, d, rows_per_batch, None)],
        out_specs=pl.BlockSpec((tm, d), lambda i: (i, 0)),
        out_shape=jax.ShapeDtypeStruct((m, d), BF16),
        compiler_params=_cparams(("arbitrary",)),
        name="norm_mod",
    )(x, g.reshape(1, d), sc, sh)


def _final_norm(x, g, tm):
    m, d = x.shape
    return pl.pallas_call(
        _final_norm_kernel,
        grid=(m // tm,),
        in_specs=[pl.BlockSpec((tm, d), lambda i: (i, 0)), pl.BlockSpec((1, d), lambda i: (0, 0))],
        out_specs=pl.BlockSpec((tm, d), lambda i: (i, 0)),
        out_shape=jax.ShapeDtypeStruct((m, d), F32),
        compiler_params=_cparams(("arbitrary",)),
        name="final_norm",
    )(x, g.reshape(1, d))


def _proj_kernel(h_ref, w_ref, cos_ref, sin_ref, *rest, rope, plain_tiles, n_out, scale, w_rows, n_skip):
    rest = rest[n_skip:]
    outs, wb_ref = rest[:n_out], rest[n_out]

    @pl.when(pl.program_id(1) == 0)
    def _():
        wb_ref[...] = w_ref[...].astype(BF16)

    if w_rows:
        acc = lax.dot_general(h_ref[...], wb_ref[...], _NT, preferred_element_type=F32)
    else:
        acc = jnp.dot(h_ref[...], wb_ref[...], preferred_element_type=F32)
    if scale != 1.0:
        acc = acc * scale
    tn = acc.shape[1]

    def emit(n_rope):
        if not n_rope:
            for o in outs:
                o[...] = acc.astype(o.dtype)
            return
        cos = cos_ref[...]
        sin = sin_ref[...]
        for c in range(tn // HEAD_DIM):
            y = acc[:, c * HEAD_DIM:(c + 1) * HEAD_DIM]
            if c < n_rope:
                y = y * cos + pltpu.roll(y, HEAD_DIM // 2, 1) * sin
            for o in outs:
                o[:, c * HEAD_DIM:(c + 1) * HEAD_DIM] = y.astype(o.dtype)

    if rope and plain_tiles:
        j = pl.program_id(0)
        plain = functools.reduce(jnp.logical_or, [j == t for t in plain_tiles])
        pl.when(plain)(lambda: emit(0))
        pl.when(jnp.logical_not(plain))(lambda: emit(rope))
    else:
        emit(rope)


def _proj(h, w, layer, col0, ncols, cos, sin, rope, out_dtypes, tm, tn, scale=1.0, plain_tiles=(), w_rows=False,
          stack=None):
    m, k = h.shape
    npb = cos.shape[0] // tm
    j0 = col0 // tn
    n_out = len(out_dtypes)
    if w_rows:
        w_spec = pl.BlockSpec((None, tn, k), lambda j, i: (layer, j + j0, 0))
    else:
        w_spec = pl.BlockSpec((None, k, tn), lambda j, i: (layer, 0, j + j0))
    in_specs = [pl.BlockSpec((tm, k), lambda j, i: (i, 0)),
                w_spec,
                pl.BlockSpec((tm, HEAD_DIM), lambda j, i: (i % npb, 0)),
                pl.BlockSpec((tm, HEAD_DIM), lambda j, i: (i % npb, 0))]
    out_specs = [pl.BlockSpec((tm, tn), lambda j, i: (i, j)) for _ in out_dtypes]
    out_shape = [jax.ShapeDtypeStruct((m, ncols), dt) for dt in out_dtypes]
    args, aliases = [h, w, cos, sin], {}
    if stack is not None:
        buf, slot, n_slots = stack
        out_specs[0] = pl.BlockSpec((None, tm, tn), lambda j, i: (slot, i, j))
        out_shape[0] = jax.ShapeDtypeStruct((n_slots, m, ncols), out_dtypes[0])
        if buf is not None:
            in_specs.append(pl.BlockSpec(memory_space=pl.ANY))
            args.append(buf)
            aliases = {len(args) - 1: 0}
    return pl.pallas_call(
        functools.partial(_proj_kernel, rope=rope, plain_tiles=tuple(plain_tiles), n_out=n_out, scale=scale,
                          w_rows=w_rows, n_skip=len(aliases)),
        grid=(ncols // tn, m // tm),
        in_specs=in_specs,
        out_specs=out_specs,
        out_shape=out_shape,
        scratch_shapes=[pltpu.VMEM((tn, k) if w_rows else (k, tn), BF16)],
        input_output_aliases=aliases,
        compiler_params=_cparams(("arbitrary", "arbitrary")),
        name="proj_rope" if rope else "proj",
    )(*args)


def _gres_kernel(h_ref, w_ref, x_ref, g_ref, hs_ref, xs_ref, gs_ref, o_ref, os_ref, wb_ref):
    @pl.when(pl.program_id(1) == 0)
    def _():
        wb_ref[...] = w_ref[...].astype(BF16)
        os_ref[...] = xs_ref[...] + gs_ref[...] * jnp.dot(hs_ref[...], wb_ref[...], preferred_element_type=F32)

    acc = jnp.dot(h_ref[...], wb_ref[...], preferred_element_type=F32)
    o_ref[...] = x_ref[...] + g_ref[...] * acc


def _gres(h, w, layer, x, gate, tm, tn, rows_per_batch, h_s, x_s, gate_s):
    m, k = h.shape
    n = w.shape[2]
    m_s = h_s.shape[0]
    return pl.pallas_call(
        _gres_kernel,
        grid=(n // tn, m // tm),
        in_specs=[pl.BlockSpec((tm, k), lambda j, i: (i, 0)),
                  pl.BlockSpec((None, k, tn), lambda j, i: (layer, 0, j)),
                  pl.BlockSpec((tm, tn), lambda j, i: (i, j)),
                  _mod_spec(gate, tm, tn, rows_per_batch, True),
                  pl.BlockSpec((m_s, k), lambda j, i: (0, 0)),
                  pl.BlockSpec((m_s, tn), lambda j, i: (0, j)),
                  pl.BlockSpec((m_s, tn), lambda j, i: (0, j))],
        out_specs=[pl.BlockSpec((tm, tn), lambda j, i: (i, j)), pl.BlockSpec((m_s, tn), lambda j, i: (0, j))],
        out_shape=[jax.ShapeDtypeStruct((m, n), F32), jax.ShapeDtypeStruct((m_s, n), F32)],
        scratch_shapes=[pltpu.VMEM((k, tn), BF16)],
        compiler_params=_cparams(("arbitrary", "arbitrary")),
        name="gated_residual",
    )(h, w, x, gate, h_s, x_s, gate_s)


def _cast_kernel(w_ref, o_ref):
    o_ref[...] = w_ref[...].astype(o_ref.dtype)


def _cast_bf16(w, tk):
    depth, k, n = w.shape
    return pl.pallas_call(
        _cast_kernel,
        grid=(depth, k // tk),
        in_specs=[pl.BlockSpec((None, tk, n), lambda l, i: (l, i, 0))],
        out_specs=pl.BlockSpec((None, tk, n), lambda l, i: (l, i, 0)),
        out_shape=jax.ShapeDtypeStruct(w.shape, BF16),
        compiler_params=_cparams(("arbitrary", "arbitrary")),
        name="cast_bf16",
    )(w)


def _out_norm_kernel(m_ref, w_ref, x_ref, gate_ref, g_ref, sc_ref, sh_ref, xo_ref, ho_ref):
    x = x_ref[...] + gate_ref[...] * jnp.dot(m_ref[...], w_ref[...], preferred_element_type=F32)
    xo_ref[...] = x
    y = x * lax.rsqrt(jnp.mean(x * x, axis=-1, keepdims=True) + EPS) * g_ref[...]
    ho_ref[...] = (y * (1.0 + sc_ref[...]) + sh_ref[...]).astype(ho_ref.dtype)


def _out_norm(mix, wb, layer, x, gate, g, sc, sh, tm, rows_per_batch):
    m, k = mix.shape
    d = wb.shape[2]
    row = lambda i: (i, 0)
    return pl.pallas_call(
        _out_norm_kernel,
        grid=(m // tm,),
        in_specs=[pl.BlockSpec((tm, k), row),
                  pl.BlockSpec((None, k, d), lambda i: (layer, 0, 0)),
                  pl.BlockSpec((tm, d), row),
                  _mod_spec(gate, tm, d, rows_per_batch, None),
                  pl.BlockSpec((1, d), lambda i: (0, 0)),
                  _mod_spec(sc, tm, d, rows_per_batch, None),
                  _mod_spec(sh, tm, d, rows_per_batch, None)],
        out_specs=[pl.BlockSpec((tm, d), row), pl.BlockSpec((tm, d), row)],
        out_shape=[jax.ShapeDtypeStruct((m, d), F32), jax.ShapeDtypeStruct((m, d), BF16)],
        compiler_params=_cparams(("arbitrary",)),
        name="out_proj_norm",
    )(mix, wb, x, gate, g.reshape(1, d), sc, sh)


def _swiglu_kernel(h_ref, wg_ref, wu_ref, hs_ref, o_ref, os_ref, wgb_ref, wub_ref):
    def act(h):
        a = jnp.dot(h, wgb_ref[...], preferred_element_type=F32)
        b = jnp.dot(h, wub_ref[...], preferred_element_type=F32)
        return (a * jax.nn.sigmoid(a) * b).astype(BF16)

    @pl.when(pl.program_id(1) == 0)
    def _():
        wgb_ref[...] = wg_ref[...].astype(BF16)
        wub_ref[...] = wu_ref[...].astype(BF16)
        os_ref[...] = act(hs_ref[...])

    o_ref[...] = act(h_ref[...])


def _swiglu(h, wg, wu, layer, tm, tn, h_s):
    m, k = h.shape
    n = wg.shape[2]
    m_s = h_s.shape[0]
    return pl.pallas_call(
        _swiglu_kernel,
        grid=(n // tn, m // tm),
        in_specs=[pl.BlockSpec((tm, k), lambda j, i: (i, 0)),
                  pl.BlockSpec((None, k, tn), lambda j, i: (layer, 0, j)),
                  pl.BlockSpec((None, k, tn), lambda j, i: (layer, 0, j)),
                  pl.BlockSpec((m_s, k), lambda j, i: (0, 0))],
        out_specs=[pl.BlockSpec((tm, tn), lambda j, i: (i, j)), pl.BlockSpec((m_s, tn), lambda j, i: (0, j))],
        out_shape=[jax.ShapeDtypeStruct((m, n), BF16), jax.ShapeDtypeStruct((m_s, n), BF16)],
        scratch_shapes=[pltpu.VMEM((k, tn), BF16), pltpu.VMEM((k, tn), BF16)],
        compiler_params=_cparams(("arbitrary", "arbitrary")),
        name="swiglu",
    )(h, wg, wu, h_s)


def _kth_threshold(count_ge, topk, shape):
    t0 = jnp.where(count_ge(jnp.zeros(shape, I32)) >= topk, 0, INT_MIN).astype(I32)

    def body(it, t):
        cand = t | jnp.left_shift(jnp.int32(1), 30 - it)
        return jnp.where(count_ge(cand) >= topk, cand, t)

    return lax.fori_loop(0, 31, body, t0)


def _tie_cutoff(count_tie_below, need, nbits, shape):
    def body(it, a):
        cand = a | jnp.left_shift(jnp.int32(1), nbits - 1 - it)
        return jnp.where(count_tie_below(cand) < need, cand, a)

    return lax.fori_loop(0, nbits, body, jnp.zeros(shape, I32))


def _dsa_prompt_kernel(q_ref, k_ref, vt_ref, iq_ref, ik_ref, iwt_ref, o_ref,
                       iqcat_ref, keys_ref, bias_ref, cut_ref, m_ref, acc_ref, q2_ref,
                       *, tq, ck, ng, topk, nbits, idx_scale):
    i = pl.program_id(1)
    nkc = ((i + 1) * tq + ck - 1) // ck
    qpos = i * tq + lax.broadcasted_iota(I32, (1, tq), 1)

    def kpos_of(c):
        return c * ck + lax.broadcasted_iota(I32, (ck, tq), 0)

    @pl.when(pl.program_id(2) == 0)
    def _index():
        for h in range(IDX_HEADS):
            hi, lo = _split_bf16(iq_ref[:, h * IDX_DIM:(h + 1) * IDX_DIM])
            iqcat_ref[:, h * 2 * IDX_DIM:h * 2 * IDX_DIM + IDX_DIM] = hi
            iqcat_ref[:, h * 2 * IDX_DIM + IDX_DIM:(h + 1) * 2 * IDX_DIM] = lo

        def score_chunk(c, carry):
            ikh = ik_ref[pl.ds(pl.multiple_of(c * ck, ck), ck), :].astype(BF16)
            ikcat = jnp.concatenate([ikh, ikh], axis=1)
            acc = jnp.zeros((ck, tq), F32)
            for h in range(IDX_HEADS):
                lg = lax.dot_general(ikcat, iqcat_ref[:, h * 2 * IDX_DIM:(h + 1) * 2 * IDX_DIM], _NT,
                                     preferred_element_type=F32)
                acc = acc + iwt_ref[h:h + 1, :] * jnp.maximum(lg, 0.0)
            sc = jnp.where(kpos_of(c) <= qpos, acc * idx_scale, -jnp.inf)
            keys_ref[c] = _sortable(sc)
            return carry

        lax.fori_loop(0, nkc, score_chunk, 0)

        def count(pred):
            def body(c, part):
                m = jnp.where(pred(keys_ref[c], c), 1.0, 0.0)
                return part + jnp.sum(m.reshape(ck // 32, 32, tq), axis=0)
            part = lax.fori_loop(0, nkc, body, jnp.zeros((32, tq), F32))
            return jnp.sum(part, axis=0, keepdims=True)

        thr = _kth_threshold(lambda cand: count(lambda kc, c: kc >= cand), float(topk), (1, tq))
        n_gt = count(lambda kc, c: kc > thr)
        n_ge = count(lambda kc, c: kc >= thr)
        need = float(topk) - n_gt
        crowded = (n_ge > float(topk)) & (thr > KEY_NEG_INF)
        cut_ref[...] = jnp.full((1, tq), 2 ** 30, I32)

        @pl.when(jnp.max(jnp.where(crowded, 1.0, 0.0)) > 0.5)
        def _():
            cut_ref[...] = _tie_cutoff(lambda cand: count(lambda kc, c: (kc == thr) & (kpos_of(c) < cand)),
                                       need, nbits, (1, tq))

        cut = cut_ref[...]

        def bias_chunk(c, carry):
            kc = keys_ref[c]
            kpos = kpos_of(c)
            sel = ((kc > thr) | ((kc == thr) & (kpos <= cut))) & (kpos <= qpos)
            bias_ref[c] = jnp.where(sel, 0.0, NEG).astype(BF16)
            return carry

        lax.fori_loop(0, nkc, bias_chunk, 0)

    m_ref[...] = jnp.full(m_ref.shape, NEG, F32)
    acc_ref[...] = jnp.zeros(acc_ref.shape, F32)
    for g in range(ng):
        for j in range(KV_GROUP):
            h = g * KV_GROUP + j
            q2_ref[g, j * tq:(j + 1) * tq, :] = q_ref[:, h * HEAD_DIM:(h + 1) * HEAD_DIM]
    ones = jnp.ones((ONES_ROWS, ck), BF16)

    def attend(c, carry):
        off = pl.multiple_of(c * ck, ck)
        b = bias_ref[c]
        b2 = jnp.concatenate([b] * KV_GROUP, axis=1)

        def scores(g):
            s = lax.dot_general(k_ref[pl.ds(off, ck), g * HEAD_DIM:(g + 1) * HEAD_DIM], q2_ref[g], _NT,
                                preferred_element_type=F32)
            return s.astype(BF16) + b2

        def softmax(g, s):
            m = m_ref[g]
            m_new = jnp.maximum(m, jnp.max(s, axis=0, keepdims=True).astype(F32))
            m_ref[g] = m_new
            return jnp.exp2(s - m_new.astype(BF16)), jnp.exp2(m - m_new)

        def weigh(g, p, alpha):
            vt = jnp.concatenate([vt_ref[c, g * HEAD_DIM:(g + 1) * HEAD_DIM, :], ones], axis=0)
            acc_ref[g] = alpha * acc_ref[g] + jnp.dot(vt, p, preferred_element_type=F32)

        ahead = 2
        s = {g: scores(g) for g in range(min(ahead, ng))}
        for g in range(ng):
            p, alpha = softmax(g, s.pop(g))
            if g + ahead < ng:
                s[g + ahead] = scores(g + ahead)
            weigh(g, p, alpha)
        return carry

    lax.fori_loop(0, nkc, attend, 0)
    for g in range(ng):
        out = acc_ref[g, :HEAD_DIM] * (1.0 / acc_ref[g, HEAD_DIM:HEAD_DIM + 1])
        for j in range(KV_GROUP):
            h = g * KV_GROUP + j
            o_ref[:, h * HEAD_DIM:(h + 1) * HEAD_DIM] = out[:, j * tq:(j + 1) * tq].T.astype(o_ref.dtype)


def _dsa_prompt(q, k, vt, iq, ikiw, iwt, nb, s_len, tq, ck, ng):
    n_kv = k.shape[1] // HEAD_DIM
    nq = s_len // tq
    topk = min(TOPK_MAX, s_len // 4)
    nh = ng * KV_GROUP
    kern = functools.partial(_dsa_prompt_kernel, tq=tq, ck=ck, ng=ng, topk=topk, nbits=max(1, (s_len - 1).bit_length()),
                             idx_scale=(IDX_HEADS ** -0.5) * (IDX_DIM ** -0.5))
    return pl.pallas_call(
        kern,
        grid=(nb, nq, n_kv // ng),
        in_specs=[pl.BlockSpec((tq, nh * HEAD_DIM), lambda b, i, g: (b * nq + i, g)),
                  pl.BlockSpec((s_len, ng * HEAD_DIM), lambda b, i, g: (b, g)),
                  pl.BlockSpec((None, s_len // ck, ng * HEAD_DIM, ck), lambda b, i, g: (b, 0, g, 0)),
                  pl.BlockSpec((tq, IDX_HEADS * IDX_DIM), lambda b, i, g: (b * nq + i, 0)),
                  pl.BlockSpec((s_len, IDX_DIM), lambda b, i, g: (b, 0)),
                  pl.BlockSpec((IDX_HEADS, tq), lambda b, i, g: (0, b * nq + i))],
        out_specs=pl.BlockSpec((tq, nh * HEAD_DIM), lambda b, i, g: (b * nq + i, g)),
        out_shape=jax.ShapeDtypeStruct(q.shape, BF16),
        scratch_shapes=[pltpu.VMEM((tq, IDX_HEADS * 2 * IDX_DIM), BF16),
                        pltpu.VMEM((s_len // ck, ck, tq), I32),
                        pltpu.VMEM((s_len // ck, ck, tq), BF16),
                        pltpu.VMEM((1, tq), I32),
                        pltpu.VMEM((ng, 1, KV_GROUP * tq), F32),
                        pltpu.VMEM((ng, HEAD_DIM + ONES_ROWS, KV_GROUP * tq), F32),
                        pltpu.VMEM((ng, KV_GROUP * tq, HEAD_DIM), BF16)],
        compiler_params=_cparams(("arbitrary", "arbitrary", "arbitrary")),
        name="dsa_prompt",
    )(q, k, vt, iq, ikiw, iwt)


def _softplus2(z):
    return jnp.maximum(z, 0.0) + jnp.log2(1.0 + jnp.exp2(-jnp.abs(z)))


def _sb_prompt_kernel(q_ref, k_ref, vt_ref, ln_ref, o_ref, tail_ref, acc_ref, *, tq, ck, ng):
    i = pl.program_id(1)
    nkc = ((i + 1) * tq + ck - 1) // ck
    nh = ng * KV_GROUP
    qpos = i * tq + lax.broadcasted_iota(I32, (1, tq), 1)
    tail_ref[...] = jnp.zeros(tail_ref.shape, F32)
    acc_ref[...] = jnp.zeros(acc_ref.shape, F32)

    def chunk(c, masked):
        off = pl.multiple_of(c * ck, ck)
        mask = (c * ck + lax.broadcasted_iota(I32, (ck, tq), 0)) < qpos if masked else None

        def logits(h):
            g = h // KV_GROUP
            return lax.dot_general(k_ref[pl.ds(off, ck), g * HEAD_DIM:(g + 1) * HEAD_DIM],
                                   q_ref[:, h * HEAD_DIM:(h + 1) * HEAD_DIM], _NT, preferred_element_type=F32)

        def keep(h, z):
            sp = jnp.where(z > 64.0, z, jnp.log2(1.0 + jnp.exp2(z)))
            spm = jnp.where(mask, sp, 0.0) if masked else sp
            la = jnp.dot(ln_ref[...], spm.astype(BF16), preferred_element_type=F32)
            tail = tail_ref[h]
            tail_ref[h] = tail + la[ck:ck + 1]
            return z - sp + tail, la

        def weigh(h, base, la):
            g = h // KV_GROUP
            a = jnp.exp2(base + la[:ck])
            if masked:
                a = jnp.where(mask, a, 0.0)
            acc_ref[h] += jnp.dot(vt_ref[c, g * HEAD_DIM:(g + 1) * HEAD_DIM, :], a.astype(BF16),
                                  preferred_element_type=F32)

        a1, a2 = 2, 4
        z = {h: logits(h) for h in range(min(a2, nh))}
        kept = {h: keep(h, z.pop(h)) for h in range(min(a1, nh))}
        for h in range(nh):
            if h + a1 < nh:
                kept[h + a1] = keep(h + a1, z.pop(h + a1))
            if h + a2 < nh:
                z[h + a2] = logits(h + a2)
            weigh(h, *kept.pop(h))

    chunk(nkc - 1, True)

    def live(carry):
        r, top = carry
        return (r < nkc) & (top > EXP2_UNDERFLOW)

    def body(carry):
        r, _ = carry
        chunk(nkc - 1 - r, False)
        return r + 1, jnp.max(tail_ref[...])

    lax.while_loop(live, body, (jnp.int32(1), jnp.max(tail_ref[...])))
    for h in range(nh):
        o_ref[:, h * HEAD_DIM:(h + 1) * HEAD_DIM] = acc_ref[h].T.astype(o_ref.dtype)


def _suffix_matrix(n):
    return jnp.asarray(np.tril(np.ones((n, n), np.float32), -1), dtype=BF16)


def _sb_prompt(q, k, vt, nb, s_len, tq, ck, ng):
    assert ck % tq == 0
    n_kv = k.shape[1] // HEAD_DIM
    nq = s_len // tq
    gw = KV_GROUP * HEAD_DIM
    ln = jnp.concatenate([-_suffix_matrix(ck).T, -jnp.ones((8, ck), BF16)], axis=0)
    return pl.pallas_call(
        functools.partial(_sb_prompt_kernel, tq=tq, ck=ck, ng=ng),
        grid=(nb, nq, n_kv // ng),
        in_specs=[pl.BlockSpec((tq, ng * gw), lambda b, i, g: (b * nq + i, g)),
                  pl.BlockSpec((s_len, ng * HEAD_DIM), lambda b, i, g: (b, g)),
                  pl.BlockSpec((None, s_len // ck, ng * HEAD_DIM, ck), lambda b, i, g: (b, 0, g, 0)),
                  pl.BlockSpec((ck + 8, ck), lambda b, i, g: (0, 0))],
        out_specs=pl.BlockSpec((tq, ng * gw), lambda b, i, g: (b * nq + i, g)),
        out_shape=jax.ShapeDtypeStruct(q.shape, BF16),
        scratch_shapes=[pltpu.VMEM((ng * KV_GROUP, 1, tq), F32),
                        pltpu.VMEM((ng * KV_GROUP, HEAD_DIM, tq), F32)],
        compiler_params=_cparams(("arbitrary", "arbitrary", "arbitrary")),
        name="sb_prompt",
    )(q, k, vt, ln)


def _page_consts(page, n_kv, rows):
    lane = np.arange(page * n_kv)
    expand = (lane[None, :] // n_kv == np.arange(page)[:, None]).astype(np.float32)
    valid = (lane[None, :] % n_kv == (np.arange(rows)[:, None] // (rows // n_kv))).astype(np.float32)
    return jnp.asarray(expand, BF16), jnp.asarray(expand.T.copy(), BF16), jnp.asarray(valid, F32)


def _idx_sample_kernel(pt_ref, iq_ref, w_ref, ikn_ref, *rest, pp, idx_scale):
    pages, (past_ref, new_ref) = rest[:pp], rest[pp:]
    hi, lo = _split_bf16(iq_ref[...])
    iqcat = jnp.concatenate([hi, lo], axis=1)
    w = w_ref[...]
    rows = iqcat.shape[0]

    def logits(ik):
        ikh = ik.astype(BF16)
        return lax.dot_general(iqcat, jnp.concatenate([ikh, ikh], axis=1), _NT, preferred_element_type=F32)

    def score(lg):
        x = jnp.maximum(lg, 0.0) * w
        return jnp.sum(x.reshape(IDX_HEADS, rows // IDX_HEADS, x.shape[1]), axis=0) * idx_scale

    lgs = [logits(pages[j][...]) for j in range(pp)]
    for j in range(pp):
        past_ref[:, j * LANES:(j + 1) * LANES] = score(lgs[j])

    @pl.when(pl.program_id(1) == 0)
    def _():
        new_ref[...] = score(logits(ikn_ref[...]))


def _idx_sample(cache_idx_k, layer, page_table, iq_rows, w_rows, ik_new_pad, pp):
    db, n_pages = page_table.shape
    page = cache_idx_k.shape[2]
    rows = iq_rows.shape[1]
    r8 = rows // IDX_HEADS
    page_specs = [pl.BlockSpec((None, None, page, IDX_DIM),
                               functools.partial(lambda b, p, pt, j: (layer, pt[b, p * pp + j], 0, 0), j=j))
                  for j in range(pp)]
    return pl.pallas_call(
        functools.partial(_idx_sample_kernel, pp=pp, idx_scale=(IDX_HEADS ** -0.5) * (IDX_DIM ** -0.5)),
        grid_spec=pltpu.PrefetchScalarGridSpec(
            num_scalar_prefetch=1,
            grid=(db, n_pages // pp),
            in_specs=[pl.BlockSpec((None, rows, IDX_DIM), lambda b, p, pt: (b, 0, 0)),
                      pl.BlockSpec((None, rows, 1), lambda b, p, pt: (b, 0, 0)),
                      pl.BlockSpec((None, page, IDX_DIM), lambda b, p, pt: (b, 0, 0))] + page_specs,
            out_specs=[pl.BlockSpec((None, r8, pp * page), lambda b, p, pt: (b, 0, p)),
                       pl.BlockSpec((None, r8, page), lambda b, p, pt: (b, 0, 0))]),
        out_shape=[jax.ShapeDtypeStruct((db, r8, n_pages * page), F32),
                   jax.ShapeDtypeStruct((db, r8, page), F32)],
        compiler_params=_cparams(("arbitrary", "arbitrary")),
        name="idx_sample",
    )(page_table, iq_rows, w_rows, ik_new_pad, *([cache_idx_k] * pp))


def _select_sample_kernel(s_ref, o_ref, cut_ref, *, topk, nbits, past_len):
    sc = s_ref[...]
    rows, length = sc.shape
    qpos = past_len + lax.broadcasted_iota(I32, (rows, 1), 0) // KV_GROUP
    kpos = lax.broadcasted_iota(I32, (rows, length), 1)
    causal = kpos <= qpos
    key = _sortable(jnp.where(causal, sc, -jnp.inf))

    n_acc = 4

    def count(pred_of):
        accs = [jnp.zeros((rows, LANES), F32) for _ in range(n_acc)]
        for j in range(length // LANES):
            sl = slice(j * LANES, (j + 1) * LANES)
            accs[j % n_acc] = accs[j % n_acc] + jnp.where(pred_of(key[:, sl], kpos[:, sl]), 1.0, 0.0)
        return jnp.sum((accs[0] + accs[1]) + (accs[2] + accs[3]), axis=1, keepdims=True)

    thr = _kth_threshold(lambda cand: count(lambda k, p: k >= cand), float(topk), (rows, 1))
    need = float(topk) - count(lambda k, p: k > thr)
    crowded = (count(lambda k, p: k >= thr) > float(topk)) & (thr > KEY_NEG_INF)
    cut_ref[...] = jnp.full((rows, 1), 2 ** 30, I32)

    @pl.when(jnp.max(jnp.where(crowded, 1.0, 0.0)) > 0.5)
    def _():
        cut_ref[...] = _tie_cutoff(lambda cand: count(lambda k, p: (k == thr) & (p < cand)), need, nbits, (rows, 1))

    sel = ((key > thr) | ((key == thr) & (kpos <= cut_ref[...]))) & causal
    o_ref[...] = jnp.where(sel, 1.0, 0.0)


def _select_sample(scores, past_len, n_new):
    db, rows, length = scores.shape
    topk = min(TOPK_MAX, (past_len + n_new) // 4)
    return pl.pallas_call(
        functools.partial(_select_sample_kernel, topk=topk, nbits=max(1, (length - 1).bit_length()), past_len=past_len),
        grid=(db,),
        in_specs=[pl.BlockSpec((None, rows, length), lambda b: (b, 0, 0))],
        out_specs=pl.BlockSpec((None, rows, length), lambda b: (b, 0, 0)),
        out_shape=jax.ShapeDtypeStruct(scores.shape, F32),
        scratch_shapes=[pltpu.VMEM((rows, 1), I32)],
        compiler_params=_cparams(("arbitrary",)),
        name="select_sample",
    )(scores)


def _flat_bf16(ref):
    x = ref[...]
    return x.reshape(x.shape[0] * x.shape[1], x.shape[2]).astype(BF16)


def _dsa_sample_kernel(pt_ref, q_ref, mask_ref, e_ref, valid_ref, kn_ref, vn_ref, *rest,
                       pp, n_steps):
    kpages, vpages = rest[:pp], rest[pp:2 * pp]
    o_ref, m_ref, l_ref, acc_ref = rest[2 * pp:]
    p = pl.program_id(1)
    q = q_ref[...]
    rows = q.shape[0]
    reps = rows // mask_ref.shape[0]
    valid = valid_ref[...]

    @pl.when(p == 0)
    def _():
        m_ref[...] = jnp.full(m_ref.shape, NEG, F32)
        l_ref[...] = jnp.zeros(l_ref.shape, F32)
        acc_ref[...] = jnp.zeros(acc_ref.shape, F32)

    def pages(krefs, vrefs):
        ss = []
        for j, kref in enumerate(krefs):
            s = lax.dot_general(q, _flat_bf16(kref), _NT, preferred_element_type=F32)
            mrows = jnp.concatenate([mask_ref[:, j * LANES:(j + 1) * LANES]] * reps, axis=0).astype(BF16)
            keep = jnp.dot(mrows, e_ref[...], preferred_element_type=F32) * valid > 0.5
            ss.append(jnp.where(keep, s, NEG))
        m = m_ref[...]
        m_new = m
        for s in ss:
            m_new = jnp.maximum(m_new, jnp.max(s, axis=1, keepdims=True))
        alpha = jnp.exp2(m - m_new)
        l = alpha * l_ref[...]
        acc = alpha * acc_ref[...]
        for s, vref in zip(ss, vrefs):
            pr = jnp.exp2(s - m_new)
            l = l + jnp.sum(pr, axis=1, keepdims=True)
            acc = acc + jnp.dot(pr.astype(BF16), _flat_bf16(vref), preferred_element_type=F32)
        l_ref[...] = l
        acc_ref[...] = acc
        m_ref[...] = m_new

    @pl.when(p < n_steps - 1)
    def _():
        pages(kpages, vpages)

    @pl.when(p == n_steps - 1)
    def _():
        pages([kn_ref], [vn_ref])
        o_ref[...] = (acc_ref[...] / l_ref[...]).astype(o_ref.dtype)


def _dsa_sample(cache_k, cache_v, layer, page_table, q_rows, mask, k_new_pad, v_new_pad, pp):
    db, n_pages = page_table.shape
    page, n_kv = cache_k.shape[2], cache_k.shape[3]
    rows = q_rows.shape[1]
    n_steps = n_pages // pp + 1
    expand, _, valid = _page_consts(page, n_kv, rows)

    def cache_spec(j):
        return pl.BlockSpec((None, None, page, n_kv, HEAD_DIM),
                            lambda b, p, pt: (layer, pt[b, jnp.minimum(p, n_steps - 2) * pp + j], 0, 0, 0))

    new_spec = pl.BlockSpec((None, page, n_kv, HEAD_DIM), lambda b, p, pt: (b, 0, 0, 0))
    mask_spec = pl.BlockSpec((None, mask.shape[1], pp * page),
                             lambda b, p, pt: (b, 0, jnp.where(p == n_steps - 1, n_pages // pp, p)))
    return pl.pallas_call(
        functools.partial(_dsa_sample_kernel, pp=pp, n_steps=n_steps),
        grid_spec=pltpu.PrefetchScalarGridSpec(
            num_scalar_prefetch=1,
            grid=(db, n_steps),
            in_specs=[pl.BlockSpec((None, rows, HEAD_DIM), lambda b, p, pt: (b, 0, 0)),
                      mask_spec,
                      pl.BlockSpec(expand.shape, lambda b, p, pt: (0, 0)),
                      pl.BlockSpec(valid.shape, lambda b, p, pt: (0, 0)),
                      new_spec, new_spec] + [cache_spec(j) for j in range(pp)] * 2,
            out_specs=pl.BlockSpec((None, rows, HEAD_DIM), lambda b, p, pt: (b, 0, 0)),
            scratch_shapes=[pltpu.VMEM((rows, 1), F32), pltpu.VMEM((rows, 1), F32), pltpu.VMEM((rows, HEAD_DIM), F32)]),
        out_shape=jax.ShapeDtypeStruct((db, rows, HEAD_DIM), BF16),
        compiler_params=_cparams(("arbitrary", "arbitrary")),
        name="dsa_sample",
    )(page_table, q_rows, mask, expand, valid, k_new_pad, v_new_pad, *([cache_k] * pp), *([cache_v] * pp))


def _sb_sample_kernel(pt_ref, q_ref, e_ref, c_ref, valid_ref, u_ref, kn_ref, vn_ref, tail_in_ref, acc_in_ref, *rest,
                      pp, top, with_new, past_len):
    kpages, vpages = rest[:pp], rest[pp:2 * pp]
    tail_ref, acc_ref = rest[2 * pp:]
    p = pl.program_id(1)
    q = q_ref[...]
    rows = q.shape[0]
    n_kv = valid_ref.shape[1] // e_ref.shape[0]
    valid = valid_ref[...]
    page_len = e_ref.shape[0]
    qpos = past_len + (lax.broadcasted_iota(I32, (rows, 1), 0) % (rows // n_kv)) // KV_GROUP

    @pl.when(p == 0)
    def _():
        tail_ref[...] = tail_in_ref[...]
        acc_ref[...] = acc_in_ref[...]

    def pages(krefs, vrefs, starts):
        zfs = [lax.dot_general(q, _flat_bf16(kref), _NT, preferred_element_type=F32) * valid for kref in krefs]
        zs = []
        for zf in zfs:
            hi, lo = _split_bf16(zf)
            zz = jnp.dot(jnp.concatenate([hi, lo], axis=0), c_ref[...], preferred_element_type=F32)
            zs.append(zz[:rows] + zz[rows:])
        parts = []
        tail = tail_ref[...]
        for z, start in zip(zs, starts):
            mask = (start + lax.broadcasted_iota(I32, (rows, page_len), 1)) < qpos
            sp = _softplus2(z)
            lk = jnp.where(mask, -sp, 0.0)
            hi, lo = _split_bf16(lk)
            ll = jnp.dot(jnp.concatenate([hi, lo], axis=0), u_ref[...], preferred_element_type=F32)
            parts.append((mask, z - sp + tail, ll))
            tail = tail + jnp.sum(lk, axis=1, keepdims=True)
        tail_ref[...] = tail
        acc = acc_ref[...]
        for (mask, base, ll), vref in zip(parts, vrefs):
            a = jnp.where(mask, jnp.exp2(base + ll[:rows] + ll[rows:]), 0.0).astype(BF16)
            ae = (jnp.dot(a, e_ref[...], preferred_element_type=F32) * valid).astype(BF16)
            acc = acc + jnp.dot(ae, _flat_bf16(vref), preferred_element_type=F32)
        acc_ref[...] = acc

    if with_new:
        @pl.when(p == 0)
        def _():
            pages([kn_ref], [vn_ref], [past_len])

    @pl.when(p >= int(with_new))
    def _():
        pages(kpages, vpages, [(top - 1 - ((p - int(with_new)) * pp + j)) * page_len for j in range(pp)])


def _sb_sample_part(cache_k, cache_v, layer, page_table, q_rows, k_new_pad, v_new_pad, tail, acc, pp, top, count, with_new):
    db, n_pages = page_table.shape
    page, n_kv = cache_k.shape[2], cache_k.shape[3]
    rows = q_rows.shape[1]
    first = int(with_new)
    expand, compact, valid = _page_consts(page, n_kv, rows)

    def cache_spec(j):
        return pl.BlockSpec((None, None, page, n_kv, HEAD_DIM),
                            lambda b, p, pt: (layer, pt[b, top - 1 - (jnp.maximum(p - first, 0) * pp + j)], 0, 0, 0))

    new_spec = pl.BlockSpec((None, page, n_kv, HEAD_DIM), lambda b, p, pt: (b, 0, 0, 0))
    const = lambda a: pl.BlockSpec(a.shape, lambda b, p, pt: (0, 0))
    per_b = lambda a: pl.BlockSpec((None,) + a.shape[1:], lambda b, p, pt: (b, 0, 0))
    u = _suffix_matrix(page)
    return pl.pallas_call(
        functools.partial(_sb_sample_kernel, pp=pp, top=top, with_new=with_new, past_len=n_pages * page),
        grid_spec=pltpu.PrefetchScalarGridSpec(
            num_scalar_prefetch=1,
            grid=(db, count // pp + first),
            in_specs=[per_b(q_rows), const(expand), const(compact), const(valid), const(u),
                      new_spec, new_spec, per_b(tail), per_b(acc)] + [cache_spec(j) for j in range(pp)] * 2,
            out_specs=[per_b(tail), per_b(acc)]),
        out_shape=[jax.ShapeDtypeStruct(tail.shape, F32), jax.ShapeDtypeStruct(acc.shape, F32)],
        compiler_params=_cparams(("arbitrary", "arbitrary")),
        name="sb_sample",
    )(page_table, q_rows, expand, compact, valid, u, k_new_pad, v_new_pad, tail, acc,
      *([cache_k] * pp), *([cache_v] * pp))


def _sb_sample(cache_k, cache_v, layer, page_table, q_rows, k_new_pad, v_new_pad, pp):
    db, n_pages = page_table.shape
    rows = q_rows.shape[1]
    args = (cache_k, cache_v, layer, page_table, q_rows, k_new_pad, v_new_pad)
    state = (jnp.zeros((db, rows, 1), F32), jnp.zeros((db, rows, HEAD_DIM), F32))
    state = tuple(_sb_sample_part(*args, *state, pp, n_pages, pp, True))
    rest = n_pages - pp
    if rest > 0:
        state = lax.cond(jnp.max(state[0]) > EXP2_UNDERFLOW,
                         lambda s: tuple(_sb_sample_part(*args, *s, pp, rest, rest, False)), lambda s: s, state)
    return state[1].astype(BF16)


def _rope_tables(pos):
    half = HEAD_DIM // 2
    inv = np.float32(ROPE_THETA) ** (-np.arange(half, dtype=np.float32) / np.float32(half))
    ang = pos.astype(np.float32)[:, None] * inv[None, :].astype(np.float32)
    cos, sin = np.cos(ang), np.sin(ang)
    return jnp.asarray(np.concatenate([cos, cos], axis=1)), jnp.asarray(np.concatenate([-sin, sin], axis=1))


def _pick_tile(n, prefs):
    for t in prefs:
        if n % t == 0:
            return t
    return n


def _rows_to_heads(x, db, t, n_kv):
    return x.reshape(db, t, n_kv, KV_GROUP, HEAD_DIM).transpose(0, 2, 1, 3, 4).reshape(db, n_kv * t * KV_GROUP, HEAD_DIM)


def _heads_to_rows(x, db, t, n_kv):
    return x.reshape(db, n_kv, t, KV_GROUP, HEAD_DIM).transpose(0, 2, 1, 3, 4).reshape(db * t, n_kv * KV_GROUP * HEAD_DIM)


def _chunk_t(v, nb, s_len, ck):
    return v.reshape(nb, s_len // ck, ck, v.shape[1]).transpose(0, 1, 3, 2)


def _pad_page(x, page):
    return jnp.pad(x, [(0, 0), (0, page - x.shape[1])] + [(0, 0)] * (x.ndim - 2))


def kernel(x_prompt, x_sample, c_prompt, c_sample, cache_k, cache_v, cache_idx_k, page_table, norm_mix_g, norm_ffn_g,
           w_ada, b_ada, w_in_dsa, w_in_sb, w_out, w_gate, w_up, w_down, norm_final_g):
    nb, s_len, d = x_prompt.shape
    db, t_new, _ = x_sample.shape
    depth = w_ada.shape[0]
    page, n_kv = cache_k.shape[2], cache_k.shape[3]
    n_pages = page_table.shape[1]
    past_len = n_pages * page
    attn_w = w_out.shape[1]
    kv_w = n_kv * HEAD_DIM
    idx_w = IDX_HEADS * IDX_DIM
    mp, ms = nb * s_len, db * t_new

    tm = _pick_tile(s_len, (1024, 512, 256, 128))
    tm2 = _pick_tile(s_len, (512, 256, 128))
    tm3 = _pick_tile(s_len, (256, 128))
    w_out16 = _cast_bf16(w_out, _pick_tile(w_out.shape[1], (512, 256, 128)))
    tq_dsa, tq_sb, ck = 256, 256, 256
    ng = _pick_tile(n_kv, (4, 2, 1))
    pp = _pick_tile(n_pages, (8, 4, 2, 1))
    attn_scale = HEAD_DIM ** -0.5 * LOG2E

    n_c = nb + db
    c_all = jnp.pad(jnp.concatenate([c_prompt, c_sample], axis=0), ((0, (-n_c) % 16), (0, 0)))
    mod = _adaln(c_all, w_ada, b_ada).reshape(depth, c_all.shape[0], 6, d)

    cos_p, sin_p = _rope_tables(np.arange(s_len))
    cos_s, sin_s = _rope_tables(np.tile(past_len + np.arange(t_new), db))

    xp = x_prompt.reshape(mp, d)
    xs = x_sample.reshape(ms, d)
    outs = {n: [] for n in ("ikp", "ks", "vs", "iks")}
    k_all = v_all = None

    for l in range(depth):
        mp_l = [mod[l, :nb, j].reshape(nb, 1, d) for j in range(6)]
        ms_l = [jnp.repeat(mod[l, nb:n_c, j], t_new, axis=0) for j in range(6)]
        hp = _norm(xp, norm_mix_g[l], mp_l[1], mp_l[0], tm, s_len)
        hs = _norm(xs, norm_mix_g[l], ms_l[1], ms_l[0], ms, 1)
        i = l // 2
        if l % 2 == 0:
            w = jnp.swapaxes(w_in_dsa, 1, 2)
            tail0 = attn_w + 2 * kv_w + idx_w
            w_tail = jnp.pad(w[i, tail0:, :], ((0, 2 * LANES - IDX_DIM - IDX_HEADS), (0, 0)))[None]
            tn = 1024
            hd = tn // HEAD_DIM
            (q,) = _proj(hp, w, i, 0, attn_w, cos_p, sin_p, hd, (BF16,), tm2, tn, attn_scale, w_rows=True)
            k_all, k16 = _proj(hp, w, i, attn_w, kv_w, cos_p, sin_p, hd, (F32, BF16), tm2, tn, w_rows=True,
                               stack=(k_all, l, depth))
            v_all, v16 = _proj(hp, w, i, attn_w + kv_w, kv_w, cos_p, sin_p, 0, (F32, BF16), tm2, tn, w_rows=True,
                               stack=(v_all, l, depth))
            (iq,) = _proj(hp, w, i, attn_w + 2 * kv_w, idx_w, cos_p, sin_p, hd, (F32,), tm2, tn, w_rows=True)
            (ikiw,) = _proj(hp, w_tail, 0, 0, 2 * LANES, cos_p, sin_p, 1, (F32,), tm2, 2 * LANES, w_rows=True)
            mix_p = _dsa_prompt(q, k16, _chunk_t(v16, nb, s_len, ck), iq, ikiw, ikiw[:, IDX_DIM:IDX_DIM + IDX_HEADS].T,
                                nb, s_len, tq_dsa, ck, ng)
            outs["ikp"].append(ikiw[:, :IDX_DIM].reshape(nb, s_len, IDX_DIM))

            (all_s,) = _proj(hs, w, i, 0, tail0, cos_s, sin_s, hd, (F32,), ms, tn, plain_tiles=((attn_w + kv_w) // tn,),
                             w_rows=True)
            (ikiw_s,) = _proj(hs, w_tail, 0, 0, 2 * LANES, cos_s, sin_s, 1, (F32,), ms, 2 * LANES, w_rows=True)
            qs = (all_s[:, :attn_w] * attn_scale).astype(BF16)
            ks32 = all_s[:, attn_w:attn_w + kv_w]
            vs32 = all_s[:, attn_w + kv_w:attn_w + 2 * kv_w]
            iqs = all_s[:, attn_w + 2 * kv_w:]
            iks = ikiw_s[:, :IDX_DIM].reshape(db, t_new, IDX_DIM)
            iws = ikiw_s[:, IDX_DIM:IDX_DIM + IDX_HEADS].reshape(db, t_new, IDX_HEADS)
            iq_rows = jnp.broadcast_to(iqs.reshape(db, t_new, IDX_HEADS, 1, IDX_DIM).transpose(0, 2, 1, 3, 4),
                                       (db, IDX_HEADS, t_new, KV_GROUP, IDX_DIM)).reshape(db, -1, IDX_DIM)
            w_rows = jnp.broadcast_to(iws.transpose(0, 2, 1)[..., None], (db, IDX_HEADS, t_new, KV_GROUP)).reshape(db, -1, 1)
            sc_past, sc_new = _idx_sample(cache_idx_k, i, page_table, iq_rows, w_rows, _pad_page(iks, page),
                                          _pick_tile(n_pages, (16, 8, 4, 2, 1)))
            scores = jnp.concatenate([sc_past, sc_new, jnp.zeros((db, sc_new.shape[1], (pp - 1) * page), F32)], axis=2)
            mask = _select_sample(scores, past_len, t_new)
            ks4 = ks32.reshape(db, t_new, n_kv, HEAD_DIM)
            vs4 = vs32.reshape(db, t_new, n_kv, HEAD_DIM)
            mix_s = _dsa_sample(cache_k, cache_v, l, page_table, _rows_to_heads(qs, db, t_new, n_kv), mask,
                                _pad_page(ks4, page), _pad_page(vs4, page), pp)
            mix_s = _heads_to_rows(mix_s, db, t_new, n_kv)
            outs["iks"].append(iks)
        else:
            w = w_in_sb
            tn = 1024
            (q,) = _proj(hp, w, i, 0, attn_w, cos_p, sin_p, 0, (BF16,), tm2, tn, attn_scale)
            k_all, k16 = _proj(hp, w, i, attn_w, kv_w, cos_p, sin_p, 0, (F32, BF16), tm2, tn, stack=(k_all, l, depth))
            v_all, v16 = _proj(hp, w, i, attn_w + kv_w, kv_w, cos_p, sin_p, 0, (F32, BF16), tm2, tn,
                               stack=(v_all, l, depth))
            mix_p = _sb_prompt(q, k16, _chunk_t(v16, nb, s_len, ck), nb, s_len, tq_sb, ck, ng)
            (all_s,) = _proj(hs, w, i, 0, attn_w + 2 * kv_w, cos_s, sin_s, 0, (F32,), ms, tn)
            qs = (all_s[:, :attn_w] * attn_scale).astype(BF16)
            ks32 = all_s[:, attn_w:attn_w + kv_w]
            vs32 = all_s[:, attn_w + kv_w:]
            ks4 = ks32.reshape(db, t_new, n_kv, HEAD_DIM)
            vs4 = vs32.reshape(db, t_new, n_kv, HEAD_DIM)
            mix_s = _sb_sample(cache_k, cache_v, l, page_table, _rows_to_heads(qs, db, t_new, n_kv),
                               _pad_page(ks4, page), _pad_page(vs4, page), pp)
            mix_s = _heads_to_rows(mix_s, db, t_new, n_kv)
        outs["ks"].append(ks4)
        outs["vs"].append(vs4)

        xp, hp = _out_norm(mix_p, w_out16, l, xp, mp_l[2], norm_ffn_g[l], mp_l[4], mp_l[3], tm3, s_len)
        xs, hs = _out_norm(mix_s, w_out16, l, xs, ms_l[2], norm_ffn_g[l], ms_l[4], ms_l[3], ms, 1)
        tn_ff = _pick_tile(w_gate.shape[2], (512, 256, 128))
        act_p, act_s = _swiglu(hp, w_gate, w_up, l, tm, tn_ff, hs)
        xp, xs = _gres(act_p, w_down, l, xp, mp_l[5], tm2, 512, s_len, act_s, xs, ms_l[5])

    y_prompt = _final_norm(xp, norm_final_g, tm).reshape(nb, s_len, d)
    y_sample = _final_norm(xs, norm_final_g, ms).reshape(db, t_new, d)
    kv_shape = (depth, nb, s_len, n_kv, HEAD_DIM)
    return (y_prompt, y_sample, k_all.reshape(kv_shape), v_all.reshape(kv_shape), jnp.stack(outs["ikp"]),
            jnp.stack(outs["ks"]), jnp.stack(outs["vs"]), jnp.stack(outs["iks"]))
```

```python
import functools

import numpy as np
import jax
import jax.numpy as jnp
from jax import lax
from jax.experimental import pallas as pl
from jax.experimental.pallas import tpu as pltpu

F32 = jnp.float32
BF16 = jnp.bfloat16
I32 = jnp.int32

LANES = 128
HEAD_DIM = 128
KV_GROUP = 2
IDX_HEADS = 16
IDX_DIM = 128
TOPK_MAX = 256
ROPE_THETA = 10000.0
EPS = 1e-6
NEG = -1e30
INT_MIN = -2 ** 31
KEY_NEG_INF = -2139095041
ONES_ROWS = 16
LOG2E = 1.4426950408889634
EXP2_UNDERFLOW = -160.0
VMEM_LIMIT = 56 * 1024 * 1024

_NT = (((1,), (1,)), ((), ()))


def _cparams(sem):
    return pltpu.CompilerParams(dimension_semantics=sem, vmem_limit_bytes=VMEM_LIMIT)


def _split_bf16(x):
    hi = x.astype(BF16)
    lo = (x - hi.astype(F32)).astype(BF16)
    return hi, lo


def _sortable(x):
    bits = lax.bitcast_convert_type(x, I32)
    return bits ^ ((bits >> 31) & 0x7FFFFFFF)


def _adaln_kernel(c_ref, w_ref, b_ref, o_ref):
    c = c_ref[...]
    s = (c * jax.nn.sigmoid(c)).astype(BF16)
    o_ref[...] = jnp.dot(s, w_ref[...].astype(BF16), preferred_element_type=F32) + b_ref[...]


def _adaln(c_pad, w_ada, b_ada):
    depth, d, n = w_ada.shape
    r = c_pad.shape[0]
    tn = 1024
    return pl.pallas_call(
        _adaln_kernel,
        grid=(depth, n // tn),
        in_specs=[pl.BlockSpec((r, d), lambda l, j: (0, 0)),
                  pl.BlockSpec((None, d, tn), lambda l, j: (l, 0, j)),
                  pl.BlockSpec((None, 1, tn), lambda l, j: (l, 0, j))],
        out_specs=pl.BlockSpec((None, r, tn), lambda l, j: (l, 0, j)),
        out_shape=jax.ShapeDtypeStruct((depth, r, n), F32),
        compiler_params=_cparams(("arbitrary", "arbitrary")),
        name="adaln",
    )(c_pad, w_ada, b_ada.reshape(depth, 1, n))


def _norm_kernel(x_ref, g_ref, sc_ref, sh_ref, o_ref):
    x = x_ref[...]
    y = x * lax.rsqrt(jnp.mean(x * x, axis=-1, keepdims=True) + EPS) * g_ref[...]
    o_ref[...] = (y * (1.0 + sc_ref[...]) + sh_ref[...]).astype(o_ref.dtype)


def _final_norm_kernel(x_ref, g_ref, o_ref):
    x = x_ref[...]
    o_ref[...] = x * lax.rsqrt(jnp.mean(x * x, axis=-1, keepdims=True) + EPS) * g_ref[...]


def _mod_spec(mod, tm, width, rows_per_batch, col_of):
    if mod.ndim == 3:
        return pl.BlockSpec((None, 1, width), lambda *ij: ((ij[0] if col_of is None else ij[1]) * tm // rows_per_batch, 0,
                                                           0 if col_of is None else ij[0]))
    return pl.BlockSpec((tm, width), lambda *ij: ((ij[0] if col_of is None else ij[1]), 0 if col_of is None else ij[0]))


def _norm(x, g, sc, sh, tm, rows_per_batch):
    m, d = x.shape
    return pl.pallas_call(
        _norm_kernel,
        grid=(m // tm,),
        in_specs=[pl.BlockSpec((tm, d), lambda i: (i, 0)),
                  pl.BlockSpec((1, d), lambda i: (0, 0)),
                  _mod_spec(sc, tm, d, rows_per_batch, None),
                  _mod_spec(sh, tm, d, rows_per_batch, None)],
        out_specs=pl.BlockSpec((tm, d), lambda i: (i, 0)),
        out_shape=jax.ShapeDtypeStruct((m, d), BF16),
        compiler_params=_cparams(("arbitrary",)),
        name="norm_mod",
    )(x, g.reshape(1, d), sc, sh)


def _final_norm(x, g, tm):
    m, d = x.shape
    return pl.pallas_call(
        _final_norm_kernel,
        grid=(m // tm,),
        in_specs=[pl.BlockSpec((tm, d), lambda i: (i, 0)), pl.BlockSpec((1, d), lambda i: (0, 0))],
        out_specs=pl.BlockSpec((tm, d), lambda i: (i, 0)),
        out_shape=jax.ShapeDtypeStruct((m, d), F32),
        compiler_params=_cparams(("arbitrary",)),
        name="final_norm",
    )(x, g.reshape(1, d))


def _proj_kernel(h_ref, w_ref, cos_ref, sin_ref, *rest, rope, plain_tiles, n_out, scale, w_rows, n_skip):
    rest = rest[n_skip:]
    outs, wb_ref = rest[:n_out], rest[n_out]

    @pl.when(pl.program_id(1) == 0)
    def _():
        wb_ref[...] = w_ref[...].astype(BF16)

    if w_rows:
        acc = lax.dot_general(h_ref[...], wb_ref[...], _NT, preferred_element_type=F32)
    else:
        acc = jnp.dot(h_ref[...], wb_ref[...], preferred_element_type=F32)
    if scale != 1.0:
        acc = acc * scale
    tn = acc.shape[1]

    def emit(n_rope):
        if not n_rope:
            for o in outs:
                o[...] = acc.astype(o.dtype)
            return
        cos = cos_ref[...]
        sin = sin_ref[...]
        for c in range(tn // HEAD_DIM):
            y = acc[:, c * HEAD_DIM:(c + 1) * HEAD_DIM]
            if c < n_rope:
                y = y * cos + pltpu.roll(y, HEAD_DIM // 2, 1) * sin
            for o in outs:
                o[:, c * HEAD_DIM:(c + 1) * HEAD_DIM] = y.astype(o.dtype)

    if rope and plain_tiles:
        j = pl.program_id(0)
        plain = functools.reduce(jnp.logical_or, [j == t for t in plain_tiles])
        pl.when(plain)(lambda: emit(0))
        pl.when(jnp.logical_not(plain))(lambda: emit(rope))
    else:
        emit(rope)


def _proj(h, w, layer, col0, ncols, cos, sin, rope, out_dtypes, tm, tn, scale=1.0, plain_tiles=(), w_rows=False,
          stack=None):
    m, k = h.shape
    npb = cos.shape[0] // tm
    j0 = col0 // tn
    n_out = len(out_dtypes)
    if w_rows:
        w_spec = pl.BlockSpec((None, tn, k), lambda j, i: (layer, j + j0, 0))
    else:
        w_spec = pl.BlockSpec((None, k, tn), lambda j, i: (layer, 0, j + j0))
    in_specs = [pl.BlockSpec((tm, k), lambda j, i: (i, 0)),
                w_spec,
                pl.BlockSpec((tm, HEAD_DIM), lambda j, i: (i % npb, 0)),
                pl.BlockSpec((tm, HEAD_DIM), lambda j, i: (i % npb, 0))]
    out_specs = [pl.BlockSpec((tm, tn), lambda j, i: (i, j)) for _ in out_dtypes]
    out_shape = [jax.ShapeDtypeStruct((m, ncols), dt) for dt in out_dtypes]
    args, aliases = [h, w, cos, sin], {}
    if stack is not None:
        buf, slot = stack
        out_specs[0] = pl.BlockSpec((None, tm, tn), lambda j, i: (slot, i, j))
        out_shape[0] = jax.ShapeDtypeStruct(buf.shape, buf.dtype)
        in_specs.append(pl.BlockSpec(memory_space=pl.ANY))
        args.append(buf)
        aliases = {len(args) - 1: 0}
    return pl.pallas_call(
        functools.partial(_proj_kernel, rope=rope, plain_tiles=tuple(plain_tiles), n_out=n_out, scale=scale,
                          w_rows=w_rows, n_skip=len(aliases)),
        grid=(ncols // tn, m // tm),
        in_specs=in_specs,
        out_specs=out_specs,
        out_shape=out_shape,
        scratch_shapes=[pltpu.VMEM((tn, k) if w_rows else (k, tn), BF16)],
        input_output_aliases=aliases,
        compiler_params=_cparams(("arbitrary", "arbitrary")),
        name="proj_rope" if rope else "proj",
    )(*args)


def _gres_kernel(h_ref, w_ref, x_ref, g_ref, hs_ref, xs_ref, gs_ref, o_ref, os_ref, wb_ref):
    @pl.when(pl.program_id(1) == 0)
    def _():
        wb_ref[...] = w_ref[...].astype(BF16)
        os_ref[...] = xs_ref[...] + gs_ref[...] * jnp.dot(hs_ref[...], wb_ref[...], preferred_element_type=F32)

    acc = jnp.dot(h_ref[...], wb_ref[...], preferred_element_type=F32)
    o_ref[...] = x_ref[...] + g_ref[...] * acc


def _gres(h, w, layer, x, gate, tm, tn, rows_per_batch, h_s, x_s, gate_s):
    m, k = h.shape
    n = w.shape[2]
    m_s = h_s.shape[0]
    return pl.pallas_call(
        _gres_kernel,
        grid=(n // tn, m // tm),
        in_specs=[pl.BlockSpec((tm, k), lambda j, i: (i, 0)),
                  pl.BlockSpec((None, k, tn), lambda j, i: (layer, 0, j)),
                  pl.BlockSpec((tm, tn), lambda j, i: (i, j)),
                  _mod_spec(gate, tm, tn, rows_per_batch, True),
                  pl.BlockSpec((m_s, k), lambda j, i: (0, 0)),
                  pl.BlockSpec((m_s, tn), lambda j, i: (0, j)),
                  pl.BlockSpec((m_s, tn), lambda j, i: (0, j))],
        out_specs=[pl.BlockSpec((tm, tn), lambda j, i: (i, j)), pl.BlockSpec((m_s, tn), lambda j, i: (0, j))],
        out_shape=[jax.ShapeDtypeStruct((m, n), F32), jax.ShapeDtypeStruct((m_s, n), F32)],
        scratch_shapes=[pltpu.VMEM((k, tn), BF16)],
        compiler_params=_cparams(("arbitrary", "arbitrary")),
        name="gated_residual",
    )(h, w, x, gate, h_s, x_s, gate_s)


def _cast_kernel(w_ref, o_ref):
    o_ref[...] = w_ref[...].astype(o_ref.dtype)


def _cast_bf16(w, tk):
    depth, k, n = w.shape
    return pl.pallas_call(
        _cast_kernel,
        grid=(depth, k // tk),
        in_specs=[pl.BlockSpec((None, tk, n), lambda l, i: (l, i, 0))],
        out_specs=pl.BlockSpec((None, tk, n), lambda l, i: (l, i, 0)),
        out_shape=jax.ShapeDtypeStruct(w.shape, BF16),
        compiler_params=_cparams(("arbitrary", "arbitrary")),
        name="cast_bf16",
    )(w)


def _out_norm_kernel(m_ref, w_ref, x_ref, gate_ref, g_ref, sc_ref, sh_ref, xo_ref, ho_ref):
    x = x_ref[...] + gate_ref[...] * jnp.dot(m_ref[...], w_ref[...], preferred_element_type=F32)
    xo_ref[...] = x
    y = x * lax.rsqrt(jnp.mean(x * x, axis=-1, keepdims=True) + EPS) * g_ref[...]
    ho_ref[...] = (y * (1.0 + sc_ref[...]) + sh_ref[...]).astype(ho_ref.dtype)


def _out_norm(mix, wb, layer, x, gate, g, sc, sh, tm, rows_per_batch):
    m, k = mix.shape
    d = wb.shape[2]
    row = lambda i: (i, 0)
    return pl.pallas_call(
        _out_norm_kernel,
        grid=(m // tm,),
        in_specs=[pl.BlockSpec((tm, k), row),
                  pl.BlockSpec((None, k, d), lambda i: (layer, 0, 0)),
                  pl.BlockSpec((tm, d), row),
                  _mod_spec(gate, tm, d, rows_per_batch, None),
                  pl.BlockSpec((1, d), lambda i: (0, 0)),
                  _mod_spec(sc, tm, d, rows_per_batch, None),
                  _mod_spec(sh, tm, d, rows_per_batch, None)],
        out_specs=[pl.BlockSpec((tm, d), row), pl.BlockSpec((tm, d), row)],
        out_shape=[jax.ShapeDtypeStruct((m, d), F32), jax.ShapeDtypeStruct((m, d), BF16)],
        compiler_params=_cparams(("arbitrary",)),
        name="out_proj_norm",
    )(mix, wb, x, gate, g.reshape(1, d), sc, sh)


def _swiglu_kernel(h_ref, wg_ref, wu_ref, hs_ref, o_ref, os_ref, wgb_ref, wub_ref):
    def act(h):
        a = jnp.dot(h, wgb_ref[...], preferred_element_type=F32)
        b = jnp.dot(h, wub_ref[...], preferred_element_type=F32)
        return (a * jax.nn.sigmoid(a) * b).astype(BF16)

    @pl.when(pl.program_id(1) == 0)
    def _():
        wgb_ref[...] = wg_ref[...].astype(BF16)
        wub_ref[...] = wu_ref[...].astype(BF16)
        os_ref[...] = act(hs_ref[...])

    o_ref[...] = act(h_ref[...])


def _swiglu(h, wg, wu, layer, tm, tn, h_s):
    m, k = h.shape
    n = wg.shape[2]
    m_s = h_s.shape[0]
    return pl.pallas_call(
        _swiglu_kernel,
        grid=(n // tn, m // tm),
        in_specs=[pl.BlockSpec((tm, k), lambda j, i: (i, 0)),
                  pl.BlockSpec((None, k, tn), lambda j, i: (layer, 0, j)),
                  pl.BlockSpec((None, k, tn), lambda j, i: (layer, 0, j)),
                  pl.BlockSpec((m_s, k), lambda j, i: (0, 0))],
        out_specs=[pl.BlockSpec((tm, tn), lambda j, i: (i, j)), pl.BlockSpec((m_s, tn), lambda j, i: (0, j))],
        out_shape=[jax.ShapeDtypeStruct((m, n), BF16), jax.ShapeDtypeStruct((m_s, n), BF16)],
        scratch_shapes=[pltpu.VMEM((k, tn), BF16), pltpu.VMEM((k, tn), BF16)],
        compiler_params=_cparams(("arbitrary", "arbitrary")),
        name="swiglu",
    )(h, wg, wu, h_s)


def _kth_threshold(count_ge, topk, shape):
    t0 = jnp.where(count_ge(jnp.zeros(shape, I32)) >= topk, 0, INT_MIN).astype(I32)

    def body(it, t):
        cand = t | jnp.left_shift(jnp.int32(1), 30 - it)
        return jnp.where(count_ge(cand) >= topk, cand, t)

    return lax.fori_loop(0, 31, body, t0)


def _tie_cutoff(count_tie_below, need, nbits, shape):
    def body(it, a):
        cand = a | jnp.left_shift(jnp.int32(1), nbits - 1 - it)
        return jnp.where(count_tie_below(cand) < need, cand, a)

    return lax.fori_loop(0, nbits, body, jnp.zeros(shape, I32))


def _dsa_prompt_kernel(q_ref, k_ref, vt_ref, iq_ref, ik_ref, iwt_ref, o_ref,
                       iqcat_ref, keys_ref, bias_ref, cut_ref, m_ref, acc_ref, q2_ref,
                       *, tq, ck, ng, topk, nbits, idx_scale):
    i = pl.program_id(1)
    nkc = ((i + 1) * tq + ck - 1) // ck
    qpos = i * tq + lax.broadcasted_iota(I32, (1, tq), 1)

    def kpos_of(c):
        return c * ck + lax.broadcasted_iota(I32, (ck, tq), 0)

    @pl.when(pl.program_id(2) == 0)
    def _index():
        for h in range(IDX_HEADS):
            hi, lo = _split_bf16(iq_ref[:, h * IDX_DIM:(h + 1) * IDX_DIM])
            iqcat_ref[:, h * 2 * IDX_DIM:h * 2 * IDX_DIM + IDX_DIM] = hi
            iqcat_ref[:, h * 2 * IDX_DIM + IDX_DIM:(h + 1) * 2 * IDX_DIM] = lo

        def score_chunk(c, carry):
            ikh = ik_ref[pl.ds(pl.multiple_of(c * ck, ck), ck), :].astype(BF16)
            ikcat = jnp.concatenate([ikh, ikh], axis=1)
            acc = jnp.zeros((ck, tq), F32)
            for h in range(IDX_HEADS):
                lg = lax.dot_general(ikcat, iqcat_ref[:, h * 2 * IDX_DIM:(h + 1) * 2 * IDX_DIM], _NT,
                                     preferred_element_type=F32)
                acc = acc + iwt_ref[h:h + 1, :] * jnp.maximum(lg, 0.0)
            sc = jnp.where(kpos_of(c) <= qpos, acc * idx_scale, -jnp.inf)
            keys_ref[c] = _sortable(sc)
            return carry

        lax.fori_loop(0, nkc, score_chunk, 0)

        def count(pred):
            def body(c, part):
                m = jnp.where(pred(keys_ref[c], c), 1.0, 0.0)
                return part + jnp.sum(m.reshape(ck // 32, 32, tq), axis=0)
            part = lax.fori_loop(0, nkc, body, jnp.zeros((32, tq), F32))
            return jnp.sum(part, axis=0, keepdims=True)

        thr = _kth_threshold(lambda cand: count(lambda kc, c: kc >= cand), float(topk), (1, tq))
        n_gt = count(lambda kc, c: kc > thr)
        n_ge = count(lambda kc, c: kc >= thr)
        need = float(topk) - n_gt
        crowded = (n_ge > float(topk)) & (thr > KEY_NEG_INF)
        cut_ref[...] = jnp.full((1, tq), 2 ** 30, I32)

        @pl.when(jnp.max(jnp.where(crowded, 1.0, 0.0)) > 0.5)
        def _():
            cut_ref[...] = _tie_cutoff(lambda cand: count(lambda kc, c: (kc == thr) & (kpos_of(c) < cand)),
                                       need, nbits, (1, tq))

        cut = cut_ref[...]

        def bias_chunk(c, carry):
            kc = keys_ref[c]
            kpos = kpos_of(c)
            sel = ((kc > thr) | ((kc == thr) & (kpos <= cut))) & (kpos <= qpos)
            bias_ref[c] = jnp.where(sel, 0.0, NEG).astype(BF16)
            return carry

        lax.fori_loop(0, nkc, bias_chunk, 0)

    m_ref[...] = jnp.full(m_ref.shape, NEG, F32)
    acc_ref[...] = jnp.zeros(acc_ref.shape, F32)
    for g in range(ng):
        for j in range(KV_GROUP):
            h = g * KV_GROUP + j
            q2_ref[g, j * tq:(j + 1) * tq, :] = q_ref[:, h * HEAD_DIM:(h + 1) * HEAD_DIM]
    ones = jnp.ones((ONES_ROWS, ck), BF16)

    def attend(c, carry):
        off = pl.multiple_of(c * ck, ck)
        b = bias_ref[c]
        b2 = jnp.concatenate([b] * KV_GROUP, axis=1)

        def scores(g):
            s = lax.dot_general(k_ref[pl.ds(off, ck), g * HEAD_DIM:(g + 1) * HEAD_DIM], q2_ref[g], _NT,
                                preferred_element_type=F32)
            return s.astype(BF16) + b2

        def softmax(g, s):
            m = m_ref[g]
            m_new = jnp.maximum(m, jnp.max(s, axis=0, keepdims=True).astype(F32))
            m_ref[g] = m_new
            return jnp.exp2(s - m_new.astype(BF16)), jnp.exp2(m - m_new)

        def weigh(g, p, alpha):
            vt = jnp.concatenate([vt_ref[c, g * HEAD_DIM:(g + 1) * HEAD_DIM, :], ones], axis=0)
            acc_ref[g] = alpha * acc_ref[g] + jnp.dot(vt, p, preferred_element_type=F32)

        ahead = 2
        s = {g: scores(g) for g in range(min(ahead, ng))}
        for g in range(ng):
            p, alpha = softmax(g, s.pop(g))
            if g + ahead < ng:
                s[g + ahead] = scores(g + ahead)
            weigh(g, p, alpha)
        return carry

    lax.fori_loop(0, nkc, attend, 0)
    for g in range(ng):
        out = acc_ref[g, :HEAD_DIM] * (1.0 / acc_ref[g, HEAD_DIM:HEAD_DIM + 1])
        for j in range(KV_GROUP):
            h = g * KV_GROUP + j
            o_ref[:, h * HEAD_DIM:(h + 1) * HEAD_DIM] = out[:, j * tq:(j + 1) * tq].T.astype(o_ref.dtype)


def _dsa_prompt(q, k, vt, iq, ikiw, iwt, nb, s_len, tq, ck, ng):
    n_kv = k.shape[1] // HEAD_DIM
    nq = s_len // tq
    topk = min(TOPK_MAX, s_len // 4)
    nh = ng * KV_GROUP
    kern = functools.partial(_dsa_prompt_kernel, tq=tq, ck=ck, ng=ng, topk=topk, nbits=max(1, (s_len - 1).bit_length()),
                             idx_scale=(IDX_HEADS ** -0.5) * (IDX_DIM ** -0.5))
    return pl.pallas_call(
        kern,
        grid=(nb, nq, n_kv // ng),
        in_specs=[pl.BlockSpec((tq, nh * HEAD_DIM), lambda b, i, g: (b * nq + i, g)),
                  pl.BlockSpec((s_len, ng * HEAD_DIM), lambda b, i, g: (b, g)),
                  pl.BlockSpec((None, s_len // ck, ng * HEAD_DIM, ck), lambda b, i, g: (b, 0, g, 0)),
                  pl.BlockSpec((tq, IDX_HEADS * IDX_DIM), lambda b, i, g: (b * nq + i, 0)),
                  pl.BlockSpec((s_len, IDX_DIM), lambda b, i, g: (b, 0)),
                  pl.BlockSpec((IDX_HEADS, tq), lambda b, i, g: (0, b * nq + i))],
        out_specs=pl.BlockSpec((tq, nh * HEAD_DIM), lambda b, i, g: (b * nq + i, g)),
        out_shape=jax.ShapeDtypeStruct(q.shape, BF16),
        scratch_shapes=[pltpu.VMEM((tq, IDX_HEADS * 2 * IDX_DIM), BF16),
                        pltpu.VMEM((s_len // ck, ck, tq), I32),
                        pltpu.VMEM((s_len // ck, ck, tq), BF16),
                        pltpu.VMEM((1, tq), I32),
                        pltpu.VMEM((ng, 1, KV_GROUP * tq), F32),
                        pltpu.VMEM((ng, HEAD_DIM + ONES_ROWS, KV_GROUP * tq), F32),
                        pltpu.VMEM((ng, KV_GROUP * tq, HEAD_DIM), BF16)],
        compiler_params=_cparams(("arbitrary", "arbitrary", "arbitrary")),
        name="dsa_prompt",
    )(q, k, vt, iq, ikiw, iwt)


def _softplus2(z):
    return jnp.maximum(z, 0.0) + jnp.log2(1.0 + jnp.exp2(-jnp.abs(z)))


def _sb_prompt_kernel(q_ref, k_ref, vt_ref, ln_ref, o_ref, tail_ref, acc_ref, *, tq, ck, ng):
    i = pl.program_id(1)
    nkc = ((i + 1) * tq + ck - 1) // ck
    nh = ng * KV_GROUP
    qpos = i * tq + lax.broadcasted_iota(I32, (1, tq), 1)
    tail_ref[...] = jnp.zeros(tail_ref.shape, F32)
    acc_ref[...] = jnp.zeros(acc_ref.shape, F32)

    def chunk(c, masked):
        off = pl.multiple_of(c * ck, ck)
        mask = (c * ck + lax.broadcasted_iota(I32, (ck, tq), 0)) < qpos if masked else None

        def logits(h):
            g = h // KV_GROUP
            return lax.dot_general(k_ref[pl.ds(off, ck), g * HEAD_DIM:(g + 1) * HEAD_DIM],
                                   q_ref[:, h * HEAD_DIM:(h + 1) * HEAD_DIM], _NT, preferred_element_type=F32)

        def keep(h, z):
            sp = jnp.where(z > 64.0, z, jnp.log2(1.0 + jnp.exp2(z)))
            spm = jnp.where(mask, sp, 0.0) if masked else sp
            la = jnp.dot(ln_ref[...], spm.astype(BF16), preferred_element_type=F32)
            tail = tail_ref[h]
            tail_ref[h] = tail + la[ck:ck + 1]
            return z - sp + tail, la

        def weigh(h, base, la):
            g = h // KV_GROUP
            a = jnp.exp2(base + la[:ck])
            if masked:
                a = jnp.where(mask, a, 0.0)
            acc_ref[h] += jnp.dot(vt_ref[c, g * HEAD_DIM:(g + 1) * HEAD_DIM, :], a.astype(BF16),
                                  preferred_element_type=F32)

        a1, a2 = 2, 4
        z = {h: logits(h) for h in range(min(a2, nh))}
        kept = {h: keep(h, z.pop(h)) for h in range(min(a1, nh))}
        for h in range(nh):
            if h + a1 < nh:
                kept[h + a1] = keep(h + a1, z.pop(h + a1))
            if h + a2 < nh:
                z[h + a2] = logits(h + a2)
            weigh(h, *kept.pop(h))

    chunk(nkc - 1, True)

    def live(carry):
        r, top = carry
        return (r < nkc) & (top > EXP2_UNDERFLOW)

    def body(carry):
        r, _ = carry
        chunk(nkc - 1 - r, False)
        return r + 1, jnp.max(tail_ref[...])

    lax.while_loop(live, body, (jnp.int32(1), jnp.max(tail_ref[...])))
    for h in range(nh):
        o_ref[:, h * HEAD_DIM:(h + 1) * HEAD_DIM] = acc_ref[h].T.astype(o_ref.dtype)


def _suffix_matrix(n):
    return jnp.asarray(np.tril(np.ones((n, n), np.float32), -1), dtype=BF16)


def _sb_prompt(q, k, vt, nb, s_len, tq, ck, ng):
    assert ck % tq == 0
    n_kv = k.shape[1] // HEAD_DIM
    nq = s_len // tq
    gw = KV_GROUP * HEAD_DIM
    ln = jnp.concatenate([-_suffix_matrix(ck).T, -jnp.ones((8, ck), BF16)], axis=0)
    return pl.pallas_call(
        functools.partial(_sb_prompt_kernel, tq=tq, ck=ck, ng=ng),
        grid=(nb, nq, n_kv // ng),
        in_specs=[pl.BlockSpec((tq, ng * gw), lambda b, i, g: (b * nq + i, g)),
                  pl.BlockSpec((s_len, ng * HEAD_DIM), lambda b, i, g: (b, g)),
                  pl.BlockSpec((None, s_len // ck, ng * HEAD_DIM, ck), lambda b, i, g: (b, 0, g, 0)),
                  pl.BlockSpec((ck + 8, ck), lambda b, i, g: (0, 0))],
        out_specs=pl.BlockSpec((tq, ng * gw), lambda b, i, g: (b * nq + i, g)),
        out_shape=jax.ShapeDtypeStruct(q.shape, BF16),
        scratch_shapes=[pltpu.VMEM((ng * KV_GROUP, 1, tq), F32),
                        pltpu.VMEM((ng * KV_GROUP, HEAD_DIM, tq), F32)],
        compiler_params=_cparams(("arbitrary", "arbitrary", "arbitrary")),
        name="sb_prompt",
    )(q, k, vt, ln)


def _page_consts(page, n_kv, rows):
    lane = np.arange(page * n_kv)
    expand = (lane[None, :] // n_kv == np.arange(page)[:, None]).astype(np.float32)
    valid = (lane[None, :] % n_kv == (np.arange(rows)[:, None] // (rows // n_kv))).astype(np.float32)
    return jnp.asarray(expand, BF16), jnp.asarray(expand.T.copy(), BF16), jnp.asarray(valid, F32)


def _idx_sample_kernel(pt_ref, iq_ref, w_ref, ikn_ref, *rest, pp, idx_scale):
    pages, (past_ref, new_ref) = rest[:pp], rest[pp:]
    hi, lo = _split_bf16(iq_ref[...])
    iqcat = jnp.concatenate([hi, lo], axis=1)
    w = w_ref[...]
    rows = iqcat.shape[0]

    def logits(ik):
        ikh = ik.astype(BF16)
        return lax.dot_general(iqcat, jnp.concatenate([ikh, ikh], axis=1), _NT, preferred_element_type=F32)

    def score(lg):
        x = jnp.maximum(lg, 0.0) * w
        return jnp.sum(x.reshape(IDX_HEADS, rows // IDX_HEADS, x.shape[1]), axis=0) * idx_scale

    lgs = [logits(pages[j][...]) for j in range(pp)]
    for j in range(pp):
        past_ref[:, j * LANES:(j + 1) * LANES] = score(lgs[j])

    @pl.when(pl.program_id(1) == 0)
    def _():
        new_ref[...] = score(logits(ikn_ref[...]))


def _idx_sample(cache_idx_k, layer, page_table, iq_rows, w_rows, ik_new_pad, pp):
    db, n_pages = page_table.shape
    page = cache_idx_k.shape[2]
    rows = iq_rows.shape[1]
    r8 = rows // IDX_HEADS
    page_specs = [pl.BlockSpec((None, None, page, IDX_DIM),
                               functools.partial(lambda b, p, pt, j: (layer, pt[b, p * pp + j], 0, 0), j=j))
                  for j in range(pp)]
    return pl.pallas_call(
        functools.partial(_idx_sample_kernel, pp=pp, idx_scale=(IDX_HEADS ** -0.5) * (IDX_DIM ** -0.5)),
        grid_spec=pltpu.PrefetchScalarGridSpec(
            num_scalar_prefetch=1,
            grid=(db, n_pages // pp),
            in_specs=[pl.BlockSpec((None, rows, IDX_DIM), lambda b, p, pt: (b, 0, 0)),
                      pl.BlockSpec((None, rows, 1), lambda b, p, pt: (b, 0, 0)),
                      pl.BlockSpec((None, page, IDX_DIM), lambda b, p, pt: (b, 0, 0))] + page_specs,
            out_specs=[pl.BlockSpec((None, r8, pp * page), lambda b, p, pt: (b, 0, p)),
                       pl.BlockSpec((None, r8, page), lambda b, p, pt: (b, 0, 0))]),
        out_shape=[jax.ShapeDtypeStruct((db, r8, n_pages * page), F32),
                   jax.ShapeDtypeStruct((db, r8, page), F32)],
        compiler_params=_cparams(("arbitrary", "arbitrary")),
        name="idx_sample",
    )(page_table, iq_rows, w_rows, ik_new_pad, *([cache_idx_k] * pp))


def _select_sample_kernel(s_ref, o_ref, cut_ref, *, topk, nbits, past_len):
    sc = s_ref[...]
    rows, length = sc.shape
    qpos = past_len + lax.broadcasted_iota(I32, (rows, 1), 0) // KV_GROUP
    kpos = lax.broadcasted_iota(I32, (rows, length), 1)
    causal = kpos <= qpos
    key = _sortable(jnp.where(causal, sc, -jnp.inf))

    n_acc = 4

    def count(pred_of):
        accs = [jnp.zeros((rows, LANES), F32) for _ in range(n_acc)]
        for j in range(length // LANES):
            sl = slice(j * LANES, (j + 1) * LANES)
            accs[j % n_acc] = accs[j % n_acc] + jnp.where(pred_of(key[:, sl], kpos[:, sl]), 1.0, 0.0)
        return jnp.sum((accs[0] + accs[1]) + (accs[2] + accs[3]), axis=1, keepdims=True)

    thr = _kth_threshold(lambda cand: count(lambda k, p: k >= cand), float(topk), (rows, 1))
    need = float(topk) - count(lambda k, p: k > thr)
    crowded = (count(lambda k, p: k >= thr) > float(topk)) & (thr > KEY_NEG_INF)
    cut_ref[...] = jnp.full((rows, 1), 2 ** 30, I32)

    @pl.when(jnp.max(jnp.where(crowded, 1.0, 0.0)) > 0.5)
    def _():
        cut_ref[...] = _tie_cutoff(lambda cand: count(lambda k, p: (k == thr) & (p < cand)), need, nbits, (rows, 1))

    sel = ((key > thr) | ((key == thr) & (kpos <= cut_ref[...]))) & causal
    o_ref[...] = jnp.where(sel, 1.0, 0.0)


def _select_sample(scores, past_len, n_new):
    db, rows, length = scores.shape
    topk = min(TOPK_MAX, (past_len + n_new) // 4)
    return pl.pallas_call(
        functools.partial(_select_sample_kernel, topk=topk, nbits=max(1, (length - 1).bit_length()), past_len=past_len),
        grid=(db,),
        in_specs=[pl.BlockSpec((None, rows, length), lambda b: (b, 0, 0))],
        out_specs=pl.BlockSpec((None, rows, length), lambda b: (b, 0, 0)),
        out_shape=jax.ShapeDtypeStruct(scores.shape, F32),
        scratch_shapes=[pltpu.VMEM((rows, 1), I32)],
        compiler_params=_cparams(("arbitrary",)),
        name="select_sample",
    )(scores)


def _flat_bf16(ref):
    x = ref[...]
    return x.reshape(x.shape[0] * x.shape[1], x.shape[2]).astype(BF16)


def _dsa_sample_kernel(pt_ref, q_ref, mask_ref, e_ref, valid_ref, kn_ref, vn_ref, *rest,
                       pp, n_steps):
    kpages, vpages = rest[:pp], rest[pp:2 * pp]
    o_ref, m_ref, l_ref, acc_ref = rest[2 * pp:]
    p = pl.program_id(1)
    q = q_ref[...]
    rows = q.shape[0]
    reps = rows // mask_ref.shape[0]
    valid = valid_ref[...]

    @pl.when(p == 0)
    def _():
        m_ref[...] = jnp.full(m_ref.shape, NEG, F32)
        l_ref[...] = jnp.zeros(l_ref.shape, F32)
        acc_ref[...] = jnp.zeros(acc_ref.shape, F32)

    def pages(krefs, vrefs):
        ss = []
        for j, kref in enumerate(krefs):
            s = lax.dot_general(q, _flat_bf16(kref), _NT, preferred_element_type=F32)
            mrows = jnp.concatenate([mask_ref[:, j * LANES:(j + 1) * LANES]] * reps, axis=0).astype(BF16)
            keep = jnp.dot(mrows, e_ref[...], preferred_element_type=F32) * valid > 0.5
            ss.append(jnp.where(keep, s, NEG))
        m = m_ref[...]
        m_new = m
        for s in ss:
            m_new = jnp.maximum(m_new, jnp.max(s, axis=1, keepdims=True))
        alpha = jnp.exp2(m - m_new)
        l = alpha * l_ref[...]
        acc = alpha * acc_ref[...]
        for s, vref in zip(ss, vrefs):
            pr = jnp.exp2(s - m_new)
            l = l + jnp.sum(pr, axis=1, keepdims=True)
            acc = acc + jnp.dot(pr.astype(BF16), _flat_bf16(vref), preferred_element_type=F32)
        l_ref[...] = l
        acc_ref[...] = acc
        m_ref[...] = m_new

    @pl.when(p < n_steps - 1)
    def _():
        pages(kpages, vpages)

    @pl.when(p == n_steps - 1)
    def _():
        pages([kn_ref], [vn_ref])
        o_ref[...] = (acc_ref[...] / l_ref[...]).astype(o_ref.dtype)


def _dsa_sample(cache_k, cache_v, layer, page_table, q_rows, mask, k_new_pad, v_new_pad, pp):
    db, n_pages = page_table.shape
    page, n_kv = cache_k.shape[2], cache_k.shape[3]
    rows = q_rows.shape[1]
    n_steps = n_pages // pp + 1
    expand, _, valid = _page_consts(page, n_kv, rows)

    def cache_spec(j):
        return pl.BlockSpec((None, None, page, n_kv, HEAD_DIM),
                            lambda b, p, pt: (layer, pt[b, jnp.minimum(p, n_steps - 2) * pp + j], 0, 0, 0))

    new_spec = pl.BlockSpec((None, page, n_kv, HEAD_DIM), lambda b, p, pt: (b, 0, 0, 0))
    mask_spec = pl.BlockSpec((None, mask.shape[1], pp * page),
                             lambda b, p, pt: (b, 0, jnp.where(p == n_steps - 1, n_pages // pp, p)))
    return pl.pallas_call(
        functools.partial(_dsa_sample_kernel, pp=pp, n_steps=n_steps),
        grid_spec=pltpu.PrefetchScalarGridSpec(
            num_scalar_prefetch=1,
            grid=(db, n_steps),
            in_specs=[pl.BlockSpec((None, rows, HEAD_DIM), lambda b, p, pt: (b, 0, 0)),
                      mask_spec,
                      pl.BlockSpec(expand.shape, lambda b, p, pt: (0, 0)),
                      pl.BlockSpec(valid.shape, lambda b, p, pt: (0, 0)),
                      new_spec, new_spec] + [cache_spec(j) for j in range(pp)] * 2,
            out_specs=pl.BlockSpec((None, rows, HEAD_DIM), lambda b, p, pt: (b, 0, 0)),
            scratch_shapes=[pltpu.VMEM((rows, 1), F32), pltpu.VMEM((rows, 1), F32), pltpu.VMEM((rows, HEAD_DIM), F32)]),
        out_shape=jax.ShapeDtypeStruct((db, rows, HEAD_DIM), BF16),
        compiler_params=_cparams(("arbitrary", "arbitrary")),
        name="dsa_sample",
    )(page_table, q_rows, mask, expand, valid, k_new_pad, v_new_pad, *([cache_k] * pp), *([cache_v] * pp))


def _sb_sample_kernel(pt_ref, q_ref, e_ref, c_ref, valid_ref, u_ref, kn_ref, vn_ref, tail_in_ref, acc_in_ref, *rest,
                      pp, top, with_new, past_len):
    kpages, vpages = rest[:pp], rest[pp:2 * pp]
    tail_ref, acc_ref = rest[2 * pp:]
    p = pl.program_id(1)
    q = q_ref[...]
    rows = q.shape[0]
    n_kv = valid_ref.shape[1] // e_ref.shape[0]
    valid = valid_ref[...]
    page_len = e_ref.shape[0]
    qpos = past_len + (lax.broadcasted_iota(I32, (rows, 1), 0) % (rows // n_kv)) // KV_GROUP

    @pl.when(p == 0)
    def _():
        tail_ref[...] = tail_in_ref[...]
        acc_ref[...] = acc_in_ref[...]

    def pages(krefs, vrefs, starts):
        zfs = [lax.dot_general(q, _flat_bf16(kref), _NT, preferred_element_type=F32) * valid for kref in krefs]
        zs = []
        for zf in zfs:
            hi, lo = _split_bf16(zf)
            zz = jnp.dot(jnp.concatenate([hi, lo], axis=0), c_ref[...], preferred_element_type=F32)
            zs.append(zz[:rows] + zz[rows:])
        parts = []
        tail = tail_ref[...]
        for z, start in zip(zs, starts):
            mask = (start + lax.broadcasted_iota(I32, (rows, page_len), 1)) < qpos
            sp = _softplus2(z)
            lk = jnp.where(mask, -sp, 0.0)
            hi, lo = _split_bf16(lk)
            ll = jnp.dot(jnp.concatenate([hi, lo], axis=0), u_ref[...], preferred_element_type=F32)
            parts.append((mask, z - sp + tail, ll))
            tail = tail + jnp.sum(lk, axis=1, keepdims=True)
        tail_ref[...] = tail
        acc = acc_ref[...]
        for (mask, base, ll), vref in zip(parts, vrefs):
            a = jnp.where(mask, jnp.exp2(base + ll[:rows] + ll[rows:]), 0.0).astype(BF16)
            ae = (jnp.dot(a, e_ref[...], preferred_element_type=F32) * valid).astype(BF16)
            acc = acc + jnp.dot(ae, _flat_bf16(vref), preferred_element_type=F32)
        acc_ref[...] = acc

    if with_new:
        @pl.when(p == 0)
        def _():
            pages([kn_ref], [vn_ref], [past_len])

    @pl.when(p >= int(with_new))
    def _():
        pages(kpages, vpages, [(top - 1 - ((p - int(with_new)) * pp + j)) * page_len for j in range(pp)])


def _sb_sample_part(cache_k, cache_v, layer, page_table, q_rows, k_new_pad, v_new_pad, tail, acc, pp, top, count, with_new):
    db, n_pages = page_table.shape
    page, n_kv = cache_k.shape[2], cache_k.shape[3]
    rows = q_rows.shape[1]
    first = int(with_new)
    expand, compact, valid = _page_consts(page, n_kv, rows)

    def cache_spec(j):
        return pl.BlockSpec((None, None, page, n_kv, HEAD_DIM),
                            lambda b, p, pt: (layer, pt[b, top - 1 - (jnp.maximum(p - first, 0) * pp + j)], 0, 0, 0))

    new_spec = pl.BlockSpec((None, page, n_kv, HEAD_DIM), lambda b, p, pt: (b, 0, 0, 0))
    const = lambda a: pl.BlockSpec(a.shape, lambda b, p, pt: (0, 0))
    per_b = lambda a: pl.BlockSpec((None,) + a.shape[1:], lambda b, p, pt: (b, 0, 0))
    u = _suffix_matrix(page)
    return pl.pallas_call(
        functools.partial(_sb_sample_kernel, pp=pp, top=top, with_new=with_new, past_len=n_pages * page),
        grid_spec=pltpu.PrefetchScalarGridSpec(
            num_scalar_prefetch=1,
            grid=(db, count // pp + first),
            in_specs=[per_b(q_rows), const(expand), const(compact), const(valid), const(u),
                      new_spec, new_spec, per_b(tail), per_b(acc)] + [cache_spec(j) for j in range(pp)] * 2,
            out_specs=[per_b(tail), per_b(acc)]),
        out_shape=[jax.ShapeDtypeStruct(tail.shape, F32), jax.ShapeDtypeStruct(acc.shape, F32)],
        compiler_params=_cparams(("arbitrary", "arbitrary")),
        name="sb_sample",
    )(page_table, q_rows, expand, compact, valid, u, k_new_pad, v_new_pad, tail, acc,
      *([cache_k] * pp), *([cache_v] * pp))


def _sb_sample(cache_k, cache_v, layer, page_table, q_rows, k_new_pad, v_new_pad, pp):
    db, n_pages = page_table.shape
    rows = q_rows.shape[1]
    args = (cache_k, cache_v, layer, page_table, q_rows, k_new_pad, v_new_pad)
    state = (jnp.zeros((db, rows, 1), F32), jnp.zeros((db, rows, HEAD_DIM), F32))
    state = tuple(_sb_sample_part(*args, *state, pp, n_pages, pp, True))
    rest = n_pages - pp
    if rest > 0:
        state = lax.cond(jnp.max(state[0]) > EXP2_UNDERFLOW,
                         lambda s: tuple(_sb_sample_part(*args, *s, pp, rest, rest, False)), lambda s: s, state)
    return state[1].astype(BF16)


def _rope_tables(pos):
    half = HEAD_DIM // 2
    inv = np.float32(ROPE_THETA) ** (-np.arange(half, dtype=np.float32) / np.float32(half))
    ang = pos.astype(np.float32)[:, None] * inv[None, :].astype(np.float32)
    cos, sin = np.cos(ang), np.sin(ang)
    return jnp.asarray(np.concatenate([cos, cos], axis=1)), jnp.asarray(np.concatenate([-sin, sin], axis=1))


def _pick_tile(n, prefs):
    for t in prefs:
        if n % t == 0:
            return t
    return n


def _rows_to_heads(x, db, t, n_kv):
    return x.reshape(db, t, n_kv, KV_GROUP, HEAD_DIM).transpose(0, 2, 1, 3, 4).reshape(db, n_kv * t * KV_GROUP, HEAD_DIM)


def _heads_to_rows(x, db, t, n_kv):
    return x.reshape(db, n_kv, t, KV_GROUP, HEAD_DIM).transpose(0, 2, 1, 3, 4).reshape(db * t, n_kv * KV_GROUP * HEAD_DIM)


def _chunk_t(v, nb, s_len, ck):
    return v.reshape(nb, s_len // ck, ck, v.shape[1]).transpose(0, 1, 3, 2)


def _pad_page(x, page):
    return jnp.pad(x, [(0, 0), (0, page - x.shape[1])] + [(0, 0)] * (x.ndim - 2))


def kernel(x_prompt, x_sample, c_prompt, c_sample, cache_k, cache_v, cache_idx_k, page_table, norm_mix_g, norm_ffn_g,
           w_ada, b_ada, w_in_dsa, w_in_sb, w_out, w_gate, w_up, w_down, norm_final_g):
    nb, s_len, d = x_prompt.shape
    db, t_new, _ = x_sample.shape
    depth = w_ada.shape[0]
    page, n_kv = cache_k.shape[2], cache_k.shape[3]
    n_pages = page_table.shape[1]
    past_len = n_pages * page
    attn_w = w_out.shape[1]
    kv_w = n_kv * HEAD_DIM
    idx_w = IDX_HEADS * IDX_DIM
    mp, ms = nb * s_len, db * t_new

    tm = _pick_tile(s_len, (1024, 512, 256, 128))
    tm2 = _pick_tile(s_len, (512, 256, 128))
    tm3 = _pick_tile(s_len, (256, 128))
    w_out16 = _cast_bf16(w_out, _pick_tile(w_out.shape[1], (512, 256, 128)))
    tq_dsa, tq_sb, ck = 256, 256, 256
    ng = _pick_tile(n_kv, (4, 2, 1))
    pp = _pick_tile(n_pages, (8, 4, 2, 1))
    attn_scale = HEAD_DIM ** -0.5 * LOG2E

    n_c = nb + db
    c_all = jnp.pad(jnp.concatenate([c_prompt, c_sample], axis=0), ((0, (-n_c) % 16), (0, 0)))
    mod = _adaln(c_all, w_ada, b_ada).reshape(depth, c_all.shape[0], 6, d)

    cos_p, sin_p = _rope_tables(np.arange(s_len))
    cos_s, sin_s = _rope_tables(np.tile(past_len + np.arange(t_new), db))

    xp = x_prompt.reshape(mp, d)
    xs = x_sample.reshape(ms, d)
    outs = {n: [] for n in ("ikp", "ks", "vs", "iks")}
    k_all = jnp.zeros((depth, mp, kv_w), F32)
    v_all = jnp.zeros((depth, mp, kv_w), F32)

    for l in range(depth):
        mp_l = [mod[l, :nb, j].reshape(nb, 1, d) for j in range(6)]
        ms_l = [jnp.repeat(mod[l, nb:n_c, j], t_new, axis=0) for j in range(6)]
        hp = _norm(xp, norm_mix_g[l], mp_l[1], mp_l[0], tm, s_len)
        hs = _norm(xs, norm_mix_g[l], ms_l[1], ms_l[0], ms, 1)
        i = l // 2
        if l % 2 == 0:
            w = jnp.swapaxes(w_in_dsa, 1, 2)
            tail0 = attn_w + 2 * kv_w + idx_w
            w_tail = jnp.pad(w[i, tail0:, :], ((0, 2 * LANES - IDX_DIM - IDX_HEADS), (0, 0)))[None]
            tn = 1024
            hd = tn // HEAD_DIM
            (q,) = _proj(hp, w, i, 0, attn_w, cos_p, sin_p, hd, (BF16,), tm2, tn, attn_scale, w_rows=True)
            k_all, k16 = _proj(hp, w, i, attn_w, kv_w, cos_p, sin_p, hd, (F32, BF16), tm2, tn, w_rows=True,
                               stack=(k_all, l))
            v_all, v16 = _proj(hp, w, i, attn_w + kv_w, kv_w, cos_p, sin_p, 0, (F32, BF16), tm2, tn, w_rows=True,
                               stack=(v_all, l))
            (iq,) = _proj(hp, w, i, attn_w + 2 * kv_w, idx_w, cos_p, sin_p, hd, (F32,), tm2, tn, w_rows=True)
            (ikiw,) = _proj(hp, w_tail, 0, 0, 2 * LANES, cos_p, sin_p, 1, (F32,), tm2, 2 * LANES, w_rows=True)
            mix_p = _dsa_prompt(q, k16, _chunk_t(v16, nb, s_len, ck), iq, ikiw, ikiw[:, IDX_DIM:IDX_DIM + IDX_HEADS].T,
                                nb, s_len, tq_dsa, ck, ng)
            outs["ikp"].append(ikiw[:, :IDX_DIM].reshape(nb, s_len, IDX_DIM))

            (all_s,) = _proj(hs, w, i, 0, tail0, cos_s, sin_s, hd, (F32,), ms, tn, plain_tiles=((attn_w + kv_w) // tn,),
                             w_rows=True)
            (ikiw_s,) = _proj(hs, w_tail, 0, 0, 2 * LANES, cos_s, sin_s, 1, (F32,), ms, 2 * LANES, w_rows=True)
            qs = (all_s[:, :attn_w] * attn_scale).astype(BF16)
            ks32 = all_s[:, attn_w:attn_w + kv_w]
            vs32 = all_s[:, attn_w + kv_w:attn_w + 2 * kv_w]
            iqs = all_s[:, attn_w + 2 * kv_w:]
            iks = ikiw_s[:, :IDX_DIM].reshape(db, t_new, IDX_DIM)
            iws = ikiw_s[:, IDX_DIM:IDX_DIM + IDX_HEADS].reshape(db, t_new, IDX_HEADS)
            iq_rows = jnp.broadcast_to(iqs.reshape(db, t_new, IDX_HEADS, 1, IDX_DIM).transpose(0, 2, 1, 3, 4),
                                       (db, IDX_HEADS, t_new, KV_GROUP, IDX_DIM)).reshape(db, -1, IDX_DIM)
            w_rows = jnp.broadcast_to(iws.transpose(0, 2, 1)[..., None], (db, IDX_HEADS, t_new, KV_GROUP)).reshape(db, -1, 1)
            sc_past, sc_new = _idx_sample(cache_idx_k, i, page_table, iq_rows, w_rows, _pad_page(iks, page),
                                          _pick_tile(n_pages, (16, 8, 4, 2, 1)))
            scores = jnp.concatenate([sc_past, sc_new, jnp.zeros((db, sc_new.shape[1], (pp - 1) * page), F32)], axis=2)
            mask = _select_sample(scores, past_len, t_new)
            ks4 = ks32.reshape(db, t_new, n_kv, HEAD_DIM)
            vs4 = vs32.reshape(db, t_new, n_kv, HEAD_DIM)
            mix_s = _dsa_sample(cache_k, cache_v, l, page_table, _rows_to_heads(qs, db, t_new, n_kv), mask,
                                _pad_page(ks4, page), _pad_page(vs4, page), pp)
            mix_s = _heads_to_rows(mix_s, db, t_new, n_kv)
            outs["iks"].append(iks)
        else:
            w = w_in_sb
            tn = 1024
            (q,) = _proj(hp, w, i, 0, attn_w, cos_p, sin_p, 0, (BF16,), tm2, tn, attn_scale)
            k_all, k16 = _proj(hp, w, i, attn_w, kv_w, cos_p, sin_p, 0, (F32, BF16), tm2, tn, stack=(k_all, l))
            v_all, v16 = _proj(hp, w, i, attn_w + kv_w, kv_w, cos_p, sin_p, 0, (F32, BF16), tm2, tn,
                               stack=(v_all, l))
            mix_p = _sb_prompt(q, k16, _chunk_t(v16, nb, s_len, ck), nb, s_len, tq_sb, ck, ng)
            (all_s,) = _proj(hs, w, i, 0, attn_w + 2 * kv_w, cos_s, sin_s, 0, (F32,), ms, tn)
            qs = (all_s[:, :attn_w] * attn_scale).astype(BF16)
            ks32 = all_s[:, attn_w:attn_w + kv_w]
            vs32 = all_s[:, attn_w + kv_w:]
            ks4 = ks32.reshape(db, t_new, n_kv, HEAD_DIM)
            vs4 = vs32.reshape(db, t_new, n_kv, HEAD_DIM)
            mix_s = _sb_sample(cache_k, cache_v, l, page_table, _rows_to_heads(qs, db, t_new, n_kv),
                               _pad_page(ks4, page), _pad_page(vs4, page), pp)
            mix_s = _heads_to_rows(mix_s, db, t_new, n_kv)
        outs["ks"].append(ks4)
        outs["vs"].append(vs4)

        xp, hp = _out_norm(mix_p, w_out16, l, xp, mp_l[2], norm_ffn_g[l], mp_l[4], mp_l[3], tm3, s_len)
        xs, hs = _out_norm(mix_s, w_out16, l, xs, ms_l[2], norm_ffn_g[l], ms_l[4], ms_l[3], ms, 1)
        tn_ff = _pick_tile(w_gate.shape[2], (512, 256, 128))
        act_p, act_s = _swiglu(hp, w_gate, w_up, l, tm, tn_ff, hs)
        xp, xs = _gres(act_p, w_down, l, xp, mp_l[5], tm2, 512, s_len, act_s, xs, ms_l[5])

    y_prompt = _final_norm(xp, norm_final_g, tm).reshape(nb, s_len, d)
    y_sample = _final_norm(xs, norm_final_g, ms).reshape(db, t_new, d)
    kv_shape = (depth, nb, s_len, n_kv, HEAD_DIM)
    return (y_prompt, y_sample, k_all.reshape(kv_shape), v_all.reshape(kv_shape), jnp.stack(outs["ikp"]),
            jnp.stack(outs["ks"]), jnp.stack(outs["vs"]), jnp.stack(outs["iks"]))
```

```python
import functools

import numpy as np
import jax
import jax.numpy as jnp
from jax import lax
from jax.experimental import pallas as pl
from jax.experimental.pallas import tpu as pltpu

F32 = jnp.float32
BF16 = jnp.bfloat16
I32 = jnp.int32

LANES = 128
HEAD_DIM = 128
KV_GROUP = 2
IDX_HEADS = 16
IDX_DIM = 128
TOPK_MAX = 256
ROPE_THETA = 10000.0
EPS = 1e-6
NEG = -1e30
INT_MIN = -2 ** 31
KEY_NEG_INF = -2139095041
ONES_ROWS = 16
LOG2E = 1.4426950408889634
EXP2_UNDERFLOW = -160.0
VMEM_LIMIT = 56 * 1024 * 1024

_NT = (((1,), (1,)), ((), ()))


def _cparams(sem):
    return pltpu.CompilerParams(dimension_semantics=sem, vmem_limit_bytes=VMEM_LIMIT)


def _split_bf16(x):
    hi = x.astype(BF16)
    lo = (x - hi.astype(F32)).astype(BF16)
    return hi, lo


def _sortable(x):
    bits = lax.bitcast_convert_type(x, I32)
    return bits ^ ((bits >> 31) & 0x7FFFFFFF)


def _adaln_kernel(c_ref, w_ref, b_ref, o_ref):
    c = c_ref[...]
    s = (c * jax.nn.sigmoid(c)).astype(BF16)
    o_ref[...] = jnp.dot(s, w_ref[...].astype(BF16), preferred_element_type=F32) + b_ref[...]


def _adaln(c_pad, w_ada, b_ada):
    depth, d, n = w_ada.shape
    r = c_pad.shape[0]
    tn = 1024
    return pl.pallas_call(
        _adaln_kernel,
        grid=(depth, n // tn),
        in_specs=[pl.BlockSpec((r, d), lambda l, j: (0, 0)),
                  pl.BlockSpec((None, d, tn), lambda l, j: (l, 0, j)),
                  pl.BlockSpec((None, 1, tn), lambda l, j: (l, 0, j))],
        out_specs=pl.BlockSpec((None, r, tn), lambda l, j: (l, 0, j)),
        out_shape=jax.ShapeDtypeStruct((depth, r, n), F32),
        compiler_params=_cparams(("arbitrary", "arbitrary")),
        name="adaln",
    )(c_pad, w_ada, b_ada.reshape(depth, 1, n))


def _norm_kernel(x_ref, g_ref, sc_ref, sh_ref, o_ref):
    x = x_ref[...]
    y = x * lax.rsqrt(jnp.mean(x * x, axis=-1, keepdims=True) + EPS) * g_ref[...]
    o_ref[...] = (y * (1.0 + sc_ref[...]) + sh_ref[...]).astype(o_ref.dtype)


def _final_norm_kernel(x_ref, g_ref, o_ref):
    x = x_ref[...]
    o_ref[...] = x * lax.rsqrt(jnp.mean(x * x, axis=-1, keepdims=True) + EPS) * g_ref[...]


def _mod_spec(mod, tm, width, rows_per_batch, col_of):
    if mod.ndim == 3:
        return pl.BlockSpec((None, 1, width), lambda *ij: ((ij[0] if col_of is None else ij[1]) * tm // rows_per_batch, 0,
                                                           0 if col_of is None else ij[0]))
    return pl.BlockSpec((tm, width), lambda *ij: ((ij[0] if col_of is None else ij[1]), 0 if col_of is None else ij[0]))


def _norm(x, g, sc, sh, tm, rows_per_batch):
    m, d = x.shape
    return pl.pallas_call(
        _norm_kernel,
        grid=(m // tm,),
        in_specs=[pl.BlockSpec((tm, d), lambda i: (i, 0)),
                  pl.BlockSpec((1, d), lambda i: (0, 0)),
                  _mod_spec(sc, tm, d, rows_per_batch, None),
                  _mod_spec(sh, tm, d, rows_per_batch, None)],
        out_specs=pl.BlockSpec((tm, d), lambda i: (i, 0)),
        out_shape=jax.ShapeDtypeStruct((m, d), BF16),
        compiler_params=_cparams(("arbitrary",)),
        name="norm_mod",
    )(x, g.reshape(1, d), sc, sh)


def _final_norm(x, g, tm):
    m, d = x.shape
    return pl.pallas_call(
        _final_norm_kernel,
        grid=(m // tm,),
        in_specs=[pl.BlockSpec((tm, d), lambda i: (i, 0)), pl.BlockSpec((1, d), lambda i: (0, 0))],
        out_specs=pl.BlockSpec((tm, d), lambda i: (i, 0)),
        out_shape=jax.ShapeDtypeStruct((m, d), F32),
        compiler_params=_cparams(("arbitrary",)),
        name="final_norm",
    )(x, g.reshape(1, d))


def _proj_kernel(h_ref, w_ref, cos_ref, sin_ref, *rest, rope, plain_tiles, n_out, scale, w_rows, n_skip):
    rest = rest[n_skip:]
    outs, wb_ref = rest[:n_out], rest[n_out]

    @pl.when(pl.program_id(1) == 0)
    def _():
        wb_ref[...] = w_ref[...].astype(BF16)

    if w_rows:
        acc = lax.dot_general(h_ref[...], wb_ref[...], _NT, preferred_element_type=F32)
    else:
        acc = jnp.dot(h_ref[...], wb_ref[...], preferred_element_type=F32)
    if scale != 1.0:
        acc = acc * scale
    tn = acc.shape[1]

    def emit(n_rope):
        if not n_rope:
            for o in outs:
                o[...] = acc.astype(o.dtype)
            return
        cos = cos_ref[...]
        sin = sin_ref[...]
        for c in range(tn // HEAD_DIM):
            y = acc[:, c * HEAD_DIM:(c + 1) * HEAD_DIM]
            if c < n_rope:
                y = y * cos + pltpu.roll(y, HEAD_DIM // 2, 1) * sin
            for o in outs:
                o[:, c * HEAD_DIM:(c + 1) * HEAD_DIM] = y.astype(o.dtype)

    if rope and plain_tiles:
        j = pl.program_id(0)
        plain = functools.reduce(jnp.logical_or, [j == t for t in plain_tiles])
        pl.when(plain)(lambda: emit(0))
        pl.when(jnp.logical_not(plain))(lambda: emit(rope))
    else:
        emit(rope)


def _proj(h, w, layer, col0, ncols, cos, sin, rope, out_dtypes, tm, tn, scale=1.0, plain_tiles=(), w_rows=False,
          stack=None):
    m, k = h.shape
    npb = cos.shape[0] // tm
    j0 = col0 // tn
    n_out = len(out_dtypes)
    if w_rows:
        w_spec = pl.BlockSpec((None, tn, k), lambda j, i: (layer, j + j0, 0))
    else:
        w_spec = pl.BlockSpec((None, k, tn), lambda j, i: (layer, 0, j + j0))
    in_specs = [pl.BlockSpec((tm, k), lambda j, i: (i, 0)),
                w_spec,
                pl.BlockSpec((tm, HEAD_DIM), lambda j, i: (i % npb, 0)),
                pl.BlockSpec((tm, HEAD_DIM), lambda j, i: (i % npb, 0))]
    out_specs = [pl.BlockSpec((tm, tn), lambda j, i: (i, j)) for _ in out_dtypes]
    out_shape = [jax.ShapeDtypeStruct((m, ncols), dt) for dt in out_dtypes]
    args, aliases = [h, w, cos, sin], {}
    if stack is not None:
        buf, slot = stack
        out_specs[0] = pl.BlockSpec((None, tm, tn), lambda j, i: (slot, i, j))
        out_shape[0] = jax.ShapeDtypeStruct(buf.shape, buf.dtype)
        in_specs.append(pl.BlockSpec(memory_space=pl.ANY))
        args.append(buf)
        aliases = {len(args) - 1: 0}
    return pl.pallas_call(
        functools.partial(_proj_kernel, rope=rope, plain_tiles=tuple(plain_tiles), n_out=n_out, scale=scale,
                          w_rows=w_rows, n_skip=len(aliases)),
        grid=(ncols // tn, m // tm),
        in_specs=in_specs,
        out_specs=out_specs,
        out_shape=out_shape,
        scratch_shapes=[pltpu.VMEM((tn, k) if w_rows else (k, tn), BF16)],
        input_output_aliases=aliases,
        compiler_params=_cparams(("arbitrary", "arbitrary")),
        name="proj_rope" if rope else "proj",
    )(*args)


def _gres_kernel(h_ref, w_ref, x_ref, g_ref, hs_ref, xs_ref, gs_ref, o_ref, os_ref, wb_ref):
    @pl.when(pl.program_id(1) == 0)
    def _():
        wb_ref[...] = w_ref[...].astype(BF16)
        os_ref[...] = xs_ref[...] + gs_ref[...] * jnp.dot(hs_ref[...], wb_ref[...], preferred_element_type=F32)

    acc = jnp.dot(h_ref[...], wb_ref[...], preferred_element_type=F32)
    o_ref[...] = x_ref[...] + g_ref[...] * acc


def _gres(h, w, layer, x, gate, tm, tn, rows_per_batch, h_s, x_s, gate_s):
    m, k = h.shape
    n = w.shape[2]
    m_s = h_s.shape[0]
    return pl.pallas_call(
        _gres_kernel,
        grid=(n // tn, m // tm),
        in_specs=[pl.BlockSpec((tm, k), lambda j, i: (i, 0)),
                  pl.BlockSpec((None, k, tn), lambda j, i: (layer, 0, j)),
                  pl.BlockSpec((tm, tn), lambda j, i: (i, j)),
                  _mod_spec(gate, tm, tn, rows_per_batch, True),
                  pl.BlockSpec((m_s, k), lambda j, i: (0, 0)),
                  pl.BlockSpec((m_s, tn), lambda j, i: (0, j)),
                  pl.BlockSpec((m_s, tn), lambda j, i: (0, j))],
        out_specs=[pl.BlockSpec((tm, tn), lambda j, i: (i, j)), pl.BlockSpec((m_s, tn), lambda j, i: (0, j))],
        out_shape=[jax.ShapeDtypeStruct((m, n), F32), jax.ShapeDtypeStruct((m_s, n), F32)],
        scratch_shapes=[pltpu.VMEM((k, tn), BF16)],
        compiler_params=_cparams(("arbitrary", "arbitrary")),
        name="gated_residual",
    )(h, w, x, gate, h_s, x_s, gate_s)


def _cast_kernel(w_ref, o_ref):
    o_ref[...] = w_ref[...].astype(o_ref.dtype)


def _cast_bf16(w, tk):
    depth, k, n = w.shape
    return pl.pallas_call(
        _cast_kernel,
        grid=(depth, k // tk),
        in_specs=[pl.BlockSpec((None, tk, n), lambda l, i: (l, i, 0))],
        out_specs=pl.BlockSpec((None, tk, n), lambda l, i: (l, i, 0)),
        out_shape=jax.ShapeDtypeStruct(w.shape, BF16),
        compiler_params=_cparams(("arbitrary", "arbitrary")),
        name="cast_bf16",
    )(w)


def _out_norm_kernel(m_ref, w_ref, x_ref, gate_ref, g_ref, sc_ref, sh_ref, xo_ref, ho_ref):
    x = x_ref[...] + gate_ref[...] * jnp.dot(m_ref[...], w_ref[...], preferred_element_type=F32)
    xo_ref[...] = x
    y = x * lax.rsqrt(jnp.mean(x * x, axis=-1, keepdims=True) + EPS) * g_ref[...]
    ho_ref[...] = (y * (1.0 + sc_ref[...]) + sh_ref[...]).astype(ho_ref.dtype)


def _out_norm(mix, wb, layer, x, gate, g, sc, sh, tm, rows_per_batch):
    m, k = mix.shape
    d = wb.shape[2]
    row = lambda i: (i, 0)
    return pl.pallas_call(
        _out_norm_kernel,
        grid=(m // tm,),
        in_specs=[pl.BlockSpec((tm, k), row),
                  pl.BlockSpec((None, k, d), lambda i: (layer, 0, 0)),
                  pl.BlockSpec((tm, d), row),
                  _mod_spec(gate, tm, d, rows_per_batch, None),
                  pl.BlockSpec((1, d), lambda i: (0, 0)),
                  _mod_spec(sc, tm, d, rows_per_batch, None),
                  _mod_spec(sh, tm, d, rows_per_batch, None)],
        out_specs=[pl.BlockSpec((tm, d), row), pl.BlockSpec((tm, d), row)],
        out_shape=[jax.ShapeDtypeStruct((m, d), F32), jax.ShapeDtypeStruct((m, d), BF16)],
        compiler_params=_cparams(("arbitrary",)),
        name="out_proj_norm",
    )(mix, wb, x, gate, g.reshape(1, d), sc, sh)


def _swiglu_kernel(h_ref, wg_ref, wu_ref, hs_ref, o_ref, os_ref, wgb_ref, wub_ref):
    def act(h):
        a = jnp.dot(h, wgb_ref[...], preferred_element_type=F32)
        b = jnp.dot(h, wub_ref[...], preferred_element_type=F32)
        return (a * jax.nn.sigmoid(a) * b).astype(BF16)

    @pl.when(pl.program_id(1) == 0)
    def _():
        wgb_ref[...] = wg_ref[...].astype(BF16)
        wub_ref[...] = wu_ref[...].astype(BF16)
        os_ref[...] = act(hs_ref[...])

    o_ref[...] = act(h_ref[...])


def _swiglu(h, wg, wu, layer, tm, tn, h_s):
    m, k = h.shape
    n = wg.shape[2]
    m_s = h_s.shape[0]
    return pl.pallas_call(
        _swiglu_kernel,
        grid=(n // tn, m // tm),
        in_specs=[pl.BlockSpec((tm, k), lambda j, i: (i, 0)),
                  pl.BlockSpec((None, k, tn), lambda j, i: (layer, 0, j)),
                  pl.BlockSpec((None, k, tn), lambda j, i: (layer, 0, j)),
                  pl.BlockSpec((m_s, k), lambda j, i: (0, 0))],
        out_specs=[pl.BlockSpec((tm, tn), lambda j, i: (i, j)), pl.BlockSpec((m_s, tn), lambda j, i: (0, j))],
        out_shape=[jax.ShapeDtypeStruct((m, n), BF16), jax.ShapeDtypeStruct((m_s, n), BF16)],
        scratch_shapes=[pltpu.VMEM((k, tn), BF16), pltpu.VMEM((k, tn), BF16)],
        compiler_params=_cparams(("arbitrary", "arbitrary")),
        name="swiglu",
    )(h, wg, wu, h_s)


def _kth_threshold(count_ge, topk, shape):
    t0 = jnp.where(count_ge(jnp.zeros(shape, I32)) >= topk, 0, INT_MIN).astype(I32)

    def body(it, t):
        cand = t | jnp.left_shift(jnp.int32(1), 30 - it)
        return jnp.where(count_ge(cand) >= topk, cand, t)

    return lax.fori_loop(0, 31, body, t0)


def _tie_cutoff(count_tie_below, need, nbits, shape):
    def body(it, a):
        cand = a | jnp.left_shift(jnp.int32(1), nbits - 1 - it)
        return jnp.where(count_tie_below(cand) < need, cand, a)

    return lax.fori_loop(0, nbits, body, jnp.zeros(shape, I32))


def _dsa_prompt_kernel(q_ref, k_ref, vt_ref, iq_ref, ik_ref, iwt_ref, o_ref,
                       iqcat_ref, keys_ref, bias_ref, cut_ref, m_ref, acc_ref, q2_ref,
                       *, tq, ck, ng, topk, nbits, idx_scale):
    i = pl.program_id(1)
    nkc = ((i + 1) * tq + ck - 1) // ck
    qpos = i * tq + lax.broadcasted_iota(I32, (1, tq), 1)

    def kpos_of(c):
        return c * ck + lax.broadcasted_iota(I32, (ck, tq), 0)

    @pl.when(pl.program_id(2) == 0)
    def _index():
        for h in range(IDX_HEADS):
            hi, lo = _split_bf16(iq_ref[:, h * IDX_DIM:(h + 1) * IDX_DIM])
            iqcat_ref[:, h * 2 * IDX_DIM:h * 2 * IDX_DIM + IDX_DIM] = hi
            iqcat_ref[:, h * 2 * IDX_DIM + IDX_DIM:(h + 1) * 2 * IDX_DIM] = lo

        def score_chunk(c, carry):
            ikh = ik_ref[pl.ds(pl.multiple_of(c * ck, ck), ck), :].astype(BF16)
            ikcat = jnp.concatenate([ikh, ikh], axis=1)
            acc = jnp.zeros((ck, tq), F32)
            for h in range(IDX_HEADS):
                lg = lax.dot_general(ikcat, iqcat_ref[:, h * 2 * IDX_DIM:(h + 1) * 2 * IDX_DIM], _NT,
                                     preferred_element_type=F32)
                acc = acc + iwt_ref[h:h + 1, :] * jnp.maximum(lg, 0.0)
            sc = jnp.where(kpos_of(c) <= qpos, acc * idx_scale, -jnp.inf)
            keys_ref[c] = _sortable(sc)
            return carry

        lax.fori_loop(0, nkc, score_chunk, 0)

        def count(pred):
            def body(c, part):
                m = jnp.where(pred(keys_ref[c], c), 1.0, 0.0)
                return part + jnp.sum(m.reshape(ck // 32, 32, tq), axis=0)
            part = lax.fori_loop(0, nkc, body, jnp.zeros((32, tq), F32))
            return jnp.sum(part, axis=0, keepdims=True)

        thr = _kth_threshold(lambda cand: count(lambda kc, c: kc >= cand), float(topk), (1, tq))
        n_gt = count(lambda kc, c: kc > thr)
        n_ge = count(lambda kc, c: kc >= thr)
        need = float(topk) - n_gt
        crowded = (n_ge > float(topk)) & (thr > KEY_NEG_INF)
        cut_ref[...] = jnp.full((1, tq), 2 ** 30, I32)

        @pl.when(jnp.max(jnp.where(crowded, 1.0, 0.0)) > 0.5)
        def _():
            cut_ref[...] = _tie_cutoff(lambda cand: count(lambda kc, c: (kc == thr) & (kpos_of(c) < cand)),
                                       need, nbits, (1, tq))

        cut = cut_ref[...]

        def bias_chunk(c, carry):
            kc = keys_ref[c]
            kpos = kpos_of(c)
            sel = ((kc > thr) | ((kc == thr) & (kpos <= cut))) & (kpos <= qpos)
            bias_ref[c] = jnp.where(sel, 0.0, NEG).astype(BF16)
            return carry

        lax.fori_loop(0, nkc, bias_chunk, 0)

    m_ref[...] = jnp.full(m_ref.shape, NEG, F32)
    acc_ref[...] = jnp.zeros(acc_ref.shape, F32)
    for g in range(ng):
        for j in range(KV_GROUP):
            h = g * KV_GROUP + j
            q2_ref[g, j * tq:(j + 1) * tq, :] = q_ref[:, h * HEAD_DIM:(h + 1) * HEAD_DIM]
    ones = jnp.ones((ONES_ROWS, ck), BF16)

    def attend(c, carry):
        off = pl.multiple_of(c * ck, ck)
        b = bias_ref[c]
        b2 = jnp.concatenate([b] * KV_GROUP, axis=1)

        def scores(g):
            s = lax.dot_general(k_ref[pl.ds(off, ck), g * HEAD_DIM:(g + 1) * HEAD_DIM], q2_ref[g], _NT,
                                preferred_element_type=F32)
            return s.astype(BF16) + b2

        def softmax(g, s):
            m = m_ref[g]
            m_new = jnp.maximum(m, jnp.max(s, axis=0, keepdims=True).astype(F32))
            m_ref[g] = m_new
            return jnp.exp2(s - m_new.astype(BF16)), jnp.exp2(m - m_new)

        def weigh(g, p, alpha):
            vt = jnp.concatenate([vt_ref[c, g * HEAD_DIM:(g + 1) * HEAD_DIM, :], ones], axis=0)
            acc_ref[g] = alpha * acc_ref[g] + jnp.dot(vt, p, preferred_element_type=F32)

        ahead = 3
        s = {g: scores(g) for g in range(min(ahead, ng))}
        for g in range(ng):
            p, alpha = softmax(g, s.pop(g))
            if g + ahead < ng:
                s[g + ahead] = scores(g + ahead)
            weigh(g, p, alpha)
        return carry

    lax.fori_loop(0, nkc, attend, 0)
    for g in range(ng):
        out = acc_ref[g, :HEAD_DIM] * (1.0 / acc_ref[g, HEAD_DIM:HEAD_DIM + 1])
        for j in range(KV_GROUP):
            h = g * KV_GROUP + j
            o_ref[:, h * HEAD_DIM:(h + 1) * HEAD_DIM] = out[:, j * tq:(j + 1) * tq].T.astype(o_ref.dtype)


def _dsa_prompt(q, k, vt, iq, ikiw, iwt, nb, s_len, tq, ck, ng):
    n_kv = k.shape[1] // HEAD_DIM
    nq = s_len // tq
    topk = min(TOPK_MAX, s_len // 4)
    nh = ng * KV_GROUP
    kern = functools.partial(_dsa_prompt_kernel, tq=tq, ck=ck, ng=ng, topk=topk, nbits=max(1, (s_len - 1).bit_length()),
                             idx_scale=(IDX_HEADS ** -0.5) * (IDX_DIM ** -0.5))
    return pl.pallas_call(
        kern,
        grid=(nb, nq, n_kv // ng),
        in_specs=[pl.BlockSpec((tq, nh * HEAD_DIM), lambda b, i, g: (b * nq + i, g)),
                  pl.BlockSpec((s_len, ng * HEAD_DIM), lambda b, i, g: (b, g)),
                  pl.BlockSpec((None, s_len // ck, ng * HEAD_DIM, ck), lambda b, i, g: (b, 0, g, 0)),
                  pl.BlockSpec((tq, IDX_HEADS * IDX_DIM), lambda b, i, g: (b * nq + i, 0)),
                  pl.BlockSpec((s_len, IDX_DIM), lambda b, i, g: (b, 0)),
                  pl.BlockSpec((IDX_HEADS, tq), lambda b, i, g: (0, b * nq + i))],
        out_specs=pl.BlockSpec((tq, nh * HEAD_DIM), lambda b, i, g: (b * nq + i, g)),
        out_shape=jax.ShapeDtypeStruct(q.shape, BF16),
        scratch_shapes=[pltpu.VMEM((tq, IDX_HEADS * 2 * IDX_DIM), BF16),
                        pltpu.VMEM((s_len // ck, ck, tq), I32),
                        pltpu.VMEM((s_len // ck, ck, tq), BF16),
                        pltpu.VMEM((1, tq), I32),
                        pltpu.VMEM((ng, 1, KV_GROUP * tq), F32),
                        pltpu.VMEM((ng, HEAD_DIM + ONES_ROWS, KV_GROUP * tq), F32),
                        pltpu.VMEM((ng, KV_GROUP * tq, HEAD_DIM), BF16)],
        compiler_params=_cparams(("arbitrary", "arbitrary", "arbitrary")),
        name="dsa_prompt",
    )(q, k, vt, iq, ikiw, iwt)


def _softplus2(z):
    return jnp.maximum(z, 0.0) + jnp.log2(1.0 + jnp.exp2(-jnp.abs(z)))


def _sb_prompt_kernel(q_ref, k_ref, vt_ref, ln_ref, o_ref, tail_ref, acc_ref, *, tq, ck, ng):
    i = pl.program_id(1)
    nkc = ((i + 1) * tq + ck - 1) // ck
    nh = ng * KV_GROUP
    qpos = i * tq + lax.broadcasted_iota(I32, (1, tq), 1)
    tail_ref[...] = jnp.zeros(tail_ref.shape, F32)
    acc_ref[...] = jnp.zeros(acc_ref.shape, F32)

    def chunk(c, masked):
        off = pl.multiple_of(c * ck, ck)
        mask = (c * ck + lax.broadcasted_iota(I32, (ck, tq), 0)) < qpos if masked else None

        def logits(h):
            g = h // KV_GROUP
            return lax.dot_general(k_ref[pl.ds(off, ck), g * HEAD_DIM:(g + 1) * HEAD_DIM],
                                   q_ref[:, h * HEAD_DIM:(h + 1) * HEAD_DIM], _NT, preferred_element_type=F32)

        def keep(h, z):
            sp = jnp.where(z > 64.0, z, jnp.log2(1.0 + jnp.exp2(z)))
            spm = jnp.where(mask, sp, 0.0) if masked else sp
            la = jnp.dot(ln_ref[...], spm.astype(BF16), preferred_element_type=F32)
            tail = tail_ref[h]
            tail_ref[h] = tail + la[ck:ck + 1]
            return z - sp + tail, la

        def weigh(h, base, la):
            g = h // KV_GROUP
            a = jnp.exp2(base + la[:ck])
            if masked:
                a = jnp.where(mask, a, 0.0)
            acc_ref[h] += jnp.dot(vt_ref[c, g * HEAD_DIM:(g + 1) * HEAD_DIM, :], a.astype(BF16),
                                  preferred_element_type=F32)

        a1, a2 = 2, 4
        z = {h: logits(h) for h in range(min(a2, nh))}
        kept = {h: keep(h, z.pop(h)) for h in range(min(a1, nh))}
        for h in range(nh):
            if h + a1 < nh:
                kept[h + a1] = keep(h + a1, z.pop(h + a1))
            if h + a2 < nh:
                z[h + a2] = logits(h + a2)
            weigh(h, *kept.pop(h))

    chunk(nkc - 1, True)

    def live(carry):
        r, top = carry
        return (r < nkc) & (top > EXP2_UNDERFLOW)

    def body(carry):
        r, _ = carry
        chunk(nkc - 1 - r, False)
        return r + 1, jnp.max(tail_ref[...])

    lax.while_loop(live, body, (jnp.int32(1), jnp.max(tail_ref[...])))
    for h in range(nh):
        o_ref[:, h * HEAD_DIM:(h + 1) * HEAD_DIM] = acc_ref[h].T.astype(o_ref.dtype)


def _suffix_matrix(n):
    return jnp.asarray(np.tril(np.ones((n, n), np.float32), -1), dtype=BF16)


def _sb_prompt(q, k, vt, nb, s_len, tq, ck, ng):
    assert ck % tq == 0
    n_kv = k.shape[1] // HEAD_DIM
    nq = s_len // tq
    gw = KV_GROUP * HEAD_DIM
    ln = jnp.concatenate([-_suffix_matrix(ck).T, -jnp.ones((8, ck), BF16)], axis=0)
    return pl.pallas_call(
        functools.partial(_sb_prompt_kernel, tq=tq, ck=ck, ng=ng),
        grid=(nb, nq, n_kv // ng),
        in_specs=[pl.BlockSpec((tq, ng * gw), lambda b, i, g: (b * nq + i, g)),
                  pl.BlockSpec((s_len, ng * HEAD_DIM), lambda b, i, g: (b, g)),
                  pl.BlockSpec((None, s_len // ck, ng * HEAD_DIM, ck), lambda b, i, g: (b, 0, g, 0)),
                  pl.BlockSpec((ck + 8, ck), lambda b, i, g: (0, 0))],
        out_specs=pl.BlockSpec((tq, ng * gw), lambda b, i, g: (b * nq + i, g)),
        out_shape=jax.ShapeDtypeStruct(q.shape, BF16),
        scratch_shapes=[pltpu.VMEM((ng * KV_GROUP, 1, tq), F32),
                        pltpu.VMEM((ng * KV_GROUP, HEAD_DIM, tq), F32)],
        compiler_params=_cparams(("arbitrary", "arbitrary", "arbitrary")),
        name="sb_prompt",
    )(q, k, vt, ln)


def _page_consts(page, n_kv, rows):
    lane = np.arange(page * n_kv)
    expand = (lane[None, :] // n_kv == np.arange(page)[:, None]).astype(np.float32)
    valid = (lane[None, :] % n_kv == (np.arange(rows)[:, None] // (rows // n_kv))).astype(np.float32)
    return jnp.asarray(expand, BF16), jnp.asarray(expand.T.copy(), BF16), jnp.asarray(valid, F32)


def _idx_sample_kernel(pt_ref, iq_ref, w_ref, ikn_ref, *rest, pp, idx_scale):
    pages, (past_ref, new_ref) = rest[:pp], rest[pp:]
    hi, lo = _split_bf16(iq_ref[...])
    iqcat = jnp.concatenate([hi, lo], axis=1)
    w = w_ref[...]
    rows = iqcat.shape[0]

    def logits(ik):
        ikh = ik.astype(BF16)
        return lax.dot_general(iqcat, jnp.concatenate([ikh, ikh], axis=1), _NT, preferred_element_type=F32)

    def score(lg):
        x = jnp.maximum(lg, 0.0) * w
        return jnp.sum(x.reshape(IDX_HEADS, rows // IDX_HEADS, x.shape[1]), axis=0) * idx_scale

    lgs = [logits(pages[j][...]) for j in range(pp)]
    for j in range(pp):
        past_ref[:, j * LANES:(j + 1) * LANES] = score(lgs[j])

    @pl.when(pl.program_id(1) == 0)
    def _():
        new_ref[...] = score(logits(ikn_ref[...]))


def _idx_sample(cache_idx_k, layer, page_table, iq_rows, w_rows, ik_new_pad, pp):
    db, n_pages = page_table.shape
    page = cache_idx_k.shape[2]
    rows = iq_rows.shape[1]
    r8 = rows // IDX_HEADS
    page_specs = [pl.BlockSpec((None, None, page, IDX_DIM),
                               functools.partial(lambda b, p, pt, j: (layer, pt[b, p * pp + j], 0, 0), j=j))
                  for j in range(pp)]
    return pl.pallas_call(
        functools.partial(_idx_sample_kernel, pp=pp, idx_scale=(IDX_HEADS ** -0.5) * (IDX_DIM ** -0.5)),
        grid_spec=pltpu.PrefetchScalarGridSpec(
            num_scalar_prefetch=1,
            grid=(db, n_pages // pp),
            in_specs=[pl.BlockSpec((None, rows, IDX_DIM), lambda b, p, pt: (b, 0, 0)),
                      pl.BlockSpec((None, rows, 1), lambda b, p, pt: (b, 0, 0)),
                      pl.BlockSpec((None, page, IDX_DIM), lambda b, p, pt: (b, 0, 0))] + page_specs,
            out_specs=[pl.BlockSpec((None, r8, pp * page), lambda b, p, pt: (b, 0, p)),
                       pl.BlockSpec((None, r8, page), lambda b, p, pt: (b, 0, 0))]),
        out_shape=[jax.ShapeDtypeStruct((db, r8, n_pages * page), F32),
                   jax.ShapeDtypeStruct((db, r8, page), F32)],
        compiler_params=_cparams(("arbitrary", "arbitrary")),
        name="idx_sample",
    )(page_table, iq_rows, w_rows, ik_new_pad, *([cache_idx_k] * pp))


def _select_sample_kernel(s_ref, o_ref, cut_ref, *, topk, nbits, past_len):
    sc = s_ref[...]
    rows, length = sc.shape
    qpos = past_len + lax.broadcasted_iota(I32, (rows, 1), 0) // KV_GROUP
    kpos = lax.broadcasted_iota(I32, (rows, length), 1)
    causal = kpos <= qpos
    key = _sortable(jnp.where(causal, sc, -jnp.inf))

    n_acc = 4

    def count(pred_of):
        accs = [jnp.zeros((rows, LANES), F32) for _ in range(n_acc)]
        for j in range(length // LANES):
            sl = slice(j * LANES, (j + 1) * LANES)
            accs[j % n_acc] = accs[j % n_acc] + jnp.where(pred_of(key[:, sl], kpos[:, sl]), 1.0, 0.0)
        return jnp.sum((accs[0] + accs[1]) + (accs[2] + accs[3]), axis=1, keepdims=True)

    thr = _kth_threshold(lambda cand: count(lambda k, p: k >= cand), float(topk), (rows, 1))
    need = float(topk) - count(lambda k, p: k > thr)
    crowded = (count(lambda k, p: k >= thr) > float(topk)) & (thr > KEY_NEG_INF)
    cut_ref[...] = jnp.full((rows, 1), 2 ** 30, I32)

    @pl.when(jnp.max(jnp.where(crowded, 1.0, 0.0)) > 0.5)
    def _():
        cut_ref[...] = _tie_cutoff(lambda cand: count(lambda k, p: (k == thr) & (p < cand)), need, nbits, (rows, 1))

    sel = ((key > thr) | ((key == thr) & (kpos <= cut_ref[...]))) & causal
    o_ref[...] = jnp.where(sel, 1.0, 0.0)


def _select_sample(scores, past_len, n_new):
    db, rows, length = scores.shape
    topk = min(TOPK_MAX, (past_len + n_new) // 4)
    return pl.pallas_call(
        functools.partial(_select_sample_kernel, topk=topk, nbits=max(1, (length - 1).bit_length()), past_len=past_len),
        grid=(db,),
        in_specs=[pl.BlockSpec((None, rows, length), lambda b: (b, 0, 0))],
        out_specs=pl.BlockSpec((None, rows, length), lambda b: (b, 0, 0)),
        out_shape=jax.ShapeDtypeStruct(scores.shape, F32),
        scratch_shapes=[pltpu.VMEM((rows, 1), I32)],
        compiler_params=_cparams(("arbitrary",)),
        name="select_sample",
    )(scores)


def _flat_bf16(ref):
    x = ref[...]
    return x.reshape(x.shape[0] * x.shape[1], x.shape[2]).astype(BF16)


def _dsa_sample_kernel(pt_ref, q_ref, mask_ref, e_ref, valid_ref, kn_ref, vn_ref, *rest,
                       pp, n_steps):
    kpages, vpages = rest[:pp], rest[pp:2 * pp]
    o_ref, m_ref, l_ref, acc_ref = rest[2 * pp:]
    p = pl.program_id(1)
    q = q_ref[...]
    rows = q.shape[0]
    reps = rows // mask_ref.shape[0]
    valid = valid_ref[...]

    @pl.when(p == 0)
    def _():
        m_ref[...] = jnp.full(m_ref.shape, NEG, F32)
        l_ref[...] = jnp.zeros(l_ref.shape, F32)
        acc_ref[...] = jnp.zeros(acc_ref.shape, F32)

    def pages(krefs, vrefs):
        ss = []
        for j, kref in enumerate(krefs):
            s = lax.dot_general(q, _flat_bf16(kref), _NT, preferred_element_type=F32)
            mrows = jnp.concatenate([mask_ref[:, j * LANES:(j + 1) * LANES]] * reps, axis=0).astype(BF16)
            keep = jnp.dot(mrows, e_ref[...], preferred_element_type=F32) * valid > 0.5
            ss.append(jnp.where(keep, s, NEG))
        m = m_ref[...]
        m_new = m
        for s in ss:
            m_new = jnp.maximum(m_new, jnp.max(s, axis=1, keepdims=True))
        alpha = jnp.exp2(m - m_new)
        l = alpha * l_ref[...]
        acc = alpha * acc_ref[...]
        for s, vref in zip(ss, vrefs):
            pr = jnp.exp2(s - m_new)
            l = l + jnp.sum(pr, axis=1, keepdims=True)
            acc = acc + jnp.dot(pr.astype(BF16), _flat_bf16(vref), preferred_element_type=F32)
        l_ref[...] = l
        acc_ref[...] = acc
        m_ref[...] = m_new

    @pl.when(p < n_steps - 1)
    def _():
        pages(kpages, vpages)

    @pl.when(p == n_steps - 1)
    def _():
        pages([kn_ref], [vn_ref])
        o_ref[...] = (acc_ref[...] / l_ref[...]).astype(o_ref.dtype)


def _dsa_sample(cache_k, cache_v, layer, page_table, q_rows, mask, k_new_pad, v_new_pad, pp):
    db, n_pages = page_table.shape
    page, n_kv = cache_k.shape[2], cache_k.shape[3]
    rows = q_rows.shape[1]
    n_steps = n_pages // pp + 1
    expand, _, valid = _page_consts(page, n_kv, rows)

    def cache_spec(j):
        return pl.BlockSpec((None, None, page, n_kv, HEAD_DIM),
                            lambda b, p, pt: (layer, pt[b, jnp.minimum(p, n_steps - 2) * pp + j], 0, 0, 0))

    new_spec = pl.BlockSpec((None, page, n_kv, HEAD_DIM), lambda b, p, pt: (b, 0, 0, 0))
    mask_spec = pl.BlockSpec((None, mask.shape[1], pp * page),
                             lambda b, p, pt: (b, 0, jnp.where(p == n_steps - 1, n_pages // pp, p)))
    return pl.pallas_call(
        functools.partial(_dsa_sample_kernel, pp=pp, n_steps=n_steps),
        grid_spec=pltpu.PrefetchScalarGridSpec(
            num_scalar_prefetch=1,
            grid=(db, n_steps),
            in_specs=[pl.BlockSpec((None, rows, HEAD_DIM), lambda b, p, pt: (b, 0, 0)),
                      mask_spec,
                      pl.BlockSpec(expand.shape, lambda b, p, pt: (0, 0)),
                      pl.BlockSpec(valid.shape, lambda b, p, pt: (0, 0)),
                      new_spec, new_spec] + [cache_spec(j) for j in range(pp)] * 2,
            out_specs=pl.BlockSpec((None, rows, HEAD_DIM), lambda b, p, pt: (b, 0, 0)),
            scratch_shapes=[pltpu.VMEM((rows, 1), F32), pltpu.VMEM((rows, 1), F32), pltpu.VMEM((rows, HEAD_DIM), F32)]),
        out_shape=jax.ShapeDtypeStruct((db, rows, HEAD_DIM), BF16),
        compiler_params=_cparams(("arbitrary", "arbitrary")),
        name="dsa_sample",
    )(page_table, q_rows, mask, expand, valid, k_new_pad, v_new_pad, *([cache_k] * pp), *([cache_v] * pp))


def _sb_sample_kernel(pt_ref, q_ref, e_ref, c_ref, valid_ref, u_ref, kn_ref, vn_ref, tail_in_ref, acc_in_ref, *rest,
                      pp, top, with_new, past_len):
    kpages, vpages = rest[:pp], rest[pp:2 * pp]
    tail_ref, acc_ref = rest[2 * pp:]
    p = pl.program_id(1)
    q = q_ref[...]
    rows = q.shape[0]
    n_kv = valid_ref.shape[1] // e_ref.shape[0]
    valid = valid_ref[...]
    page_len = e_ref.shape[0]
    qpos = past_len + (lax.broadcasted_iota(I32, (rows, 1), 0) % (rows // n_kv)) // KV_GROUP

    @pl.when(p == 0)
    def _():
        tail_ref[...] = tail_in_ref[...]
        acc_ref[...] = acc_in_ref[...]

    def pages(krefs, vrefs, starts):
        zfs = [lax.dot_general(q, _flat_bf16(kref), _NT, preferred_element_type=F32) * valid for kref in krefs]
        zs = []
        for zf in zfs:
            hi, lo = _split_bf16(zf)
            zz = jnp.dot(jnp.concatenate([hi, lo], axis=0), c_ref[...], preferred_element_type=F32)
            zs.append(zz[:rows] + zz[rows:])
        parts = []
        tail = tail_ref[...]
        for z, start in zip(zs, starts):
            mask = (start + lax.broadcasted_iota(I32, (rows, page_len), 1)) < qpos
            sp = _softplus2(z)
            lk = jnp.where(mask, -sp, 0.0)
            hi, lo = _split_bf16(lk)
            ll = jnp.dot(jnp.concatenate([hi, lo], axis=0), u_ref[...], preferred_element_type=F32)
            parts.append((mask, z - sp + tail, ll))
            tail = tail + jnp.sum(lk, axis=1, keepdims=True)
        tail_ref[...] = tail
        acc = acc_ref[...]
        for (mask, base, ll), vref in zip(parts, vrefs):
            a = jnp.where(mask, jnp.exp2(base + ll[:rows] + ll[rows:]), 0.0).astype(BF16)
            ae = (jnp.dot(a, e_ref[...], preferred_element_type=F32) * valid).astype(BF16)
            acc = acc + jnp.dot(ae, _flat_bf16(vref), preferred_element_type=F32)
        acc_ref[...] = acc

    if with_new:
        @pl.when(p == 0)
        def _():
            pages([kn_ref], [vn_ref], [past_len])

    @pl.when(p >= int(with_new))
    def _():
        pages(kpages, vpages, [(top - 1 - ((p - int(with_new)) * pp + j)) * page_len for j in range(pp)])


def _sb_sample_part(cache_k, cache_v, layer, page_table, q_rows, k_new_pad, v_new_pad, tail, acc, pp, top, count, with_new):
    db, n_pages = page_table.shape
    page, n_kv = cache_k.shape[2], cache_k.shape[3]
    rows = q_rows.shape[1]
    first = int(with_new)
    expand, compact, valid = _page_consts(page, n_kv, rows)

    def cache_spec(j):
        return pl.BlockSpec((None, None, page, n_kv, HEAD_DIM),
                            lambda b, p, pt: (layer, pt[b, top - 1 - (jnp.maximum(p - first, 0) * pp + j)], 0, 0, 0))

    new_spec = pl.BlockSpec((None, page, n_kv, HEAD_DIM), lambda b, p, pt: (b, 0, 0, 0))
    const = lambda a: pl.BlockSpec(a.shape, lambda b, p, pt: (0, 0))
    per_b = lambda a: pl.BlockSpec((None,) + a.shape[1:], lambda b, p, pt: (b, 0, 0))
    u = _suffix_matrix(page)
    return pl.pallas_call(
        functools.partial(_sb_sample_kernel, pp=pp, top=top, with_new=with_new, past_len=n_pages * page),
        grid_spec=pltpu.PrefetchScalarGridSpec(
            num_scalar_prefetch=1,
            grid=(db, count // pp + first),
            in_specs=[per_b(q_rows), const(expand), const(compact), const(valid), const(u),
                      new_spec, new_spec, per_b(tail), per_b(acc)] + [cache_spec(j) for j in range(pp)] * 2,
            out_specs=[per_b(tail), per_b(acc)]),
        out_shape=[jax.ShapeDtypeStruct(tail.shape, F32), jax.ShapeDtypeStruct(acc.shape, F32)],
        compiler_params=_cparams(("arbitrary", "arbitrary")),
        name="sb_sample",
    )(page_table, q_rows, expand, compact, valid, u, k_new_pad, v_new_pad, tail, acc,
      *([cache_k] * pp), *([cache_v] * pp))


def _sb_sample(cache_k, cache_v, layer, page_table, q_rows, k_new_pad, v_new_pad, pp):
    db, n_pages = page_table.shape
    rows = q_rows.shape[1]
    args = (cache_k, cache_v, layer, page_table, q_rows, k_new_pad, v_new_pad)
    state = (jnp.zeros((db, rows, 1), F32), jnp.zeros((db, rows, HEAD_DIM), F32))
    state = tuple(_sb_sample_part(*args, *state, pp, n_pages, pp, True))
    rest = n_pages - pp
    if rest > 0:
        state = lax.cond(jnp.max(state[0]) > EXP2_UNDERFLOW,
                         lambda s: tuple(_sb_sample_part(*args, *s, pp, rest, rest, False)), lambda s: s, state)
    return state[1].astype(BF16)


def _rope_tables(pos):
    half = HEAD_DIM // 2
    inv = np.float32(ROPE_THETA) ** (-np.arange(half, dtype=np.float32) / np.float32(half))
    ang = pos.astype(np.float32)[:, None] * inv[None, :].astype(np.float32)
    cos, sin = np.cos(ang), np.sin(ang)
    return jnp.asarray(np.concatenate([cos, cos], axis=1)), jnp.asarray(np.concatenate([-sin, sin], axis=1))


def _pick_tile(n, prefs):
    for t in prefs:
        if n % t == 0:
            return t
    return n


def _rows_to_heads(x, db, t, n_kv):
    return x.reshape(db, t, n_kv, KV_GROUP, HEAD_DIM).transpose(0, 2, 1, 3, 4).reshape(db, n_kv * t * KV_GROUP, HEAD_DIM)


def _heads_to_rows(x, db, t, n_kv):
    return x.reshape(db, n_kv, t, KV_GROUP, HEAD_DIM).transpose(0, 2, 1, 3, 4).reshape(db * t, n_kv * KV_GROUP * HEAD_DIM)


def _chunk_t(v, nb, s_len, ck):
    return v.reshape(nb, s_len // ck, ck, v.shape[1]).transpose(0, 1, 3, 2)


def _pad_page(x, page):
    return jnp.pad(x, [(0, 0), (0, page - x.shape[1])] + [(0, 0)] * (x.ndim - 2))


def kernel(x_prompt, x_sample, c_prompt, c_sample, cache_k, cache_v, cache_idx_k, page_table, norm_mix_g, norm_ffn_g,
           w_ada, b_ada, w_in_dsa, w_in_sb, w_out, w_gate, w_up, w_down, norm_final_g):
    nb, s_len, d = x_prompt.shape
    db, t_new, _ = x_sample.shape
    depth = w_ada.shape[0]
    page, n_kv = cache_k.shape[2], cache_k.shape[3]
    n_pages = page_table.shape[1]
    past_len = n_pages * page
    attn_w = w_out.shape[1]
    kv_w = n_kv * HEAD_DIM
    idx_w = IDX_HEADS * IDX_DIM
    mp, ms = nb * s_len, db * t_new

    tm = _pick_tile(s_len, (1024, 512, 256, 128))
    tm2 = _pick_tile(s_len, (512, 256, 128))
    tm3 = _pick_tile(s_len, (256, 128))
    w_out16 = _cast_bf16(w_out, _pick_tile(w_out.shape[1], (512, 256, 128)))
    tq_dsa, tq_sb, ck = 256, 256, 256
    ng = _pick_tile(n_kv, (4, 2, 1))
    pp = _pick_tile(n_pages, (8, 4, 2, 1))
    attn_scale = HEAD_DIM ** -0.5 * LOG2E

    n_c = nb + db
    c_all = jnp.pad(jnp.concatenate([c_prompt, c_sample], axis=0), ((0, (-n_c) % 16), (0, 0)))
    mod = _adaln(c_all, w_ada, b_ada).reshape(depth, c_all.shape[0], 6, d)

    cos_p, sin_p = _rope_tables(np.arange(s_len))
    cos_s, sin_s = _rope_tables(np.tile(past_len + np.arange(t_new), db))

    xp = x_prompt.reshape(mp, d)
    xs = x_sample.reshape(ms, d)
    outs = {n: [] for n in ("ikp", "ks", "vs", "iks")}
    k_all = jnp.zeros((depth, mp, kv_w), F32)
    v_all = jnp.zeros((depth, mp, kv_w), F32)

    for l in range(depth):
        mp_l = [mod[l, :nb, j].reshape(nb, 1, d) for j in range(6)]
        ms_l = [jnp.repeat(mod[l, nb:n_c, j], t_new, axis=0) for j in range(6)]
        hp = _norm(xp, norm_mix_g[l], mp_l[1], mp_l[0], tm, s_len)
        hs = _norm(xs, norm_mix_g[l], ms_l[1], ms_l[0], ms, 1)
        i = l // 2
        if l % 2 == 0:
            w = jnp.swapaxes(w_in_dsa, 1, 2)
            tail0 = attn_w + 2 * kv_w + idx_w
            w_tail = jnp.pad(w[i, tail0:, :], ((0, 2 * LANES - IDX_DIM - IDX_HEADS), (0, 0)))[None]
            tn = 1024
            hd = tn // HEAD_DIM
            (q,) = _proj(hp, w, i, 0, attn_w, cos_p, sin_p, hd, (BF16,), tm2, tn, attn_scale, w_rows=True)
            k_all, k16 = _proj(hp, w, i, attn_w, kv_w, cos_p, sin_p, hd, (F32, BF16), tm2, tn, w_rows=True,
                               stack=(k_all, l))
            v_all, v16 = _proj(hp, w, i, attn_w + kv_w, kv_w, cos_p, sin_p, 0, (F32, BF16), tm2, tn, w_rows=True,
                               stack=(v_all, l))
            (iq,) = _proj(hp, w, i, attn_w + 2 * kv_w, idx_w, cos_p, sin_p, hd, (F32,), tm2, tn, w_rows=True)
            (ikiw,) = _proj(hp, w_tail, 0, 0, 2 * LANES, cos_p, sin_p, 1, (F32,), tm2, 2 * LANES, w_rows=True)
            mix_p = _dsa_prompt(q, k16, _chunk_t(v16, nb, s_len, ck), iq, ikiw, ikiw[:, IDX_DIM:IDX_DIM + IDX_HEADS].T,
                                nb, s_len, tq_dsa, ck, ng)
            outs["ikp"].append(ikiw[:, :IDX_DIM].reshape(nb, s_len, IDX_DIM))

            (all_s,) = _proj(hs, w, i, 0, tail0, cos_s, sin_s, hd, (F32,), ms, tn, plain_tiles=((attn_w + kv_w) // tn,),
                             w_rows=True)
            (ikiw_s,) = _proj(hs, w_tail, 0, 0, 2 * LANES, cos_s, sin_s, 1, (F32,), ms, 2 * LANES, w_rows=True)
            qs = (all_s[:, :attn_w] * attn_scale).astype(BF16)
            ks32 = all_s[:, attn_w:attn_w + kv_w]
            vs32 = all_s[:, attn_w + kv_w:attn_w + 2 * kv_w]
            iqs = all_s[:, attn_w + 2 * kv_w:]
            iks = ikiw_s[:, :IDX_DIM].reshape(db, t_new, IDX_DIM)
            iws = ikiw_s[:, IDX_DIM:IDX_DIM + IDX_HEADS].reshape(db, t_new, IDX_HEADS)
            iq_rows = jnp.broadcast_to(iqs.reshape(db, t_new, IDX_HEADS, 1, IDX_DIM).transpose(0, 2, 1, 3, 4),
                                       (db, IDX_HEADS, t_new, KV_GROUP, IDX_DIM)).reshape(db, -1, IDX_DIM)
            w_rows = jnp.broadcast_to(iws.transpose(0, 2, 1)[..., None], (db, IDX_HEADS, t_new, KV_GROUP)).reshape(db, -1, 1)
            sc_past, sc_new = _idx_sample(cache_idx_k, i, page_table, iq_rows, w_rows, _pad_page(iks, page),
                                          _pick_tile(n_pages, (16, 8, 4, 2, 1)))
            scores = jnp.concatenate([sc_past, sc_new, jnp.zeros((db, sc_new.shape[1], (pp - 1) * page), F32)], axis=2)
            mask = _select_sample(scores, past_len, t_new)
            ks4 = ks32.reshape(db, t_new, n_kv, HEAD_DIM)
            vs4 = vs32.reshape(db, t_new, n_kv, HEAD_DIM)
            mix_s = _dsa_sample(cache_k, cache_v, l, page_table, _rows_to_heads(qs, db, t_new, n_kv), mask,
                                _pad_page(ks4, page), _pad_page(vs4, page), pp)
            mix_s = _heads_to_rows(mix_s, db, t_new, n_kv)
            outs["iks"].append(iks)
        else:
            w = w_in_sb
            tn = 1024
            (q,) = _proj(hp, w, i, 0, attn_w, cos_p, sin_p, 0, (BF16,), tm2, tn, attn_scale)
            k_all, k16 = _proj(hp, w, i, attn_w, kv_w, cos_p, sin_p, 0, (F32, BF16), tm2, tn, stack=(k_all, l))
            v_all, v16 = _proj(hp, w, i, attn_w + kv_w, kv_w, cos_p, sin_p, 0, (F32, BF16), tm2, tn,
                               stack=(v_all, l))
            mix_p = _sb_prompt(q, k16, _chunk_t(v16, nb, s_len, ck), nb, s_len, tq_sb, ck, ng)
            (all_s,) = _proj(hs, w, i, 0, attn_w + 2 * kv_w, cos_s, sin_s, 0, (F32,), ms, tn)
            qs = (all_s[:, :attn_w] * attn_scale).astype(BF16)
            ks32 = all_s[:, attn_w:attn_w + kv_w]
            vs32 = all_s[:, attn_w + kv_w:]
            ks4 = ks32.reshape(db, t_new, n_kv, HEAD_DIM)
            vs4 = vs32.reshape(db, t_new, n_kv, HEAD_DIM)
            mix_s = _sb_sample(cache_k, cache_v, l, page_table, _rows_to_heads(qs, db, t_new, n_kv),
                               _pad_page(ks4, page), _pad_page(vs4, page), pp)
            mix_s = _heads_to_rows(mix_s, db, t_new, n_kv)
        outs["ks"].append(ks4)
        outs["vs"].append(vs4)

        xp, hp = _out_norm(mix_p, w_out16, l, xp, mp_l[2], norm_ffn_g[l], mp_l[4], mp_l[3], tm3, s_len)
        xs, hs = _out_norm(mix_s, w_out16, l, xs, ms_l[2], norm_ffn_g[l], ms_l[4], ms_l[3], ms, 1)
        tn_ff = _pick_tile(w_gate.shape[2], (512, 256, 128))
        act_p, act_s = _swiglu(hp, w_gate, w_up, l, tm, tn_ff, hs)
        xp, xs = _gres(act_p, w_down, l, xp, mp_l[5], tm2, 512, s_len, act_s, xs, ms_l[5])

    y_prompt = _final_norm(xp, norm_final_g, tm).reshape(nb, s_len, d)
    y_sample = _final_norm(xs, norm_final_g, ms).reshape(db, t_new, d)
    kv_shape = (depth, nb, s_len, n_kv, HEAD_DIM)
    return (y_prompt, y_sample, k_all.reshape(kv_shape), v_all.reshape(kv_shape), jnp.stack(outs["ikp"]),
            jnp.stack(outs["ks"]), jnp.stack(outs["vs"]), jnp.stack(outs["iks"]))
```

```python
import functools

import numpy as np
import jax
import jax.numpy as jnp
from jax import lax
from jax.experimental import pallas as pl
from jax.experimental.pallas import tpu as pltpu

F32 = jnp.float32
BF16 = jnp.bfloat16
I32 = jnp.int32

LANES = 128
HEAD_DIM = 128
KV_GROUP = 2
IDX_HEADS = 16
IDX_DIM = 128
TOPK_MAX = 256
ROPE_THETA = 10000.0
EPS = 1e-6
NEG = -1e30
INT_MIN = -2 ** 31
KEY_NEG_INF = -2139095041
ONES_ROWS = 16
LOG2E = 1.4426950408889634
EXP2_UNDERFLOW = -160.0
VMEM_LIMIT = 56 * 1024 * 1024

_NT = (((1,), (1,)), ((), ()))


def _cparams(sem):
    return pltpu.CompilerParams(dimension_semantics=sem, vmem_limit_bytes=VMEM_LIMIT)


def _split_bf16(x):
    hi = x.astype(BF16)
    lo = (x - hi.astype(F32)).astype(BF16)
    return hi, lo


def _sortable(x):
    bits = lax.bitcast_convert_type(x, I32)
    return bits ^ ((bits >> 31) & 0x7FFFFFFF)


def _adaln_kernel(c_ref, w_ref, b_ref, o_ref):
    c = c_ref[...]
    s = (c * jax.nn.sigmoid(c)).astype(BF16)
    o_ref[...] = jnp.dot(s, w_ref[...].astype(BF16), preferred_element_type=F32) + b_ref[...]


def _adaln(c_pad, w_ada, b_ada):
    depth, d, n = w_ada.shape
    r = c_pad.shape[0]
    tn = 1024
    return pl.pallas_call(
        _adaln_kernel,
        grid=(depth, n // tn),
        in_specs=[pl.BlockSpec((r, d), lambda l, j: (0, 0)),
                  pl.BlockSpec((None, d, tn), lambda l, j: (l, 0, j)),
                  pl.BlockSpec((None, 1, tn), lambda l, j: (l, 0, j))],
        out_specs=pl.BlockSpec((None, r, tn), lambda l, j: (l, 0, j)),
        out_shape=jax.ShapeDtypeStruct((depth, r, n), F32),
        compiler_params=_cparams(("arbitrary", "arbitrary")),
        name="adaln",
    )(c_pad, w_ada, b_ada.reshape(depth, 1, n))


def _norm_kernel(x_ref, g_ref, sc_ref, sh_ref, o_ref):
    x = x_ref[...]
    y = x * lax.rsqrt(jnp.mean(x * x, axis=-1, keepdims=True) + EPS) * g_ref[...]
    o_ref[...] = (y * (1.0 + sc_ref[...]) + sh_ref[...]).astype(o_ref.dtype)


def _final_norm_kernel(x_ref, g_ref, o_ref):
    x = x_ref[...]
    o_ref[...] = x * lax.rsqrt(jnp.mean(x * x, axis=-1, keepdims=True) + EPS) * g_ref[...]


def _mod_spec(mod, tm, width, rows_per_batch, col_of):
    if mod.ndim == 3:
        return pl.BlockSpec((None, 1, width), lambda *ij: ((ij[0] if col_of is None else ij[1]) * tm // rows_per_batch, 0,
                                                           0 if col_of is None else ij[0]))
    return pl.BlockSpec((tm, width), lambda *ij: ((ij[0] if col_of is None else ij[1]), 0 if col_of is None else ij[0]))


def _norm(x, g, sc, sh, tm, rows_per_batch):
    m, d = x.shape
    return pl.pallas_call(
        _norm_kernel,
        grid=(m // tm,),
        in_specs=[pl.BlockSpec((tm, d), lambda i: (i, 0)),
                  pl.BlockSpec((1, d), lambda i: (0, 0)),
                  _mod_spec(sc, tm, d, rows_per_batch, None),
                  _mod_spec(sh, tm, d, rows_per_batch, None)],
        out_specs=pl.BlockSpec((tm, d), lambda i: (i, 0)),
        out_shape=jax.ShapeDtypeStruct((m, d), BF16),
        compiler_params=_cparams(("arbitrary",)),
        name="norm_mod",
    )(x, g.reshape(1, d), sc, sh)


def _final_norm(x, g, tm):
    m, d = x.shape
    return pl.pallas_call(
        _final_norm_kernel,
        grid=(m // tm,),
        in_specs=[pl.BlockSpec((tm, d), lambda i: (i, 0)), pl.BlockSpec((1, d), lambda i: (0, 0))],
        out_specs=pl.BlockSpec((tm, d), lambda i: (i, 0)),
        out_shape=jax.ShapeDtypeStruct((m, d), F32),
        compiler_params=_cparams(("arbitrary",)),
        name="final_norm",
    )(x, g.reshape(1, d))


def _proj_kernel(h_ref, w_ref, cos_ref, sin_ref, *rest, rope, plain_tiles, n_out, scale, w_rows, n_skip):
    rest = rest[n_skip:]
    outs, wb_ref = rest[:n_out], rest[n_out]

    @pl.when(pl.program_id(1) == 0)
    def _():
        wb_ref[...] = w_ref[...].astype(BF16)

    if w_rows:
        acc = lax.dot_general(h_ref[...], wb_ref[...], _NT, preferred_element_type=F32)
    else:
        acc = jnp.dot(h_ref[...], wb_ref[...], preferred_element_type=F32)
    if scale != 1.0:
        acc = acc * scale
    tn = acc.shape[1]

    def emit(n_rope):
        if not n_rope:
            for o in outs:
                o[...] = acc.astype(o.dtype)
            return
        cos = cos_ref[...]
        sin = sin_ref[...]
        for c in range(tn // HEAD_DIM):
            y = acc[:, c * HEAD_DIM:(c + 1) * HEAD_DIM]
            if c < n_rope:
                y = y * cos + pltpu.roll(y, HEAD_DIM // 2, 1) * sin
            for o in outs:
                o[:, c * HEAD_DIM:(c + 1) * HEAD_DIM] = y.astype(o.dtype)

    if rope and plain_tiles:
        j = pl.program_id(0)
        plain = functools.reduce(jnp.logical_or, [j == t for t in plain_tiles])
        pl.when(plain)(lambda: emit(0))
        pl.when(jnp.logical_not(plain))(lambda: emit(rope))
    else:
        emit(rope)


def _proj(h, w, layer, col0, ncols, cos, sin, rope, out_dtypes, tm, tn, scale=1.0, plain_tiles=(), w_rows=False,
          stack=None):
    m, k = h.shape
    npb = cos.shape[0] // tm
    j0 = col0 // tn
    n_out = len(out_dtypes)
    if w_rows:
        w_spec = pl.BlockSpec((None, tn, k), lambda j, i: (layer, j + j0, 0))
    else:
        w_spec = pl.BlockSpec((None, k, tn), lambda j, i: (layer, 0, j + j0))
    in_specs = [pl.BlockSpec((tm, k), lambda j, i: (i, 0)),
                w_spec,
                pl.BlockSpec((tm, HEAD_DIM), lambda j, i: (i % npb, 0)),
                pl.BlockSpec((tm, HEAD_DIM), lambda j, i: (i % npb, 0))]
    out_specs = [pl.BlockSpec((tm, tn), lambda j, i: (i, j)) for _ in out_dtypes]
    out_shape = [jax.ShapeDtypeStruct((m, ncols), dt) for dt in out_dtypes]
    args, aliases = [h, w, cos, sin], {}
    if stack is not None:
        buf, slot = stack
        out_specs[0] = pl.BlockSpec((None, tm, tn), lambda j, i: (slot, i, j))
        out_shape[0] = jax.ShapeDtypeStruct(buf.shape, buf.dtype)
        in_specs.append(pl.BlockSpec(memory_space=pl.ANY))
        args.append(buf)
        aliases = {len(args) - 1: 0}
    return pl.pallas_call(
        functools.partial(_proj_kernel, rope=rope, plain_tiles=tuple(plain_tiles), n_out=n_out, scale=scale,
                          w_rows=w_rows, n_skip=len(aliases)),
        grid=(ncols // tn, m // tm),
        in_specs=in_specs,
        out_specs=out_specs,
        out_shape=out_shape,
        scratch_shapes=[pltpu.VMEM((tn, k) if w_rows else (k, tn), BF16)],
        input_output_aliases=aliases,
        compiler_params=_cparams(("arbitrary", "arbitrary")),
        name="proj_rope" if rope else "proj",
    )(*args)


def _gres_kernel(h_ref, w_ref, x_ref, g_ref, hs_ref, xs_ref, gs_ref, o_ref, os_ref, wb_ref):
    @pl.when(pl.program_id(1) == 0)
    def _():
        wb_ref[...] = w_ref[...].astype(BF16)
        os_ref[...] = xs_ref[...] + gs_ref[...] * jnp.dot(hs_ref[...], wb_ref[...], preferred_element_type=F32)

    acc = jnp.dot(h_ref[...], wb_ref[...], preferred_element_type=F32)
    o_ref[...] = x_ref[...] + g_ref[...] * acc


def _gres(h, w, layer, x, gate, tm, tn, rows_per_batch, h_s, x_s, gate_s):
    m, k = h.shape
    n = w.shape[2]
    m_s = h_s.shape[0]
    return pl.pallas_call(
        _gres_kernel,
        grid=(n // tn, m // tm),
        in_specs=[pl.BlockSpec((tm, k), lambda j, i: (i, 0)),
                  pl.BlockSpec((None, k, tn), lambda j, i: (layer, 0, j)),
                  pl.BlockSpec((tm, tn), lambda j, i: (i, j)),
                  _mod_spec(gate, tm, tn, rows_per_batch, True),
                  pl.BlockSpec((m_s, k), lambda j, i: (0, 0)),
                  pl.BlockSpec((m_s, tn), lambda j, i: (0, j)),
                  pl.BlockSpec((m_s, tn), lambda j, i: (0, j))],
        out_specs=[pl.BlockSpec((tm, tn), lambda j, i: (i, j)), pl.BlockSpec((m_s, tn), lambda j, i: (0, j))],
        out_shape=[jax.ShapeDtypeStruct((m, n), F32), jax.ShapeDtypeStruct((m_s, n), F32)],
        scratch_shapes=[pltpu.VMEM((k, tn), BF16)],
        compiler_params=_cparams(("arbitrary", "arbitrary")),
        name="gated_residual",
    )(h, w, x, gate, h_s, x_s, gate_s)


def _cast_kernel(w_ref, o_ref):
    o_ref[...] = w_ref[...].astype(o_ref.dtype)


def _cast_bf16(w, tk):
    depth, k, n = w.shape
    return pl.pallas_call(
        _cast_kernel,
        grid=(depth, k // tk),
        in_specs=[pl.BlockSpec((None, tk, n), lambda l, i: (l, i, 0))],
        out_specs=pl.BlockSpec((None, tk, n), lambda l, i: (l, i, 0)),
        out_shape=jax.ShapeDtypeStruct(w.shape, BF16),
        compiler_params=_cparams(("arbitrary", "arbitrary")),
        name="cast_bf16",
    )(w)


def _out_norm_kernel(m_ref, w_ref, x_ref, gate_ref, g_ref, sc_ref, sh_ref, xo_ref, ho_ref):
    x = x_ref[...] + gate_ref[...] * jnp.dot(m_ref[...], w_ref[...], preferred_element_type=F32)
    xo_ref[...] = x
    y = x * lax.rsqrt(jnp.mean(x * x, axis=-1, keepdims=True) + EPS) * g_ref[...]
    ho_ref[...] = (y * (1.0 + sc_ref[...]) + sh_ref[...]).astype(ho_ref.dtype)


def _out_norm(mix, wb, layer, x, gate, g, sc, sh, tm, rows_per_batch):
    m, k = mix.shape
    d = wb.shape[2]
    row = lambda i: (i, 0)
    return pl.pallas_call(
        _out_norm_kernel,
        grid=(m // tm,),
        in_specs=[pl.BlockSpec((tm, k), row),
                  pl.BlockSpec((None, k, d), lambda i: (layer, 0, 0)),
                  pl.BlockSpec((tm, d), row),
                  _mod_spec(gate, tm, d, rows_per_batch, None),
                  pl.BlockSpec((1, d), lambda i: (0, 0)),
                  _mod_spec(sc, tm, d, rows_per_batch, None),
                  _mod_spec(sh, tm, d, rows_per_batch, None)],
        out_specs=[pl.BlockSpec((tm, d), row), pl.BlockSpec((tm, d), row)],
        out_shape=[jax.ShapeDtypeStruct((m, d), F32), jax.ShapeDtypeStruct((m, d), BF16)],
        compiler_params=_cparams(("arbitrary",)),
        name="out_proj_norm",
    )(mix, wb, x, gate, g.reshape(1, d), sc, sh)


def _swiglu_kernel(h_ref, wg_ref, wu_ref, hs_ref, o_ref, os_ref, wgb_ref, wub_ref):
    def act(h):
        a = jnp.dot(h, wgb_ref[...], preferred_element_type=F32)
        b = jnp.dot(h, wub_ref[...], preferred_element_type=F32)
        return (a * jax.nn.sigmoid(a) * b).astype(BF16)

    @pl.when(pl.program_id(1) == 0)
    def _():
        wgb_ref[...] = wg_ref[...].astype(BF16)
        wub_ref[...] = wu_ref[...].astype(BF16)
        os_ref[...] = act(hs_ref[...])

    o_ref[...] = act(h_ref[...])


def _swiglu(h, wg, wu, layer, tm, tn, h_s):
    m, k = h.shape
    n = wg.shape[2]
    m_s = h_s.shape[0]
    return pl.pallas_call(
        _swiglu_kernel,
        grid=(n // tn, m // tm),
        in_specs=[pl.BlockSpec((tm, k), lambda j, i: (i, 0)),
                  pl.BlockSpec((None, k, tn), lambda j, i: (layer, 0, j)),
                  pl.BlockSpec((None, k, tn), lambda j, i: (layer, 0, j)),
                  pl.BlockSpec((m_s, k), lambda j, i: (0, 0))],
        out_specs=[pl.BlockSpec((tm, tn), lambda j, i: (i, j)), pl.BlockSpec((m_s, tn), lambda j, i: (0, j))],
        out_shape=[jax.ShapeDtypeStruct((m, n), BF16), jax.ShapeDtypeStruct((m_s, n), BF16)],
        scratch_shapes=[pltpu.VMEM((k, tn), BF16), pltpu.VMEM((k, tn), BF16)],
        compiler_params=_cparams(("arbitrary", "arbitrary")),
        name="swiglu",
    )(h, wg, wu, h_s)


def _kth_threshold(count_ge, topk, shape):
    t0 = jnp.where(count_ge(jnp.zeros(shape, I32)) >= topk, 0, INT_MIN).astype(I32)

    def body(it, t):
        cand = t | jnp.left_shift(jnp.int32(1), 30 - it)
        return jnp.where(count_ge(cand) >= topk, cand, t)

    return lax.fori_loop(0, 31, body, t0)


def _tie_cutoff(count_tie_below, need, nbits, shape):
    def body(it, a):
        cand = a | jnp.left_shift(jnp.int32(1), nbits - 1 - it)
        return jnp.where(count_tie_below(cand) < need, cand, a)

    return lax.fori_loop(0, nbits, body, jnp.zeros(shape, I32))


def _dsa_prompt_kernel(q_ref, k_ref, vt_ref, iq_ref, ik_ref, iwt_ref, o_ref,
                       iqcat_ref, keys_ref, bias_ref, cut_ref, m_ref, acc_ref, q2_ref,
                       *, tq, ck, ng, topk, nbits, idx_scale):
    i = pl.program_id(1)
    nkc = ((i + 1) * tq + ck - 1) // ck
    qpos = i * tq + lax.broadcasted_iota(I32, (1, tq), 1)

    def kpos_of(c):
        return c * ck + lax.broadcasted_iota(I32, (ck, tq), 0)

    @pl.when(pl.program_id(2) == 0)
    def _index():
        for h in range(IDX_HEADS):
            hi, lo = _split_bf16(iq_ref[:, h * IDX_DIM:(h + 1) * IDX_DIM])
            iqcat_ref[:, h * 2 * IDX_DIM:h * 2 * IDX_DIM + IDX_DIM] = hi
            iqcat_ref[:, h * 2 * IDX_DIM + IDX_DIM:(h + 1) * 2 * IDX_DIM] = lo

        def score_chunk(c, carry):
            ikh = ik_ref[pl.ds(pl.multiple_of(c * ck, ck), ck), :].astype(BF16)
            ikcat = jnp.concatenate([ikh, ikh], axis=1)
            acc = jnp.zeros((ck, tq), F32)
            for h in range(IDX_HEADS):
                lg = lax.dot_general(ikcat, iqcat_ref[:, h * 2 * IDX_DIM:(h + 1) * 2 * IDX_DIM], _NT,
                                     preferred_element_type=F32)
                acc = acc + iwt_ref[h:h + 1, :] * jnp.maximum(lg, 0.0)
            sc = jnp.where(kpos_of(c) <= qpos, acc * idx_scale, -jnp.inf)
            keys_ref[c] = _sortable(sc)
            return carry

        lax.fori_loop(0, nkc, score_chunk, 0)

        def count(pred):
            def body(c, part):
                m = jnp.where(pred(keys_ref[c], c), 1.0, 0.0)
                return part + jnp.sum(m.reshape(ck // 32, 32, tq), axis=0)
            part = lax.fori_loop(0, nkc, body, jnp.zeros((32, tq), F32))
            return jnp.sum(part, axis=0, keepdims=True)

        thr = _kth_threshold(lambda cand: count(lambda kc, c: kc >= cand), float(topk), (1, tq))
        n_gt = count(lambda kc, c: kc > thr)
        n_ge = count(lambda kc, c: kc >= thr)
        need = float(topk) - n_gt
        crowded = (n_ge > float(topk)) & (thr > KEY_NEG_INF)
        cut_ref[...] = jnp.full((1, tq), 2 ** 30, I32)

        @pl.when(jnp.max(jnp.where(crowded, 1.0, 0.0)) > 0.5)
        def _():
            cut_ref[...] = _tie_cutoff(lambda cand: count(lambda kc, c: (kc == thr) & (kpos_of(c) < cand)),
                                       need, nbits, (1, tq))

        cut = cut_ref[...]

        def bias_chunk(c, carry):
            kc = keys_ref[c]
            kpos = kpos_of(c)
            sel = ((kc > thr) | ((kc == thr) & (kpos <= cut))) & (kpos <= qpos)
            bias_ref[c] = jnp.where(sel, 0.0, NEG).astype(BF16)
            return carry

        lax.fori_loop(0, nkc, bias_chunk, 0)

    m_ref[...] = jnp.full(m_ref.shape, NEG, F32)
    acc_ref[...] = jnp.zeros(acc_ref.shape, F32)
    for g in range(ng):
        for j in range(KV_GROUP):
            h = g * KV_GROUP + j
            q2_ref[g, j * tq:(j + 1) * tq, :] = q_ref[:, h * HEAD_DIM:(h + 1) * HEAD_DIM]
    ones = jnp.ones((ONES_ROWS, ck), BF16)

    def attend(c, carry):
        off = pl.multiple_of(c * ck, ck)
        b = bias_ref[c]
        b2 = jnp.concatenate([b] * KV_GROUP, axis=1)

        def scores(g):
            s = lax.dot_general(k_ref[pl.ds(off, ck), g * HEAD_DIM:(g + 1) * HEAD_DIM], q2_ref[g], _NT,
                                preferred_element_type=F32)
            return s.astype(BF16) + b2

        def softmax(g, s):
            m = m_ref[g]
            m_new = jnp.maximum(m, jnp.max(s, axis=0, keepdims=True).astype(F32))
            m_ref[g] = m_new
            return jnp.exp2(s - m_new.astype(BF16)), jnp.exp2(m - m_new)

        def weigh(g, p, alpha):
            vt = jnp.concatenate([vt_ref[c, g * HEAD_DIM:(g + 1) * HEAD_DIM, :], ones], axis=0)
            acc_ref[g] = alpha * acc_ref[g] + jnp.dot(vt, p, preferred_element_type=F32)

        ahead = 3
        s = {g: scores(g) for g in range(min(ahead, ng))}
        for g in range(ng):
            p, alpha = softmax(g, s.pop(g))
            if g + ahead < ng:
                s[g + ahead] = scores(g + ahead)
            weigh(g, p, alpha)
        return carry

    lax.fori_loop(0, nkc, attend, 0)
    for g in range(ng):
        out = acc_ref[g, :HEAD_DIM] * (1.0 / acc_ref[g, HEAD_DIM:HEAD_DIM + 1])
        for j in range(KV_GROUP):
            h = g * KV_GROUP + j
            o_ref[:, h * HEAD_DIM:(h + 1) * HEAD_DIM] = out[:, j * tq:(j + 1) * tq].T.astype(o_ref.dtype)


def _dsa_prompt(q, k, vt, iq, ikiw, iwt, nb, s_len, tq, ck, ng):
    n_kv = k.shape[1] // HEAD_DIM
    nq = s_len // tq
    topk = min(TOPK_MAX, s_len // 4)
    nh = ng * KV_GROUP
    kern = functools.partial(_dsa_prompt_kernel, tq=tq, ck=ck, ng=ng, topk=topk, nbits=max(1, (s_len - 1).bit_length()),
                             idx_scale=(IDX_HEADS ** -0.5) * (IDX_DIM ** -0.5))
    return pl.pallas_call(
        kern,
        grid=(nb, nq, n_kv // ng),
        in_specs=[pl.BlockSpec((tq, nh * HEAD_DIM), lambda b, i, g: (b * nq + i, g)),
                  pl.BlockSpec((s_len, ng * HEAD_DIM), lambda b, i, g: (b, g)),
                  pl.BlockSpec((None, s_len // ck, ng * HEAD_DIM, ck), lambda b, i, g: (b, 0, g, 0)),
                  pl.BlockSpec((tq, IDX_HEADS * IDX_DIM), lambda b, i, g: (b * nq + i, 0)),
                  pl.BlockSpec((s_len, IDX_DIM), lambda b, i, g: (b, 0)),
                  pl.BlockSpec((IDX_HEADS, tq), lambda b, i, g: (0, b * nq + i))],
        out_specs=pl.BlockSpec((tq, nh * HEAD_DIM), lambda b, i, g: (b * nq + i, g)),
        out_shape=jax.ShapeDtypeStruct(q.shape, BF16),
        scratch_shapes=[pltpu.VMEM((tq, IDX_HEADS * 2 * IDX_DIM), BF16),
                        pltpu.VMEM((s_len // ck, ck, tq), I32),
                        pltpu.VMEM((s_len // ck, ck, tq), BF16),
                        pltpu.VMEM((1, tq), I32),
                        pltpu.VMEM((ng, 1, KV_GROUP * tq), F32),
                        pltpu.VMEM((ng, HEAD_DIM + ONES_ROWS, KV_GROUP * tq), F32),
                        pltpu.VMEM((ng, KV_GROUP * tq, HEAD_DIM), BF16)],
        compiler_params=_cparams(("arbitrary", "arbitrary", "arbitrary")),
        name="dsa_prompt",
    )(q, k, vt, iq, ikiw, iwt)


def _softplus2(z):
    return jnp.maximum(z, 0.0) + jnp.log2(1.0 + jnp.exp2(-jnp.abs(z)))


def _sb_prompt_kernel(q_ref, k_ref, vt_ref, ln_ref, o_ref, tail_ref, acc_ref, *, tq, ck, ng):
    i = pl.program_id(2)
    nkc = ((i + 1) * tq + ck - 1) // ck
    nh = ng * KV_GROUP
    qpos = i * tq + lax.broadcasted_iota(I32, (1, tq), 1)
    tail_ref[...] = jnp.zeros(tail_ref.shape, F32)
    acc_ref[...] = jnp.zeros(acc_ref.shape, F32)

    def chunk(c, masked):
        off = pl.multiple_of(c * ck, ck)
        mask = (c * ck + lax.broadcasted_iota(I32, (ck, tq), 0)) < qpos if masked else None

        def logits(h):
            g = h // KV_GROUP
            return lax.dot_general(k_ref[pl.ds(off, ck), g * HEAD_DIM:(g + 1) * HEAD_DIM],
                                   q_ref[:, h * HEAD_DIM:(h + 1) * HEAD_DIM], _NT, preferred_element_type=F32)

        def keep(h, z):
            sp = jnp.where(z > 64.0, z, jnp.log2(1.0 + jnp.exp2(z)))
            spm = jnp.where(mask, sp, 0.0) if masked else sp
            la = jnp.dot(ln_ref[...], spm.astype(BF16), preferred_element_type=F32)
            tail = tail_ref[h]
            tail_ref[h] = tail + la[ck:ck + 1]
            return z - sp + tail, la

        def weigh(h, base, la):
            g = h // KV_GROUP
            a = jnp.exp2(base + la[:ck])
            if masked:
                a = jnp.where(mask, a, 0.0)
            acc_ref[h] += jnp.dot(vt_ref[c, g * HEAD_DIM:(g + 1) * HEAD_DIM, :], a.astype(BF16),
                                  preferred_element_type=F32)

        a1, a2 = 2, 4
        z = {h: logits(h) for h in range(min(a2, nh))}
        kept = {h: keep(h, z.pop(h)) for h in range(min(a1, nh))}
        for h in range(nh):
            if h + a1 < nh:
                kept[h + a1] = keep(h + a1, z.pop(h + a1))
            if h + a2 < nh:
                z[h + a2] = logits(h + a2)
            weigh(h, *kept.pop(h))

    chunk(nkc - 1, True)

    def live(carry):
        r, top = carry
        return (r < nkc) & (top > EXP2_UNDERFLOW)

    def body(carry):
        r, _ = carry
        chunk(nkc - 1 - r, False)
        return r + 1, jnp.max(tail_ref[...])

    lax.while_loop(live, body, (jnp.int32(1), jnp.max(tail_ref[...])))
    for h in range(nh):
        o_ref[:, h * HEAD_DIM:(h + 1) * HEAD_DIM] = acc_ref[h].T.astype(o_ref.dtype)


def _suffix_matrix(n):
    return jnp.asarray(np.tril(np.ones((n, n), np.float32), -1), dtype=BF16)


def _sb_prompt(q, k, vt, nb, s_len, tq, ck, ng):
    assert ck % tq == 0
    n_kv = k.shape[1] // HEAD_DIM
    nq = s_len // tq
    gw = KV_GROUP * HEAD_DIM
    ln = jnp.concatenate([-_suffix_matrix(ck).T, -jnp.ones((8, ck), BF16)], axis=0)
    return pl.pallas_call(
        functools.partial(_sb_prompt_kernel, tq=tq, ck=ck, ng=ng),
        grid=(nb, n_kv // ng, nq),
        in_specs=[pl.BlockSpec((tq, ng * gw), lambda b, g, i: (b * nq + i, g)),
                  pl.BlockSpec((s_len, ng * HEAD_DIM), lambda b, g, i: (b, g)),
                  pl.BlockSpec((None, s_len // ck, ng * HEAD_DIM, ck), lambda b, g, i: (b, 0, g, 0)),
                  pl.BlockSpec((ck + 8, ck), lambda b, g, i: (0, 0))],
        out_specs=pl.BlockSpec((tq, ng * gw), lambda b, g, i: (b * nq + i, g)),
        out_shape=jax.ShapeDtypeStruct(q.shape, BF16),
        scratch_shapes=[pltpu.VMEM((ng * KV_GROUP, 1, tq), F32),
                        pltpu.VMEM((ng * KV_GROUP, HEAD_DIM, tq), F32)],
        compiler_params=_cparams(("arbitrary", "arbitrary", "arbitrary")),
        name="sb_prompt",
    )(q, k, vt, ln)


def _page_consts(page, n_kv, rows):
    lane = np.arange(page * n_kv)
    expand = (lane[None, :] // n_kv == np.arange(page)[:, None]).astype(np.float32)
    valid = (lane[None, :] % n_kv == (np.arange(rows)[:, None] // (rows // n_kv))).astype(np.float32)
    return jnp.asarray(expand, BF16), jnp.asarray(expand.T.copy(), BF16), jnp.asarray(valid, F32)


def _idx_sample_kernel(pt_ref, iq_ref, w_ref, ikn_ref, *rest, pp, idx_scale):
    pages, (past_ref, new_ref) = rest[:pp], rest[pp:]
    hi, lo = _split_bf16(iq_ref[...])
    iqcat = jnp.concatenate([hi, lo], axis=1)
    w = w_ref[...]
    rows = iqcat.shape[0]

    def logits(ik):
        ikh = ik.astype(BF16)
        return lax.dot_general(iqcat, jnp.concatenate([ikh, ikh], axis=1), _NT, preferred_element_type=F32)

    def score(lg):
        x = jnp.maximum(lg, 0.0) * w
        return jnp.sum(x.reshape(IDX_HEADS, rows // IDX_HEADS, x.shape[1]), axis=0) * idx_scale

    lgs = [logits(pages[j][...]) for j in range(pp)]
    for j in range(pp):
        past_ref[:, j * LANES:(j + 1) * LANES] = score(lgs[j])

    @pl.when(pl.program_id(1) == 0)
    def _():
        new_ref[...] = score(logits(ikn_ref[...]))


def _idx_sample(cache_idx_k, layer, page_table, iq_rows, w_rows, ik_new_pad, pp):
    db, n_pages = page_table.shape
    page = cache_idx_k.shape[2]
    rows = iq_rows.shape[1]
    r8 = rows // IDX_HEADS
    page_specs = [pl.BlockSpec((None, None, page, IDX_DIM),
                               functools.partial(lambda b, p, pt, j: (layer, pt[b, p * pp + j], 0, 0), j=j))
                  for j in range(pp)]
    return pl.pallas_call(
        functools.partial(_idx_sample_kernel, pp=pp, idx_scale=(IDX_HEADS ** -0.5) * (IDX_DIM ** -0.5)),
        grid_spec=pltpu.PrefetchScalarGridSpec(
            num_scalar_prefetch=1,
            grid=(db, n_pages // pp),
            in_specs=[pl.BlockSpec((None, rows, IDX_DIM), lambda b, p, pt: (b, 0, 0)),
                      pl.BlockSpec((None, rows, 1), lambda b, p, pt: (b, 0, 0)),
                      pl.BlockSpec((None, page, IDX_DIM), lambda b, p, pt: (b, 0, 0))] + page_specs,
            out_specs=[pl.BlockSpec((None, r8, pp * page), lambda b, p, pt: (b, 0, p)),
                       pl.BlockSpec((None, r8, page), lambda b, p, pt: (b, 0, 0))]),
        out_shape=[jax.ShapeDtypeStruct((db, r8, n_pages * page), F32),
                   jax.ShapeDtypeStruct((db, r8, page), F32)],
        compiler_params=_cparams(("arbitrary", "arbitrary")),
        name="idx_sample",
    )(page_table, iq_rows, w_rows, ik_new_pad, *([cache_idx_k] * pp))


def _select_sample_kernel(s_ref, o_ref, cut_ref, *, topk, nbits, past_len):
    sc = s_ref[...]
    rows, length = sc.shape
    qpos = past_len + lax.broadcasted_iota(I32, (rows, 1), 0) // KV_GROUP
    kpos = lax.broadcasted_iota(I32, (rows, length), 1)
    causal = kpos <= qpos
    key = _sortable(jnp.where(causal, sc, -jnp.inf))

    n_acc = 4

    def count(pred_of):
        accs = [jnp.zeros((rows, LANES), F32) for _ in range(n_acc)]
        for j in range(length // LANES):
            sl = slice(j * LANES, (j + 1) * LANES)
            accs[j % n_acc] = accs[j % n_acc] + jnp.where(pred_of(key[:, sl], kpos[:, sl]), 1.0, 0.0)
        return jnp.sum((accs[0] + accs[1]) + (accs[2] + accs[3]), axis=1, keepdims=True)

    thr = _kth_threshold(lambda cand: count(lambda k, p: k >= cand), float(topk), (rows, 1))
    need = float(topk) - count(lambda k, p: k > thr)
    crowded = (count(lambda k, p: k >= thr) > float(topk)) & (thr > KEY_NEG_INF)
    cut_ref[...] = jnp.full((rows, 1), 2 ** 30, I32)

    @pl.when(jnp.max(jnp.where(crowded, 1.0, 0.0)) > 0.5)
    def _():
        cut_ref[...] = _tie_cutoff(lambda cand: count(lambda k, p: (k == thr) & (p < cand)), need, nbits, (rows, 1))

    sel = ((key > thr) | ((key == thr) & (kpos <= cut_ref[...]))) & causal
    o_ref[...] = jnp.where(sel, 1.0, 0.0)


def _select_sample(scores, past_len, n_new):
    db, rows, length = scores.shape
    topk = min(TOPK_MAX, (past_len + n_new) // 4)
    return pl.pallas_call(
        functools.partial(_select_sample_kernel, topk=topk, nbits=max(1, (length - 1).bit_length()), past_len=past_len),
        grid=(db,),
        in_specs=[pl.BlockSpec((None, rows, length), lambda b: (b, 0, 0))],
        out_specs=pl.BlockSpec((None, rows, length), lambda b: (b, 0, 0)),
        out_shape=jax.ShapeDtypeStruct(scores.shape, F32),
        scratch_shapes=[pltpu.VMEM((rows, 1), I32)],
        compiler_params=_cparams(("arbitrary",)),
        name="select_sample",
    )(scores)


def _flat_bf16(ref):
    x = ref[...]
    return x.reshape(x.shape[0] * x.shape[1], x.shape[2]).astype(BF16)


def _dsa_sample_kernel(pt_ref, q_ref, mask_ref, e_ref, valid_ref, kn_ref, vn_ref, *rest,
                       pp, n_steps):
    kpages, vpages = rest[:pp], rest[pp:2 * pp]
    o_ref, m_ref, l_ref, acc_ref = rest[2 * pp:]
    p = pl.program_id(1)
    q = q_ref[...]
    rows = q.shape[0]
    reps = rows // mask_ref.shape[0]
    valid = valid_ref[...]

    @pl.when(p == 0)
    def _():
        m_ref[...] = jnp.full(m_ref.shape, NEG, F32)
        l_ref[...] = jnp.zeros(l_ref.shape, F32)
        acc_ref[...] = jnp.zeros(acc_ref.shape, F32)

    def pages(krefs, vrefs):
        ss = []
        for j, kref in enumerate(krefs):
            s = lax.dot_general(q, _flat_bf16(kref), _NT, preferred_element_type=F32)
            mrows = jnp.concatenate([mask_ref[:, j * LANES:(j + 1) * LANES]] * reps, axis=0).astype(BF16)
            keep = jnp.dot(mrows, e_ref[...], preferred_element_type=F32) * valid > 0.5
            ss.append(jnp.where(keep, s, NEG))
        m = m_ref[...]
        m_new = m
        for s in ss:
            m_new = jnp.maximum(m_new, jnp.max(s, axis=1, keepdims=True))
        alpha = jnp.exp2(m - m_new)
        l = alpha * l_ref[...]
        acc = alpha * acc_ref[...]
        for s, vref in zip(ss, vrefs):
            pr = jnp.exp2(s - m_new)
            l = l + jnp.sum(pr, axis=1, keepdims=True)
            acc = acc + jnp.dot(pr.astype(BF16), _flat_bf16(vref), preferred_element_type=F32)
        l_ref[...] = l
        acc_ref[...] = acc
        m_ref[...] = m_new

    @pl.when(p < n_steps - 1)
    def _():
        pages(kpages, vpages)

    @pl.when(p == n_steps - 1)
    def _():
        pages([kn_ref], [vn_ref])
        o_ref[...] = (acc_ref[...] / l_ref[...]).astype(o_ref.dtype)


def _dsa_sample(cache_k, cache_v, layer, page_table, q_rows, mask, k_new_pad, v_new_pad, pp):
    db, n_pages = page_table.shape
    page, n_kv = cache_k.shape[2], cache_k.shape[3]
    rows = q_rows.shape[1]
    n_steps = n_pages // pp + 1
    expand, _, valid = _page_consts(page, n_kv, rows)

    def cache_spec(j):
        return pl.BlockSpec((None, None, page, n_kv, HEAD_DIM),
                            lambda b, p, pt: (layer, pt[b, jnp.minimum(p, n_steps - 2) * pp + j], 0, 0, 0))

    new_spec = pl.BlockSpec((None, page, n_kv, HEAD_DIM), lambda b, p, pt: (b, 0, 0, 0))
    mask_spec = pl.BlockSpec((None, mask.shape[1], pp * page),
                             lambda b, p, pt: (b, 0, jnp.where(p == n_steps - 1, n_pages // pp, p)))
    return pl.pallas_call(
        functools.partial(_dsa_sample_kernel, pp=pp, n_steps=n_steps),
        grid_spec=pltpu.PrefetchScalarGridSpec(
            num_scalar_prefetch=1,
            grid=(db, n_steps),
            in_specs=[pl.BlockSpec((None, rows, HEAD_DIM), lambda b, p, pt: (b, 0, 0)),
                      mask_spec,
                      pl.BlockSpec(expand.shape, lambda b, p, pt: (0, 0)),
                      pl.BlockSpec(valid.shape, lambda b, p, pt: (0, 0)),
                      new_spec, new_spec] + [cache_spec(j) for j in range(pp)] * 2,
            out_specs=pl.BlockSpec((None, rows, HEAD_DIM), lambda b, p, pt: (b, 0, 0)),
            scratch_shapes=[pltpu.VMEM((rows, 1), F32), pltpu.VMEM((rows, 1), F32), pltpu.VMEM((rows, HEAD_DIM), F32)]),
        out_shape=jax.ShapeDtypeStruct((db, rows, HEAD_DIM), BF16),
        compiler_params=_cparams(("arbitrary", "arbitrary")),
        name="dsa_sample",
    )(page_table, q_rows, mask, expand, valid, k_new_pad, v_new_pad, *([cache_k] * pp), *([cache_v] * pp))


def _sb_sample_kernel(pt_ref, q_ref, e_ref, c_ref, valid_ref, u_ref, kn_ref, vn_ref, tail_in_ref, acc_in_ref, *rest,
                      pp, top, with_new, past_len):
    kpages, vpages = rest[:pp], rest[pp:2 * pp]
    tail_ref, acc_ref = rest[2 * pp:]
    p = pl.program_id(1)
    q = q_ref[...]
    rows = q.shape[0]
    n_kv = valid_ref.shape[1] // e_ref.shape[0]
    valid = valid_ref[...]
    page_len = e_ref.shape[0]
    qpos = past_len + (lax.broadcasted_iota(I32, (rows, 1), 0) % (rows // n_kv)) // KV_GROUP

    @pl.when(p == 0)
    def _():
        tail_ref[...] = tail_in_ref[...]
        acc_ref[...] = acc_in_ref[...]

    def pages(krefs, vrefs, starts):
        zfs = [lax.dot_general(q, _flat_bf16(kref), _NT, preferred_element_type=F32) * valid for kref in krefs]
        zs = []
        for zf in zfs:
            hi, lo = _split_bf16(zf)
            zz = jnp.dot(jnp.concatenate([hi, lo], axis=0), c_ref[...], preferred_element_type=F32)
            zs.append(zz[:rows] + zz[rows:])
        parts = []
        tail = tail_ref[...]
        for z, start in zip(zs, starts):
            mask = (start + lax.broadcasted_iota(I32, (rows, page_len), 1)) < qpos
            sp = _softplus2(z)
            lk = jnp.where(mask, -sp, 0.0)
            hi, lo = _split_bf16(lk)
            ll = jnp.dot(jnp.concatenate([hi, lo], axis=0), u_ref[...], preferred_element_type=F32)
            parts.append((mask, z - sp + tail, ll))
            tail = tail + jnp.sum(lk, axis=1, keepdims=True)
        tail_ref[...] = tail
        acc = acc_ref[...]
        for (mask, base, ll), vref in zip(parts, vrefs):
            a = jnp.where(mask, jnp.exp2(base + ll[:rows] + ll[rows:]), 0.0).astype(BF16)
            ae = (jnp.dot(a, e_ref[...], preferred_element_type=F32) * valid).astype(BF16)
            acc = acc + jnp.dot(ae, _flat_bf16(vref), preferred_element_type=F32)
        acc_ref[...] = acc

    if with_new:
        @pl.when(p == 0)
        def _():
            pages([kn_ref], [vn_ref], [past_len])

    @pl.when(p >= int(with_new))
    def _():
        pages(kpages, vpages, [(top - 1 - ((p - int(with_new)) * pp + j)) * page_len for j in range(pp)])


def _sb_sample_part(cache_k, cache_v, layer, page_table, q_rows, k_new_pad, v_new_pad, tail, acc, pp, top, count, with_new):
    db, n_pages = page_table.shape
    page, n_kv = cache_k.shape[2], cache_k.shape[3]
    rows = q_rows.shape[1]
    first = int(with_new)
    expand, compact, valid = _page_consts(page, n_kv, rows)

    def cache_spec(j):
        return pl.BlockSpec((None, None, page, n_kv, HEAD_DIM),
                            lambda b, p, pt: (layer, pt[b, top - 1 - (jnp.maximum(p - first, 0) * pp + j)], 0, 0, 0))

    new_spec = pl.BlockSpec((None, page, n_kv, HEAD_DIM), lambda b, p, pt: (b, 0, 0, 0))
    const = lambda a: pl.BlockSpec(a.shape, lambda b, p, pt: (0, 0))
    per_b = lambda a: pl.BlockSpec((None,) + a.shape[1:], lambda b, p, pt: (b, 0, 0))
    u = _suffix_matrix(page)
    return pl.pallas_call(
        functools.partial(_sb_sample_kernel, pp=pp, top=top, with_new=with_new, past_len=n_pages * page),
        grid_spec=pltpu.PrefetchScalarGridSpec(
            num_scalar_prefetch=1,
            grid=(db, count // pp + first),
            in_specs=[per_b(q_rows), const(expand), const(compact), const(valid), const(u),
                      new_spec, new_spec, per_b(tail), per_b(acc)] + [cache_spec(j) for j in range(pp)] * 2,
            out_specs=[per_b(tail), per_b(acc)]),
        out_shape=[jax.ShapeDtypeStruct(tail.shape, F32), jax.ShapeDtypeStruct(acc.shape, F32)],
        compiler_params=_cparams(("arbitrary", "arbitrary")),
        name="sb_sample",
    )(page_table, q_rows, expand, compact, valid, u, k_new_pad, v_new_pad, tail, acc,
      *([cache_k] * pp), *([cache_v] * pp))


def _sb_sample(cache_k, cache_v, layer, page_table, q_rows, k_new_pad, v_new_pad, pp):
    db, n_pages = page_table.shape
    rows = q_rows.shape[1]
    args = (cache_k, cache_v, layer, page_table, q_rows, k_new_pad, v_new_pad)
    state = (jnp.zeros((db, rows, 1), F32), jnp.zeros((db, rows, HEAD_DIM), F32))
    state = tuple(_sb_sample_part(*args, *state, pp, n_pages, pp, True))
    rest = n_pages - pp
    if rest > 0:
        state = lax.cond(jnp.max(state[0]) > EXP2_UNDERFLOW,
                         lambda s: tuple(_sb_sample_part(*args, *s, pp, rest, rest, False)), lambda s: s, state)
    return state[1].astype(BF16)


def _rope_tables(pos):
    half = HEAD_DIM // 2
    inv = np.float32(ROPE_THETA) ** (-np.arange(half, dtype=np.float32) / np.float32(half))
    ang = pos.astype(np.float32)[:, None] * inv[None, :].astype(np.float32)
    cos, sin = np.cos(ang), np.sin(ang)
    return jnp.asarray(np.concatenate([cos, cos], axis=1)), jnp.asarray(np.concatenate([-sin, sin], axis=1))


def _pick_tile(n, prefs):
    for t in prefs:
        if n % t == 0:
            return t
    return n


def _rows_to_heads(x, db, t, n_kv):
    return x.reshape(db, t, n_kv, KV_GROUP, HEAD_DIM).transpose(0, 2, 1, 3, 4).reshape(db, n_kv * t * KV_GROUP, HEAD_DIM)


def _heads_to_rows(x, db, t, n_kv):
    return x.reshape(db, n_kv, t, KV_GROUP, HEAD_DIM).transpose(0, 2, 1, 3, 4).reshape(db * t, n_kv * KV_GROUP * HEAD_DIM)


def _chunk_t(v, nb, s_len, ck):
    return v.reshape(nb, s_len // ck, ck, v.shape[1]).transpose(0, 1, 3, 2)


def _pad_page(x, page):
    return jnp.pad(x, [(0, 0), (0, page - x.shape[1])] + [(0, 0)] * (x.ndim - 2))


def kernel(x_prompt, x_sample, c_prompt, c_sample, cache_k, cache_v, cache_idx_k, page_table, norm_mix_g, norm_ffn_g,
           w_ada, b_ada, w_in_dsa, w_in_sb, w_out, w_gate, w_up, w_down, norm_final_g):
    nb, s_len, d = x_prompt.shape
    db, t_new, _ = x_sample.shape
    depth = w_ada.shape[0]
    page, n_kv = cache_k.shape[2], cache_k.shape[3]
    n_pages = page_table.shape[1]
    past_len = n_pages * page
    attn_w = w_out.shape[1]
    kv_w = n_kv * HEAD_DIM
    idx_w = IDX_HEADS * IDX_DIM
    mp, ms = nb * s_len, db * t_new

    tm = _pick_tile(s_len, (1024, 512, 256, 128))
    tm2 = _pick_tile(s_len, (512, 256, 128))
    tm3 = _pick_tile(s_len, (256, 128))
    w_out16 = _cast_bf16(w_out, _pick_tile(w_out.shape[1], (512, 256, 128)))
    tq_dsa, tq_sb, ck = 256, 256, 256
    ng = _pick_tile(n_kv, (4, 2, 1))
    pp = _pick_tile(n_pages, (8, 4, 2, 1))
    attn_scale = HEAD_DIM ** -0.5 * LOG2E

    n_c = nb + db
    c_all = jnp.pad(jnp.concatenate([c_prompt, c_sample], axis=0), ((0, (-n_c) % 16), (0, 0)))
    mod = _adaln(c_all, w_ada, b_ada).reshape(depth, c_all.shape[0], 6, d)

    cos_p, sin_p = _rope_tables(np.arange(s_len))
    cos_s, sin_s = _rope_tables(np.tile(past_len + np.arange(t_new), db))

    xp = x_prompt.reshape(mp, d)
    xs = x_sample.reshape(ms, d)
    outs = {n: [] for n in ("ikp", "ks", "vs", "iks")}
    k_all = jnp.zeros((depth, mp, kv_w), F32)
    v_all = jnp.zeros((depth, mp, kv_w), F32)

    for l in range(depth):
        mp_l = [mod[l, :nb, j].reshape(nb, 1, d) for j in range(6)]
        ms_l = [jnp.repeat(mod[l, nb:n_c, j], t_new, axis=0) for j in range(6)]
        hp = _norm(xp, norm_mix_g[l], mp_l[1], mp_l[0], tm, s_len)
        hs = _norm(xs, norm_mix_g[l], ms_l[1], ms_l[0], ms, 1)
        i = l // 2
        if l % 2 == 0:
            w = jnp.swapaxes(w_in_dsa, 1, 2)
            tail0 = attn_w + 2 * kv_w + idx_w
            w_tail = jnp.pad(w[i, tail0:, :], ((0, 2 * LANES - IDX_DIM - IDX_HEADS), (0, 0)))[None]
            tn = 1024
            hd = tn // HEAD_DIM
            (q,) = _proj(hp, w, i, 0, attn_w, cos_p, sin_p, hd, (BF16,), tm2, tn, attn_scale, w_rows=True)
            k_all, k16 = _proj(hp, w, i, attn_w, kv_w, cos_p, sin_p, hd, (F32, BF16), tm2, tn, w_rows=True,
                               stack=(k_all, l))
            v_all, v16 = _proj(hp, w, i, attn_w + kv_w, kv_w, cos_p, sin_p, 0, (F32, BF16), tm2, tn, w_rows=True,
                               stack=(v_all, l))
            (iq,) = _proj(hp, w, i, attn_w + 2 * kv_w, idx_w, cos_p, sin_p, hd, (F32,), tm2, tn, w_rows=True)
            (ikiw,) = _proj(hp, w_tail, 0, 0, 2 * LANES, cos_p, sin_p, 1, (F32,), tm2, 2 * LANES, w_rows=True)
            mix_p = _dsa_prompt(q, k16, _chunk_t(v16, nb, s_len, ck), iq, ikiw, ikiw[:, IDX_DIM:IDX_DIM + IDX_HEADS].T,
                                nb, s_len, tq_dsa, ck, ng)
            outs["ikp"].append(ikiw[:, :IDX_DIM].reshape(nb, s_len, IDX_DIM))

            (all_s,) = _proj(hs, w, i, 0, tail0, cos_s, sin_s, hd, (F32,), ms, tn, plain_tiles=((attn_w + kv_w) // tn,),
                             w_rows=True)
            (ikiw_s,) = _proj(hs, w_tail, 0, 0, 2 * LANES, cos_s, sin_s, 1, (F32,), ms, 2 * LANES, w_rows=True)
            qs = (all_s[:, :attn_w] * attn_scale).astype(BF16)
            ks32 = all_s[:, attn_w:attn_w + kv_w]
            vs32 = all_s[:, attn_w + kv_w:attn_w + 2 * kv_w]
            iqs = all_s[:, attn_w + 2 * kv_w:]
            iks = ikiw_s[:, :IDX_DIM].reshape(db, t_new, IDX_DIM)
            iws = ikiw_s[:, IDX_DIM:IDX_DIM + IDX_HEADS].reshape(db, t_new, IDX_HEADS)
            iq_rows = jnp.broadcast_to(iqs.reshape(db, t_new, IDX_HEADS, 1, IDX_DIM).transpose(0, 2, 1, 3, 4),
                                       (db, IDX_HEADS, t_new, KV_GROUP, IDX_DIM)).reshape(db, -1, IDX_DIM)
            w_rows = jnp.broadcast_to(iws.transpose(0, 2, 1)[..., None], (db, IDX_HEADS, t_new, KV_GROUP)).reshape(db, -1, 1)
            sc_past, sc_new = _idx_sample(cache_idx_k, i, page_table, iq_rows, w_rows, _pad_page(iks, page),
                                          _pick_tile(n_pages, (16, 8, 4, 2, 1)))
            scores = jnp.concatenate([sc_past, sc_new, jnp.zeros((db, sc_new.shape[1], (pp - 1) * page), F32)], axis=2)
            mask = _select_sample(scores, past_len, t_new)
            ks4 = ks32.reshape(db, t_new, n_kv, HEAD_DIM)
            vs4 = vs32.reshape(db, t_new, n_kv, HEAD_DIM)
            mix_s = _dsa_sample(cache_k, cache_v, l, page_table, _rows_to_heads(qs, db, t_new, n_kv), mask,
                                _pad_page(ks4, page), _pad_page(vs4, page), pp)
            mix_s = _heads_to_rows(mix_s, db, t_new, n_kv)
            outs["iks"].append(iks)
        else:
            w = w_in_sb
            tn = 1024
            (q,) = _proj(hp, w, i, 0, attn_w, cos_p, sin_p, 0, (BF16,), tm2, tn, attn_scale)
            k_all, k16 = _proj(hp, w, i, attn_w, kv_w, cos_p, sin_p, 0, (F32, BF16), tm2, tn, stack=(k_all, l))
            v_all, v16 = _proj(hp, w, i, attn_w + kv_w, kv_w, cos_p, sin_p, 0, (F32, BF16), tm2, tn,
                               stack=(v_all, l))
            mix_p = _sb_prompt(q, k16, _chunk_t(v16, nb, s_len, ck), nb, s_len, tq_sb, ck, ng)
            (all_s,) = _proj(hs, w, i, 0, attn_w + 2 * kv_w, cos_s, sin_s, 0, (F32,), ms, tn)
            qs = (all_s[:, :attn_w] * attn_scale).astype(BF16)
            ks32 = all_s[:, attn_w:attn_w + kv_w]
            vs32 = all_s[:, attn_w + kv_w:]
            ks4 = ks32.reshape(db, t_new, n_kv, HEAD_DIM)
            vs4 = vs32.reshape(db, t_new, n_kv, HEAD_DIM)
            mix_s = _sb_sample(cache_k, cache_v, l, page_table, _rows_to_heads(qs, db, t_new, n_kv),
                               _pad_page(ks4, page), _pad_page(vs4, page), pp)
            mix_s = _heads_to_rows(mix_s, db, t_new, n_kv)
        outs["ks"].append(ks4)
        outs["vs"].append(vs4)

        xp, hp = _out_norm(mix_p, w_out16, l, xp, mp_l[2], norm_ffn_g[l], mp_l[4], mp_l[3], tm3, s_len)
        xs, hs = _out_norm(mix_s, w_out16, l, xs, ms_l[2], norm_ffn_g[l], ms_l[4], ms_l[3], ms, 1)
        tn_ff = _pick_tile(w_gate.shape[2], (512, 256, 128))
        act_p, act_s = _swiglu(hp, w_gate, w_up, l, tm, tn_ff, hs)
        xp, xs = _gres(act_p, w_down, l, xp, mp_l[5], tm2, 512, s_len, act_s, xs, ms_l[5])

    y_prompt = _final_norm(xp, norm_final_g, tm).reshape(nb, s_len, d)
    y_sample = _final_norm(xs, norm_final_g, ms).reshape(db, t_new, d)
    kv_shape = (depth, nb, s_len, n_kv, HEAD_DIM)
    return (y_prompt, y_sample, k_all.reshape(kv_shape), v_all.reshape(kv_shape), jnp.stack(outs["ikp"]),
            jnp.stack(outs["ks"]), jnp.stack(outs["vs"]), jnp.stack(outs["iks"]))
```
